```python
import math
import jax, jax.numpy as jnp
from jax import lax
import numpy as np

D_MODEL = 1024
BATCH = 2
SEQ = 8192
DEPTH = 4

N_MIXERS = 3
HEAD_DIM = 64
MIX_WIDTH = 768
N_MEM_HEADS = 4
MEM_WIDTH = N_MEM_HEADS * HEAD_DIM
N_MEM = 256
BRANCH = MIX_WIDTH + MEM_WIDTH
IN_WIDTH = 3 * MIX_WIDTH + MEM_WIDTH + BRANCH
RMS_EPS = 1e-6
SUBLN_EPS = 1e-5
NEG_INF = -1e30

DSW_GROUPS = ((128, 1), (512, 4), (2048, 16))
DSW_HEADS_PER_GROUP = 4
DSW_HEADS = DSW_HEADS_PER_GROUP * len(DSW_GROUPS)
DSW_GROUP_WIDTH = DSW_HEADS_PER_GROUP * HEAD_DIM
BAND_BLOCK = 128

MOBA_HEADS = MIX_WIDTH // HEAD_DIM
MOBA_BLOCK = 256
MOBA_TOPK = 3
MOBA_Q_CHUNK = 64

DIFF_HEADS = 6
DIFF_D = 64
DIFF_Q_BLOCK = 128
N_DIFF_LAYERS = len(range(2, DEPTH, N_MIXERS))

kernel_name = "hybrid_dilated_moba_diff_trunk"


def alibi_slopes(n):
    return jnp.asarray(2.0 ** (-8.0 * np.arange(1, n + 1) / n), dtype=jnp.float32)


def rmsnorm(x, g, eps=RMS_EPS):
    xf = x.astype(jnp.float32)
    y = xf * lax.rsqrt(jnp.mean(xf * xf, axis=-1, keepdims=True) + eps)
    return (y * g.astype(jnp.float32)).astype(x.dtype)


def split_heads(t, n):
    b, s, _ = t.shape
    return t.reshape(b, s, n, -1).transpose(0, 2, 1, 3)


def merge_heads(t):
    b, h, s, d = t.shape
    return t.transpose(0, 2, 1, 3).reshape(b, s, h * d)


def banded_causal_attention(q, k, v, span, slopes):
    b, h, L, hd = q.shape
    nb = -(-L // BAND_BLOCK)
    Lp = nb * BAND_BLOCK
    pad = ((0, 0), (0, 0), (0, Lp - L), (0, 0))
    qb = jnp.pad(q, pad).reshape(b, h, nb, BAND_BLOCK, hd)
    kb = jnp.pad(k, pad).reshape(b, h, nb, BAND_BLOCK, hd)
    vb = jnp.pad(v, pad).reshape(b, h, nb, BAND_BLOCK, hd)
    shift = ((0, 0), (0, 0), (1, 0), (0, 0), (0, 0))
    kk = jnp.concatenate([jnp.pad(kb, shift)[:, :, :-1], kb], axis=3)
    vv = jnp.concatenate([jnp.pad(vb, shift)[:, :, :-1], vb], axis=3)
    dist = (jnp.arange(BAND_BLOCK)[:, None] + BAND_BLOCK) - jnp.arange(2 * BAND_BLOCK)[None, :]
    kpos = jnp.arange(nb)[:, None] * BAND_BLOCK - BAND_BLOCK + jnp.arange(2 * BAND_BLOCK)[None, :]
    valid = ((dist >= 0) & (dist <= span))[None, :, :] & (kpos >= 0)[:, None, :]
    s = jnp.einsum('bhnqd,bhnkd->bhnqk', qb, kk).astype(jnp.float32) * (hd ** -0.5)
    s = s - slopes[None, :, None, None, None] * dist.astype(jnp.float32)
    s = jnp.where(valid, s, NEG_INF)
    lse = jax.nn.logsumexp(s, axis=-1)
    p = jnp.exp(s - lse[..., None])
    out = jnp.einsum('bhnqk,bhnkd->bhnqd', p, vv.astype(jnp.float32))
    return out.reshape(b, h, Lp, hd)[:, :, :L], lse.reshape(b, h, Lp)[:, :, :L]


def dilated_window_mixer(q, k, v):
    b, S, _ = q.shape
    hg = DSW_HEADS_PER_GROUP
    slopes_all = alibi_slopes(DSW_HEADS)
    outs, lses = [], []
    for g, (window, dil) in enumerate(DSW_GROUPS):
        cols = slice(g * DSW_GROUP_WIDTH, (g + 1) * DSW_GROUP_WIDTH)
        U = S // dil

        def to_residue(t):
            t = t.reshape(b, U, dil, hg, HEAD_DIM).transpose(0, 3, 2, 1, 4)
            return t.reshape(b, hg * dil, U, HEAD_DIM)

        slopes = jnp.repeat(slopes_all[g * hg:(g + 1) * hg] * dil, dil)
        o, lse = banded_causal_attention(to_residue(q[..., cols]), to_residue(k[..., cols]),
                                         to_residue(v[..., cols]), window // dil, slopes)
        o = o.reshape(b, hg, dil, U, HEAD_DIM).transpose(0, 3, 2, 1, 4).reshape(b, S, hg, HEAD_DIM)
        lse = lse.reshape(b, hg, dil, U).transpose(0, 3, 2, 1).reshape(b, S, hg)
        outs.append(o)
        lses.append(lse)
    alpha = jax.nn.softmax(jnp.stack(lses, axis=0), axis=0)
    mixed = [(outs[g] * alpha[g][..., None]).reshape(b, S, DSW_GROUP_WIDTH) for g in range(len(DSW_GROUPS))]
    return jnp.concatenate(mixed, axis=-1).astype(q.dtype)


def moba_mixer(q, k, v):
    b, S, _ = q.shape
    H = MOBA_HEADS
    slopes = alibi_slopes(H)
    nblk = -(-S // MOBA_BLOCK)
    Sp = nblk * MOBA_BLOCK
    n_gate = max(nblk, MOBA_TOPK)
    pad = ((0, 0), (0, 0), (0, Sp - S), (0, 0))
    qh = jnp.pad(split_heads(q, H), pad)
    kb = jnp.pad(split_heads(k, H), pad).reshape(b, H, nblk, MOBA_BLOCK, HEAD_DIM)
    vb = jnp.pad(split_heads(v, H), pad).reshape(b, H, nblk, MOBA_BLOCK, HEAD_DIM)
    kmean = jnp.mean(kb.astype(jnp.float32), axis=3)
    nchunk = Sp // MOBA_Q_CHUNK
    qc = qh.reshape(b, H, nchunk, MOBA_Q_CHUNK, HEAD_DIM).transpose(2, 0, 1, 3, 4)
    bi = jnp.arange(b)[:, None, None, None]
    hi = jnp.arange(H)[None, :, None, None]
    scale = HEAD_DIM ** -0.5

    def one_chunk(args):
        c, qx = args
        t = c * MOBA_Q_CHUNK + jnp.arange(MOBA_Q_CHUNK)
        n = (c * MOBA_Q_CHUNK) // MOBA_BLOCK
        gate = jnp.einsum('bhqd,bhnd->bhqn', qx.astype(jnp.float32), kmean)
        gate = jnp.pad(gate, ((0, 0), (0, 0), (0, 0), (0, n_gate - nblk)), constant_values=NEG_INF)
        gate = jnp.where(jnp.arange(n_gate) < n, gate, NEG_INF)
        _, idx = lax.top_k(gate, MOBA_TOPK)
        idx = jnp.minimum(idx, nblk - 1)
        sel_ok = jnp.arange(MOBA_TOPK) < n
        ks = kb[bi, hi, idx]
        vs = vb[bi, hi, idx]
        kpos_sel = idx[..., None] * MOBA_BLOCK + jnp.arange(MOBA_BLOCK)
        dist_sel = (t[None, None, :, None, None] - kpos_sel).astype(jnp.float32)
        s_sel = jnp.einsum('bhqd,bhqjkd->bhqjk', qx, ks).astype(jnp.float32) * scale
        s_sel = s_sel - slopes[None, :, None, None, None] * dist_sel
        s_sel = jnp.where(sel_ok[:, None], s_sel, NEG_INF)
        k_own = lax.dynamic_index_in_dim(kb, n, axis=2, keepdims=False)
        v_own = lax.dynamic_index_in_dim(vb, n, axis=2, keepdims=False)
        dist_own = t[:, None] - (n * MOBA_BLOCK + jnp.arange(MOBA_BLOCK))[None, :]
        s_own = jnp.einsum('bhqd,bhkd->bhqk', qx, k_own).astype(jnp.float32) * scale
        s_own = jnp.where(dist_own >= 0,
                          s_own - slopes[None, :, None, None] * dist_own.astype(jnp.float32), NEG_INF)
        s_all = jnp.concatenate([s_sel.reshape(b, H, MOBA_Q_CHUNK, MOBA_TOPK * MOBA_BLOCK), s_own], axis=-1)
        p = jax.nn.softmax(s_all, axis=-1)
        p_sel = p[..., :MOBA_TOPK * MOBA_BLOCK].reshape(b, H, MOBA_Q_CHUNK, MOBA_TOPK, MOBA_BLOCK)
        p_own = p[..., MOBA_TOPK * MOBA_BLOCK:]
        return (jnp.einsum('bhqjk,bhqjkd->bhqd', p_sel, vs.astype(jnp.float32))
                + jnp.einsum('bhqk,bhkd->bhqd', p_own, v_own.astype(jnp.float32)))

    o = lax.map(one_chunk, (jnp.arange(nchunk), qc))
    o = o.transpose(1, 2, 0, 3, 4).reshape(b, H, Sp, HEAD_DIM)[:, :, :S]
    return merge_heads(o).astype(q.dtype)


def diff_mixer(q, k, v, lq1, lk1, lq2, lk2, subln_g, lambda_init):
    b, S, _ = q.shape
    H = DIFF_HEADS
    slopes = alibi_slopes(H)
    qh = q.reshape(b, S, H, 2, DIFF_D).transpose(0, 2, 3, 1, 4)
    kh = k.reshape(b, S, H, 2, DIFF_D).transpose(0, 2, 3, 1, 4)
    vh = split_heads(v, H).astype(jnp.float32)
    lam = (jnp.exp(jnp.sum(lq1.astype(jnp.float32) * lk1.astype(jnp.float32)))
           - jnp.exp(jnp.sum(lq2.astype(jnp.float32) * lk2.astype(jnp.float32))) + lambda_init)
    nqb = S // DIFF_Q_BLOCK
    qblocks = qh.reshape(b, H, 2, nqb, DIFF_Q_BLOCK, DIFF_D).transpose(3, 0, 1, 2, 4, 5)
    kpos = jnp.arange(S)

    def one_block(args):
        j, qx = args
        t = j * DIFF_Q_BLOCK + jnp.arange(DIFF_Q_BLOCK)
        dist = t[:, None] - kpos[None, :]
        s = jnp.einsum('bhmqd,bhmkd->bhmqk', qx, kh).astype(jnp.float32) * (DIFF_D ** -0.5)
        s = jnp.where(dist >= 0, s - slopes[None, :, None, None, None] * dist.astype(jnp.float32), NEG_INF)
        p = jax.nn.softmax(s, axis=-1)
        w = p[:, :, 0] - lam * p[:, :, 1]
        return jnp.einsum('bhqk,bhkd->bhqd', w, vh)

    o = lax.map(one_block, (jnp.arange(nqb), qblocks))
    o = o.transpose(1, 2, 0, 3, 4).reshape(b, H, S, 2 * DIFF_D)
    o = rmsnorm(o, subln_g, SUBLN_EPS) * (1.0 - lambda_init)
    return merge_heads(o).astype(q.dtype)


def memory_attention(qm, mem_n, w_mem_kv):
    kv = mem_n @ w_mem_kv
    km = split_heads(kv[..., :MEM_WIDTH], N_MEM_HEADS)
    vm = split_heads(kv[..., MEM_WIDTH:], N_MEM_HEADS)
    qh = split_heads(qm, N_MEM_HEADS)
    s = jnp.einsum('bhqd,bhkd->bhqk', qh, km).astype(jnp.float32) * (HEAD_DIM ** -0.5)
    p = jax.nn.softmax(s, axis=-1)
    return merge_heads(jnp.einsum('bhqk,bhkd->bhqd', p, vm.astype(jnp.float32))).astype(qm.dtype)


def setup_inputs(seed: int = 0) -> dict:
    key = jax.random.key(seed)
    ks = jax.random.split(key, 14)
    f32 = jnp.float32
    nd = max(N_DIFF_LAYERS, 0)
    return {
        "x": jax.random.normal(ks[0], (BATCH, SEQ, D_MODEL), f32),
        "mem": jax.random.normal(ks[1], (BATCH, N_MEM, D_MODEL), f32),
        "norm_g": 1.0 + 0.02 * jax.random.normal(ks[2], (DEPTH, D_MODEL), f32),
        "w_in": jax.random.normal(ks[3], (DEPTH, D_MODEL, IN_WIDTH), f32) * D_MODEL ** -0.5,
        "w_out": jax.random.normal(ks[4], (DEPTH, BRANCH, D_MODEL), f32) * BRANCH ** -0.5,
        "mem_norm_g": 1.0 + 0.02 * jax.random.normal(ks[5], (DEPTH, D_MODEL), f32),
        "w_mem_kv": jax.random.normal(ks[6], (DEPTH, D_MODEL, 2 * MEM_WIDTH), f32) * D_MODEL ** -0.5,
        "diff_lambda_q1": 0.1 * jax.random.normal(ks[7], (nd, DIFF_D), f32),
        "diff_lambda_k1": 0.1 * jax.random.normal(ks[8], (nd, DIFF_D), f32),
        "diff_lambda_q2": 0.1 * jax.random.normal(ks[9], (nd, DIFF_D), f32),
        "diff_lambda_k2": 0.1 * jax.random.normal(ks[10], (nd, DIFF_D), f32),
        "diff_subln_g": 1.0 + 0.02 * jax.random.normal(ks[11], (nd, 2 * DIFF_D), f32),
        "final_norm_g": 1.0 + 0.02 * jax.random.normal(ks[12], (D_MODEL,), f32),
    }


def reference(x, mem, norm_g, w_in, w_out, mem_norm_g, w_mem_kv, diff_lambda_q1, diff_lambda_k1,
              diff_lambda_q2, diff_lambda_k2, diff_subln_g, final_norm_g):
    c1, c2, c3, c4 = MIX_WIDTH, 2 * MIX_WIDTH, 3 * MIX_WIDTH, 3 * MIX_WIDTH + MEM_WIDTH
    for i in range(DEPTH):
        h = rmsnorm(x, norm_g[i])
        proj = h @ w_in[i]
        q, k, v = proj[..., :c1], proj[..., c1:c2], proj[..., c2:c3]
        qm, gate = proj[..., c3:c4], proj[..., c4:]
        kind = i % N_MIXERS
        if kind == 0:
            mix = dilated_window_mixer(q, k, v)
        elif kind == 1:
            mix = moba_mixer(q, k, v)
        else:
            c = i // N_MIXERS
            lambda_init = 0.8 - 0.6 * math.exp(-0.3 * i)
            mix = diff_mixer(q, k, v, diff_lambda_q1[c], diff_lambda_k1[c], diff_lambda_q2[c],
                             diff_lambda_k2[c], diff_subln_g[c], lambda_init)
        mo = memory_attention(qm, rmsnorm(mem, mem_norm_g[i]), w_mem_kv[i])
        y = jnp.concatenate([mix, mo], axis=-1) * jax.nn.silu(gate)
        x = x + y @ w_out[i]
    return rmsnorm(x, final_norm_g)
```

```python
import functools
import math

import numpy as np
import jax
import jax.numpy as jnp
from jax import lax
from jax.experimental import pallas as pl
from jax.experimental.pallas import tpu as pltpu

D_MODEL = 1024
DEPTH = 4
N_MIXERS = 3
HEAD_DIM = 64
MIX_WIDTH = 768
N_MEM_HEADS = 4
MEM_WIDTH = N_MEM_HEADS * HEAD_DIM
N_MEM = 256
BRANCH = MIX_WIDTH + MEM_WIDTH
IN_WIDTH = 3 * MIX_WIDTH + MEM_WIDTH + BRANCH
QKVM_WIDTH = 3 * MIX_WIDTH + MEM_WIDTH
RMS_EPS = 1e-6
SUBLN_EPS = 1e-5
NEG_INF = -1e30

DSW_GROUPS = ((128, 1), (512, 4), (2048, 16))
DSW_HEADS = 12
DSW_GROUP_WIDTH = 256
BAND_BLOCK = 128

MOBA_HEADS = 12
MOBA_BLOCK = 256
MOBA_TOPK = 3

DIFF_HEADS = 6
DIFF_D = 64

LANES = 128
VMEM_LIMIT_BYTES = 56 * 1024 * 1024

BF16 = jnp.bfloat16
F32 = jnp.float32


def _alibi_slopes(n):
    return np.asarray(2.0 ** (-8.0 * np.arange(1, n + 1) / n), dtype=np.float32)


def _dot(a, b):
    return jnp.dot(a, b, preferred_element_type=F32)


def _dot_nt(a, b):
    return lax.dot_general(a, b, (((1,), (1,)), ((), ())), preferred_element_type=F32)


def _compiler_params(n_axes):
    return pltpu.CompilerParams(dimension_semantics=("parallel",) * n_axes,
                                vmem_limit_bytes=VMEM_LIMIT_BYTES)


def _rms_proj_kernel(x_ref, g_ref, w_ref, *out_refs, splits, n_chunk, kmean_cols):
    x = x_ref[...].astype(F32)
    y = x * lax.rsqrt(jnp.mean(x * x, axis=-1, keepdims=True) + RMS_EPS)
    h = (y * g_ref[...]).astype(BF16)
    tm = x.shape[0]
    for (lo, hi), o_ref in zip(splits, out_refs):
        for c in range(lo, hi, n_chunk):
            acc = _dot(h, w_ref[:, c:c + n_chunk])
            o_ref[:, c - lo:c - lo + n_chunk] = acc.astype(o_ref.dtype)
            if kmean_cols is not None and kmean_cols[0] <= c < kmean_cols[1]:
                km_ref = out_refs[-1]
                for r in range(tm // MOBA_BLOCK):
                    blk = acc[r * MOBA_BLOCK:(r + 1) * MOBA_BLOCK]
                    km_ref[r, :, c - kmean_cols[0]:c - kmean_cols[0] + n_chunk] = (
                        jnp.mean(blk, axis=0, keepdims=True))


def _rms_proj(x, g, w, splits, dtypes, *, tm, n_chunk=256, kmean_cols=None):
    T, D = x.shape
    N = w.shape[1]
    out_shape = [jax.ShapeDtypeStruct((T, hi - lo), dt) for (lo, hi), dt in zip(splits, dtypes)]
    out_specs = [pl.BlockSpec((tm, hi - lo), lambda i: (i, 0)) for (lo, hi) in splits]
    if kmean_cols is not None:
        kw = kmean_cols[1] - kmean_cols[0]
        out_shape.append(jax.ShapeDtypeStruct((T // MOBA_BLOCK, 1, kw), F32))
        out_specs.append(pl.BlockSpec((tm // MOBA_BLOCK, 1, kw), lambda i: (i, 0, 0)))
    kern = functools.partial(_rms_proj_kernel, splits=tuple(splits), n_chunk=n_chunk,
                             kmean_cols=kmean_cols)
    return pl.pallas_call(
        kern,
        grid=(T // tm,),
        in_specs=[pl.BlockSpec((tm, D), lambda i: (i, 0)),
                  pl.BlockSpec((1, D), lambda i: (0, 0)),
                  pl.BlockSpec((D, N), lambda i: (0, 0))],
        out_specs=out_specs,
        out_shape=out_shape,
        compiler_params=_compiler_params(1),
        name="rms_proj",
    )(x, g, w)


def _dilated_kernel(q_ref, kc_ref, kp_ref, vc_ref, vp_ref, o_ref, lse_ref, *, tu, slopes):
    ui = pl.program_id(2)
    bb = BAND_BLOCK
    row = lax.broadcasted_iota(jnp.int32, (bb, bb), 0)
    col = lax.broadcasted_iota(jnp.int32, (bb, bb), 1)
    d_cur = (row - col).astype(F32)
    d_prev = d_cur + float(bb)
    valid_cur = row >= col
    valid_prev_static = col >= row
    lane = lax.broadcasted_iota(jnp.int32, (bb, LANES), 1)
    for pair in range(2):
        cs = slice(pair * LANES, (pair + 1) * LANES)
        for s in range(tu // bb):
            rs = slice(s * bb, (s + 1) * bb)
            qs = q_ref[rs, cs]
            k_cur = kc_ref[rs, cs]
            v_cur = vc_ref[rs, cs]
            if s == 0:
                k_prev = kp_ref[:, cs]
                v_prev = vp_ref[:, cs]
                valid_prev = jnp.logical_and(valid_prev_static, ui > 0)
            else:
                ps = slice((s - 1) * bb, s * bb)
                k_prev = kc_ref[ps, cs]
                v_prev = vc_ref[ps, cs]
                valid_prev = valid_prev_static
            o_h, lse_h = [], []
            for hh in range(2):
                own = (lane >= hh * HEAD_DIM) & (lane < (hh + 1) * HEAD_DIM)
                slope = slopes[pair * 2 + hh]
                qh = jnp.where(own, qs, jnp.zeros_like(qs))
                s_cur = jnp.where(valid_cur, _dot_nt(qh, k_cur) - slope * d_cur, NEG_INF)
                s_prev = jnp.where(valid_prev, _dot_nt(qh, k_prev) - slope * d_prev, NEG_INF)
                m = jnp.maximum(jnp.max(s_cur, axis=1, keepdims=True),
                                jnp.max(s_prev, axis=1, keepdims=True))
                p_cur = jnp.exp(s_cur - m)
                p_prev = jnp.exp(s_prev - m)
                l = jnp.sum(p_cur, axis=1, keepdims=True) + jnp.sum(p_prev, axis=1, keepdims=True)
                acc = _dot(p_cur.astype(BF16), v_cur) + _dot(p_prev.astype(BF16), v_prev)
                o_h.append(acc / l)
                lse_h.append(m + jnp.log(l))
            first = lane < HEAD_DIM
            o_ref[rs, cs] = jnp.where(first, o_h[0], o_h[1])
            lse_ref[rs, cs] = jnp.where(first, lse_h[0], lse_h[1])


def _dilated_group(qkvm, g, *, batch, seq, tu=512):
    window, dil = DSW_GROUPS[g]
    assert window // dil == BAND_BLOCK
    U = seq // dil
    tu = min(tu, U)
    gw = DSW_GROUP_WIDTH
    row_w = QKVM_WIDTH // gw
    x = qkvm.reshape(batch, U, dil * QKVM_WIDTH)
    sub = tu // BAND_BLOCK
    slopes = tuple(float(v) for v in _alibi_slopes(DSW_HEADS)[g * 4:(g + 1) * 4] * np.float32(dil))

    def cur(off):
        return pl.BlockSpec((None, tu, gw), lambda b, r, u: (b, u, r * row_w + off + g))

    def prev(off):
        return pl.BlockSpec((None, BAND_BLOCK, gw),
                            lambda b, r, u: (b, jnp.maximum(u * sub - 1, 0), r * row_w + off + g))

    out_spec = pl.BlockSpec((None, tu, gw), lambda b, r, u: (b, u, r))
    o, lse = pl.pallas_call(
        functools.partial(_dilated_kernel, tu=tu, slopes=slopes),
        grid=(batch, dil, U // tu),
        in_specs=[cur(0), cur(3), prev(3), cur(6), prev(6)],
        out_specs=[out_spec, out_spec],
        out_shape=[jax.ShapeDtypeStruct((batch, U, dil * gw), F32)] * 2,
        compiler_params=_compiler_params(3),
        name=f"dilated_g{g}",
    )(x, x, x, x, x)
    return o.reshape(batch * seq, gw), lse.reshape(batch * seq, gw)


_SEL_LANES = 32
_ALIBI_PIECES = 3


def _bf16_pieces(v, n):
    out, rest = [], np.float32(v)
    for _ in range(n):
        piece = np.asarray(rest, np.float32).astype(jnp.bfloat16).astype(np.float32)
        out.append(float(piece))
        rest = np.float32(rest - piece)
    assert rest == 0.0, "slope does not split exactly into bf16 pieces"
    return out


def _moba_prep_kernel(const_ref, q_ref, k_ref, km_ref, qa_ref, ka_ref, *, nblk):
    n = pl.program_id(2)
    tq = q_ref.shape[0]
    q2 = q_ref[...]
    k2 = k_ref[...]
    lane = lax.broadcasted_iota(jnp.int32, (tq, LANES), 1)
    rowi = lax.broadcasted_iota(jnp.int32, (tq, LANES), 0)
    km_rows = lax.broadcasted_iota(jnp.int32, (LANES, LANES), 0)
    km_lane = lax.broadcasted_iota(jnp.int32, (LANES, LANES), 1)
    km = km_ref[...]
    for hh in range(2):
        own_lo = hh * HEAD_DIM
        off = (1 - hh) * HEAD_DIM
        own = (lane >= own_lo) & (lane < own_lo + HEAD_DIM)
        pieces = []
        if off > 0:
            pieces.append(jnp.zeros((off, LANES), F32))
        pieces.append(km)
        if LANES - off - nblk > 0:
            pieces.append(jnp.zeros((LANES - off - nblk, LANES), F32))
        km_full = jnp.concatenate(pieces, axis=0)
        km_own = (km_lane >= own_lo) & (km_lane < own_lo + HEAD_DIM)
        km_full = jnp.where(km_own, km_full, 0.0).astype(BF16)
        qh = jnp.where(own, q2, jnp.zeros_like(q2))
        gate = _dot_nt(qh, km_full)
        blk = lane - off
        cand = (blk >= 0) & (blk < n)
        lane_f = lane.astype(F32)
        g = jnp.where(cand, gate, -jnp.inf)
        sel = jnp.zeros((tq, LANES), jnp.bool_)
        for _ in range(MOBA_TOPK):
            mx = jnp.max(g, axis=1, keepdims=True)
            hit = (g == mx) & (mx > -jnp.inf)
            idx = jnp.min(jnp.where(hit, lane_f, float(LANES)), axis=1, keepdims=True)
            pick = lane_f == idx
            sel = sel | pick
            g = jnp.where(pick, -jnp.inf, g)
        is_sel_lane = (blk >= 0) & (blk < _SEL_LANES)
        sel_val = jnp.where(sel | (blk == n), 0.0, NEG_INF)
        q_extra = jnp.where(is_sel_lane, sel_val, const_ref[hh:hh + 1, :])
        qa = jnp.where(own, q2.astype(F32), q_extra)
        qa_ref[:, hh * LANES:(hh + 1) * LANES] = qa.astype(BF16)
        a0 = off + _SEL_LANES
        is_alibi = (lane >= a0) & (lane < a0 + _ALIBI_PIECES)
        k_extra = jnp.where(blk == n, 1.0, jnp.where(is_alibi, rowi.astype(F32), 0.0))
        ka = jnp.where(own, k2.astype(F32), k_extra)
        ka_ref[:, hh * LANES:(hh + 1) * LANES] = ka.astype(BF16)


def _moba_kernel(slope_ref, qa_ref, ka_ref, v_ref, o_ref):
    n = pl.program_id(2)
    tq = qa_ref.shape[0]
    tk = MOBA_BLOCK
    row = lax.broadcasted_iota(jnp.int32, (tq, tk), 0)
    col = lax.broadcasted_iota(jnp.int32, (tq, tk), 1)
    causal = row >= col
    lane = lax.broadcasted_iota(jnp.int32, (tq, LANES), 1)
    outs = []
    for hh in range(2):
        cs = slice(hh * LANES, (hh + 1) * LANES)
        qa = qa_ref[:, cs]
        blk_slope = slope_ref[2 * pl.program_id(1) + hh] * float(tk)

        own_rows = pl.ds(pl.multiple_of(n * tk, tk), tk)
        s = jnp.where(causal, _dot_nt(qa, ka_ref[own_rows, cs]), NEG_INF)
        m0 = jnp.max(s, axis=1, keepdims=True)
        p = jnp.exp(s - m0)
        l0 = jnp.sum(p, axis=1, keepdims=True)
        acc0 = _dot(p.astype(BF16), v_ref[own_rows, :])

        def body(j, carry, qa=qa, cs=cs, blk_slope=blk_slope):
            m, l, acc = carry
            rows = pl.ds(pl.multiple_of(j * tk, tk), tk)
            shift = blk_slope * (j - n).astype(F32)
            s = _dot_nt(qa, ka_ref[rows, cs])
            m_new = jnp.maximum(m, jnp.max(s, axis=1, keepdims=True) + shift)
            p = jnp.exp(s - (m_new - shift))
            alpha = jnp.exp(m - m_new)
            l = alpha * l + jnp.sum(p, axis=1, keepdims=True)
            acc = alpha * acc + _dot(p.astype(BF16), v_ref[rows, :])
            return m_new, l, acc

        m, l, acc = lax.fori_loop(0, n, body, (m0, l0, acc0))
        outs.append(acc / l)
    o_ref[...] = jnp.where(lane < HEAD_DIM, outs[0], outs[1])


def _moba_mixer(qkvm, kmean, *, batch, seq):
    nblk = seq // MOBA_BLOCK
    assert nblk <= _SEL_LANES
    npair = MOBA_HEADS // 2
    x = qkvm.reshape(batch, seq, QKVM_WIDTH)
    km = kmean.reshape(batch, nblk, MIX_WIDTH)
    slopes = _alibi_slopes(MOBA_HEADS)
    q_const = np.zeros((npair, 8, LANES), np.float32)
    for p in range(npair):
        for hh in range(2):
            a0 = (1 - hh) * HEAD_DIM + _SEL_LANES
            q_const[p, hh, a0:a0 + _ALIBI_PIECES] = _bf16_pieces(slopes[2 * p + hh], _ALIBI_PIECES)
    tq = MOBA_BLOCK
    kcol = MIX_WIDTH // LANES
    qa, ka = pl.pallas_call(
        functools.partial(_moba_prep_kernel, nblk=nblk),
        grid=(batch, npair, nblk),
        in_specs=[pl.BlockSpec((None, 8, LANES), lambda b, p, i: (p, 0, 0)),
                  pl.BlockSpec((None, tq, LANES), lambda b, p, i: (b, i, p)),
                  pl.BlockSpec((None, tq, LANES), lambda b, p, i: (b, i, kcol + p)),
                  pl.BlockSpec((None, nblk, LANES), lambda b, p, i: (b, 0, p))],
        out_specs=[pl.BlockSpec((None, tq, 2 * LANES), lambda b, p, i: (b, i, p))] * 2,
        out_shape=[jax.ShapeDtypeStruct((batch, seq, MOBA_HEADS * LANES), BF16)] * 2,
        compiler_params=_compiler_params(3),
        name="moba_prep",
    )(jnp.asarray(q_const), x, x, km)
    out = pl.pallas_call(
        _moba_kernel,
        grid=(batch, npair, nblk),
        in_specs=[pl.BlockSpec(memory_space=pltpu.SMEM),
                  pl.BlockSpec((None, tq, 2 * LANES), lambda b, p, i: (b, i, p)),
                  pl.BlockSpec((None, seq, 2 * LANES), lambda b, p, i: (b, 0, p)),
                  pl.BlockSpec((None, seq, LANES), lambda b, p, i: (b, 0, 2 * kcol + p))],
        out_specs=pl.BlockSpec((None, tq, LANES), lambda b, p, i: (b, i, p)),
        out_shape=jax.ShapeDtypeStruct((batch, seq, MIX_WIDTH), F32),
        compiler_params=_compiler_params(3),
        name="moba_attn",
    )(jnp.asarray(slopes), qa, ka, x)
    return out.reshape(batch * seq, MIX_WIDTH)


def _diff_kernel(slope_ref, lam_ref, g_ref, q_ref, k_ref, v_ref, o_ref, *, tk, lambda_init):
    i = pl.program_id(2)
    tq = q_ref.shape[0]
    sub = tq // tk
    q2 = q_ref[...]
    lane = lax.broadcasted_iota(jnp.int32, (tq, LANES), 1)
    zero = jnp.zeros_like(q2)
    q_maps = (jnp.where(lane < DIFF_D, q2, zero), jnp.where(lane >= DIFF_D, q2, zero))
    slope = slope_ref[pl.program_id(1)]
    col = lax.broadcasted_iota(jnp.int32, (1, tk), 1)
    row2 = lax.broadcasted_iota(jnp.int32, (tq, tk), 0)
    col2 = lax.broadcasted_iota(jnp.int32, (tq, tk), 1)

    def step(j, carry, masked):
        rows = pl.ds(pl.multiple_of(j * tk, tk), tk)
        kb = k_ref[rows, :]
        vb = v_ref[rows, :]
        bias = slope * (col + (j * tk - i * tq)).astype(F32)
        new = []
        for mi in range(2):
            m, l, acc = carry[mi]
            s = _dot_nt(q_maps[mi], kb) + bias
            if masked:
                s = jnp.where(row2 + i * tq >= col2 + j * tk, s, NEG_INF)
            m_new = jnp.maximum(m, jnp.max(s, axis=1, keepdims=True))
            p = jnp.exp(s - m_new)
            alpha = jnp.exp(m - m_new)
            l = alpha * l + jnp.sum(p, axis=1, keepdims=True)
            acc = alpha * acc + _dot(p.astype(BF16), vb)
            new.append((m_new, l, acc))
        return tuple(new)

    init = tuple((jnp.full((tq, 1), NEG_INF, F32), jnp.zeros((tq, 1), F32),
                  jnp.zeros((tq, LANES), F32)) for _ in range(2))
    carry = lax.fori_loop(0, i * sub, lambda j, c: step(j, c, False), init)
    for d in range(sub):
        carry = step(i * sub + d, carry, True)
    (_, l1, acc1), (_, l2, acc2) = carry

    lq1, lk1, lq2, lk2 = (lam_ref[r:r + 1, :] for r in range(4))
    lam = (jnp.exp(jnp.sum(lq1 * lk1, axis=1, keepdims=True))
           - jnp.exp(jnp.sum(lq2 * lk2, axis=1, keepdims=True)) + lambda_init)
    o = acc1 / l1 - lam * (acc2 / l2)
    y = o * lax.rsqrt(jnp.mean(o * o, axis=-1, keepdims=True) + SUBLN_EPS)
    o_ref[...] = (y * g_ref[...]) * (1.0 - lambda_init)


def _diff_mixer(qkvm, lam_vecs, subln_g, lambda_init, *, batch, seq, tq=512, tk=256):
    x = qkvm.reshape(batch, seq, QKVM_WIDTH)
    slopes = _alibi_slopes(DIFF_HEADS)
    kcol = MIX_WIDTH // LANES
    out = pl.pallas_call(
        functools.partial(_diff_kernel, tk=tk, lambda_init=float(lambda_init)),
        grid=(batch, DIFF_HEADS, seq // tq),
        in_specs=[pl.BlockSpec(memory_space=pltpu.SMEM),
                  pl.BlockSpec((4, DIFF_D), lambda b, h, i: (0, 0)),
                  pl.BlockSpec((1, 2 * DIFF_D), lambda b, h, i: (0, 0)),
                  pl.BlockSpec((None, tq, LANES), lambda b, h, i: (b, i, h)),
                  pl.BlockSpec((None, seq, LANES), lambda b, h, i: (b, 0, kcol + h)),
                  pl.BlockSpec((None, seq, LANES), lambda b, h, i: (b, 0, 2 * kcol + h))],
        out_specs=pl.BlockSpec((None, tq, LANES), lambda b, h, i: (b, i, h)),
        out_shape=jax.ShapeDtypeStruct((batch, seq, MIX_WIDTH), F32),
        compiler_params=_compiler_params(3),
        name="diff_attn",
    )(jnp.asarray(slopes), lam_vecs, subln_g, x, x, x)
    return out.reshape(batch * seq, MIX_WIDTH)


def _post_kernel(*refs, n_mix, dilated, final):
    mix_refs = refs[:n_mix]
    x_ref, qm_ref, gate_ref, kv_ref, w_ref = refs[n_mix:n_mix + 5]
    rest = refs[n_mix + 5:]
    fg_ref = rest[0] if final else None
    o_ref = rest[-1]
    tm = x_ref.shape[0]

    if dilated:
        outs = [r[...] for r in mix_refs[:3]]
        lses = [r[...] for r in mix_refs[3:]]
        mx = jnp.maximum(jnp.maximum(lses[0], lses[1]), lses[2])
        es = [jnp.exp(v - mx) for v in lses]
        den = es[0] + es[1] + es[2]
        mix_parts = [o * (e / den) for o, e in zip(outs, es)]
    else:
        mix = mix_refs[0][...]
        mix_parts = [mix[:, g * DSW_GROUP_WIDTH:(g + 1) * DSW_GROUP_WIDTH] for g in range(3)]

    lane = lax.broadcasted_iota(jnp.int32, (tm, LANES), 1)
    mem_parts = []
    for pair in range(MEM_WIDTH // LANES):
        cs = slice(pair * LANES, (pair + 1) * LANES)
        q2 = qm_ref[:, cs]
        k2 = kv_ref[:, cs]
        v2 = kv_ref[:, MEM_WIDTH + pair * LANES:MEM_WIDTH + (pair + 1) * LANES]
        o_h = []
        for hh in range(2):
            own = (lane >= hh * HEAD_DIM) & (lane < (hh + 1) * HEAD_DIM)
            s = _dot_nt(jnp.where(own, q2, jnp.zeros_like(q2)), k2)
            p = jnp.exp(s - jnp.max(s, axis=1, keepdims=True))
            l = jnp.sum(p, axis=1, keepdims=True)
            o_h.append(_dot(p.astype(BF16), v2) / l)
        mem_parts.append(jnp.where(lane < HEAD_DIM, o_h[0], o_h[1]))

    acc = x_ref[...]
    parts = [(g * DSW_GROUP_WIDTH, mix_parts[g]) for g in range(3)]
    parts += [(MIX_WIDTH + p * LANES, mem_parts[p]) for p in range(len(mem_parts))]
    for c0, val in parts:
        w = val.shape[1]
        gt = gate_ref[:, c0:c0 + w]
        y = val * (gt * (1.0 / (1.0 + jnp.exp(-gt))))
        acc = acc + _dot(y.astype(BF16), w_ref[c0:c0 + w, :])
    if final:
        yn = acc * lax.rsqrt(jnp.mean(acc * acc, axis=-1, keepdims=True) + RMS_EPS)
        acc = yn * fg_ref[...]
    o_ref[...] = acc


def _post(mix_list, x, qkvm, gate, mem_kv, w_out, final_g, *, dilated, batch, seq, tm=512):
    T = x.shape[0]
    tiles_per_batch = seq // tm
    n_mix = len(mix_list)
    final = final_g is not None
    qm_block = (3 * MIX_WIDTH) // MEM_WIDTH
    in_specs = [pl.BlockSpec((tm, m.shape[1]), lambda i: (i, 0)) for m in mix_list]
    in_specs += [pl.BlockSpec((tm, D_MODEL), lambda i: (i, 0)),
                 pl.BlockSpec((tm, MEM_WIDTH), lambda i: (i, qm_block)),
                 pl.BlockSpec((tm, BRANCH), lambda i: (i, 0)),
                 pl.BlockSpec((None, N_MEM, 2 * MEM_WIDTH), lambda i: (i // tiles_per_batch, 0, 0)),
                 pl.BlockSpec((BRANCH, D_MODEL), lambda i: (0, 0))]
    args = list(mix_list) + [x, qkvm, gate, mem_kv, w_out]
    if final:
        in_specs.append(pl.BlockSpec((1, D_MODEL), lambda i: (0, 0)))
        args.append(final_g)
    return pl.pallas_call(
        functools.partial(_post_kernel, n_mix=n_mix, dilated=dilated, final=final),
        grid=(T // tm,),
        in_specs=in_specs,
        out_specs=pl.BlockSpec((tm, D_MODEL), lambda i: (i, 0)),
        out_shape=jax.ShapeDtypeStruct((T, D_MODEL), F32),
        compiler_params=_compiler_params(1),
        name="post",
    )(*args)


def kernel(x, mem, norm_g, w_in, w_out, mem_norm_g, w_mem_kv, diff_lambda_q1, diff_lambda_k1,
           diff_lambda_q2, diff_lambda_k2, diff_subln_g, final_norm_g):
    batch, seq, d = x.shape
    depth = w_in.shape[0]
    T = batch * seq
    xf = x.reshape(T, d)
    memf = mem.reshape(batch * mem.shape[1], d)

    scale = HEAD_DIM ** -0.5
    col_scale = np.ones((IN_WIDTH,), np.float32)
    col_scale[:MIX_WIDTH] = scale
    col_scale[3 * MIX_WIDTH:QKVM_WIDTH] = scale
    w_in_b = (w_in * col_scale).astype(BF16)
    w_out_b = w_out.astype(BF16)
    w_kv_b = w_mem_kv.astype(BF16)

    for i in range(depth):
        kind = i % N_MIXERS
        kmean_cols = (MIX_WIDTH, 2 * MIX_WIDTH) if kind == 1 else None
        proj = _rms_proj(xf, norm_g[i].reshape(1, d), w_in_b[i],
                         [(0, QKVM_WIDTH), (QKVM_WIDTH, IN_WIDTH)], [BF16, F32],
                         tm=512, kmean_cols=kmean_cols)
        qkvm, gate = proj[0], proj[1]
        (mem_kv,) = _rms_proj(memf, mem_norm_g[i].reshape(1, d), w_kv_b[i],
                              [(0, 2 * MEM_WIDTH)], [BF16], tm=N_MEM)
        mem_kv = mem_kv.reshape(batch, N_MEM, 2 * MEM_WIDTH)
        if kind == 0:
            res = [_dilated_group(qkvm, g, batch=batch, seq=seq) for g in range(len(DSW_GROUPS))]
            mix_list = [r[0] for r in res] + [r[1] for r in res]
        elif kind == 1:
            mix_list = [_moba_mixer(qkvm, proj[2], batch=batch, seq=seq)]
        else:
            c = i // N_MIXERS
            lambda_init = 0.8 - 0.6 * math.exp(-0.3 * i)
            lam_vecs = jnp.stack([diff_lambda_q1[c], diff_lambda_k1[c],
                                  diff_lambda_q2[c], diff_lambda_k2[c]]).astype(F32)
            mix_list = [_diff_mixer(qkvm, lam_vecs, diff_subln_g[c].reshape(1, 2 * DIFF_D),
                                    lambda_init, batch=batch, seq=seq)]
        fg = final_norm_g.reshape(1, d) if i == depth - 1 else None
        xf = _post(mix_list, xf, qkvm, gate, mem_kv, w_out_b[i], fg,
                   dilated=(kind == 0), batch=batch, seq=seq)
    return xf.reshape(batch, seq, d)
```

```python
import functools
import math

import numpy as np
import jax
import jax.numpy as jnp
from jax import lax
from jax.experimental import pallas as pl
from jax.experimental.pallas import tpu as pltpu

D_MODEL = 1024
DEPTH = 4
N_MIXERS = 3
HEAD_DIM = 64
MIX_WIDTH = 768
N_MEM_HEADS = 4
MEM_WIDTH = N_MEM_HEADS * HEAD_DIM
N_MEM = 256
BRANCH = MIX_WIDTH + MEM_WIDTH
IN_WIDTH = 3 * MIX_WIDTH + MEM_WIDTH + BRANCH
QKVM_WIDTH = 3 * MIX_WIDTH + MEM_WIDTH
RMS_EPS = 1e-6
SUBLN_EPS = 1e-5
NEG_INF = -1e30

DSW_GROUPS = ((128, 1), (512, 4), (2048, 16))
DSW_HEADS = 12
DSW_GROUP_WIDTH = 256
BAND_BLOCK = 128

MOBA_HEADS = 12
MOBA_BLOCK = 256
MOBA_TOPK = 3

DIFF_HEADS = 6
DIFF_D = 64

LANES = 128
VMEM_LIMIT_BYTES = 56 * 1024 * 1024

BF16 = jnp.bfloat16
F32 = jnp.float32


def _alibi_slopes(n):
    return np.asarray(2.0 ** (-8.0 * np.arange(1, n + 1) / n), dtype=np.float32)


def _dot(a, b):
    return jnp.dot(a, b, preferred_element_type=F32)


def _dot_nt(a, b):
    return lax.dot_general(a, b, (((1,), (1,)), ((), ())), preferred_element_type=F32)


def _compiler_params(n_axes):
    return pltpu.CompilerParams(dimension_semantics=("parallel",) * n_axes,
                                vmem_limit_bytes=VMEM_LIMIT_BYTES)


def _rms_proj_kernel(x_ref, g_ref, w_ref, *out_refs, splits, n_chunk, kmean_cols):
    x = x_ref[...].astype(F32)
    y = x * lax.rsqrt(jnp.mean(x * x, axis=-1, keepdims=True) + RMS_EPS)
    h = (y * g_ref[...]).astype(BF16)
    tm = x.shape[0]
    for (lo, hi), o_ref in zip(splits, out_refs):
        for c in range(lo, hi, n_chunk):
            acc = _dot(h, w_ref[:, c:c + n_chunk])
            o_ref[:, c - lo:c - lo + n_chunk] = acc.astype(o_ref.dtype)
            if kmean_cols is not None and kmean_cols[0] <= c < kmean_cols[1]:
                km_ref = out_refs[-1]
                for r in range(tm // MOBA_BLOCK):
                    blk = acc[r * MOBA_BLOCK:(r + 1) * MOBA_BLOCK]
                    km_ref[r, :, c - kmean_cols[0]:c - kmean_cols[0] + n_chunk] = (
                        jnp.mean(blk, axis=0, keepdims=True))


def _rms_proj(x, g, w, splits, dtypes, *, tm, n_chunk=256, kmean_cols=None):
    T, D = x.shape
    N = w.shape[1]
    out_shape = [jax.ShapeDtypeStruct((T, hi - lo), dt) for (lo, hi), dt in zip(splits, dtypes)]
    out_specs = [pl.BlockSpec((tm, hi - lo), lambda i: (i, 0)) for (lo, hi) in splits]
    if kmean_cols is not None:
        kw = kmean_cols[1] - kmean_cols[0]
        out_shape.append(jax.ShapeDtypeStruct((T // MOBA_BLOCK, 1, kw), F32))
        out_specs.append(pl.BlockSpec((tm // MOBA_BLOCK, 1, kw), lambda i: (i, 0, 0)))
    kern = functools.partial(_rms_proj_kernel, splits=tuple(splits), n_chunk=n_chunk,
                             kmean_cols=kmean_cols)
    return pl.pallas_call(
        kern,
        grid=(T // tm,),
        in_specs=[pl.BlockSpec((tm, D), lambda i: (i, 0)),
                  pl.BlockSpec((1, D), lambda i: (0, 0)),
                  pl.BlockSpec((D, N), lambda i: (0, 0))],
        out_specs=out_specs,
        out_shape=out_shape,
        compiler_params=_compiler_params(1),
        name="rms_proj",
    )(x, g, w)


def _dilated_kernel(q_ref, kc_ref, kp_ref, vc_ref, vp_ref, o_ref, lse_ref, *, tu, slopes):
    ui = pl.program_id(2)
    bb = BAND_BLOCK
    row = lax.broadcasted_iota(jnp.int32, (bb, bb), 0)
    col = lax.broadcasted_iota(jnp.int32, (bb, bb), 1)
    d_cur = (row - col).astype(F32)
    d_prev = d_cur + float(bb)
    valid_cur = row >= col
    valid_prev_static = col >= row
    lane = lax.broadcasted_iota(jnp.int32, (bb, LANES), 1)
    for pair in range(2):
        cs = slice(pair * LANES, (pair + 1) * LANES)
        for s in range(tu // bb):
            rs = slice(s * bb, (s + 1) * bb)
            qs = q_ref[rs, cs]
            k_cur = kc_ref[rs, cs]
            v_cur = vc_ref[rs, cs]
            if s == 0:
                k_prev = kp_ref[:, cs]
                v_prev = vp_ref[:, cs]
                valid_prev = jnp.logical_and(valid_prev_static, ui > 0)
            else:
                ps = slice((s - 1) * bb, s * bb)
                k_prev = kc_ref[ps, cs]
                v_prev = vc_ref[ps, cs]
                valid_prev = valid_prev_static
            o_h, lse_h = [], []
            for hh in range(2):
                own = (lane >= hh * HEAD_DIM) & (lane < (hh + 1) * HEAD_DIM)
                slope = slopes[pair * 2 + hh]
                qh = jnp.where(own, qs, jnp.zeros_like(qs))
                s_cur = jnp.where(valid_cur, _dot_nt(qh, k_cur) - slope * d_cur, NEG_INF)
                s_prev = jnp.where(valid_prev, _dot_nt(qh, k_prev) - slope * d_prev, NEG_INF)
                m = jnp.maximum(jnp.max(s_cur, axis=1, keepdims=True),
                                jnp.max(s_prev, axis=1, keepdims=True))
                p_cur = jnp.exp(s_cur - m)
                p_prev = jnp.exp(s_prev - m)
                l = jnp.sum(p_cur, axis=1, keepdims=True) + jnp.sum(p_prev, axis=1, keepdims=True)
                acc = _dot(p_cur.astype(BF16), v_cur) + _dot(p_prev.astype(BF16), v_prev)
                o_h.append(acc / l)
                lse_h.append(m + jnp.log(l))
            first = lane < HEAD_DIM
            o_ref[rs, cs] = jnp.where(first, o_h[0], o_h[1])
            lse_ref[rs, cs] = jnp.where(first, lse_h[0], lse_h[1])


def _dilated_group(qkvm, g, *, batch, seq, tu=512):
    window, dil = DSW_GROUPS[g]
    assert window // dil == BAND_BLOCK
    U = seq // dil
    tu = min(tu, U)
    gw = DSW_GROUP_WIDTH
    row_w = QKVM_WIDTH // gw
    x = qkvm.reshape(batch, U, dil * QKVM_WIDTH)
    sub = tu // BAND_BLOCK
    slopes = tuple(float(v) for v in _alibi_slopes(DSW_HEADS)[g * 4:(g + 1) * 4] * np.float32(dil))

    def cur(off):
        return pl.BlockSpec((None, tu, gw), lambda b, r, u: (b, u, r * row_w + off + g))

    def prev(off):
        return pl.BlockSpec((None, BAND_BLOCK, gw),
                            lambda b, r, u: (b, jnp.maximum(u * sub - 1, 0), r * row_w + off + g))

    out_spec = pl.BlockSpec((None, tu, gw), lambda b, r, u: (b, u, r))
    o, lse = pl.pallas_call(
        functools.partial(_dilated_kernel, tu=tu, slopes=slopes),
        grid=(batch, dil, U // tu),
        in_specs=[cur(0), cur(3), prev(3), cur(6), prev(6)],
        out_specs=[out_spec, out_spec],
        out_shape=[jax.ShapeDtypeStruct((batch, U, dil * gw), F32)] * 2,
        compiler_params=_compiler_params(3),
        name=f"dilated_g{g}",
    )(x, x, x, x, x)
    return o.reshape(batch * seq, gw), lse.reshape(batch * seq, gw)


_SEL_LANES = 32
_ALIBI_PIECES = 3


def _bf16_pieces(v, n):
    out, rest = [], np.float32(v)
    for _ in range(n):
        piece = np.asarray(rest, np.float32).astype(jnp.bfloat16).astype(np.float32)
        out.append(float(piece))
        rest = np.float32(rest - piece)
    assert rest == 0.0, "slope does not split exactly into bf16 pieces"
    return out


def _moba_prep_kernel(const_ref, q_ref, k_ref, km_ref, qa_ref, ka_ref, *, nblk, blocks_per_step):
    n = pl.program_id(2)
    tq = q_ref.shape[0]
    q2 = q_ref[...]
    k2 = k_ref[...]
    lane = lax.broadcasted_iota(jnp.int32, (tq, LANES), 1)
    rowi = lax.broadcasted_iota(jnp.int32, (tq, LANES), 0)
    km_rows = lax.broadcasted_iota(jnp.int32, (LANES, LANES), 0)
    km_lane = lax.broadcasted_iota(jnp.int32, (LANES, LANES), 1)
    km = km_ref[...]
    for hh in range(2):
        own_lo = hh * HEAD_DIM
        off = (1 - hh) * HEAD_DIM
        own = (lane >= own_lo) & (lane < own_lo + HEAD_DIM)
        pieces = []
        if off > 0:
            pieces.append(jnp.zeros((off, LANES), F32))
        pieces.append(km)
        if LANES - off - nblk > 0:
            pieces.append(jnp.zeros((LANES - off - nblk, LANES), F32))
        km_full = jnp.concatenate(pieces, axis=0)
        km_own = (km_lane >= own_lo) & (km_lane < own_lo + HEAD_DIM)
        km_full = jnp.where(km_own, km_full, 0.0).astype(BF16)
        qh = jnp.where(own, q2, jnp.zeros_like(q2))
        gate = _dot_nt(qh, km_full)
        blk = lane - off
        cand = (blk >= 0) & (blk < n)
        lane_f = lane.astype(F32)
        g = jnp.where(cand, gate, -jnp.inf)
        sel = jnp.zeros((tq, LANES), jnp.bool_)
        for _ in range(MOBA_TOPK):
            mx = jnp.max(g, axis=1, keepdims=True)
            hit = (g == mx) & (mx > -jnp.inf)
            idx = jnp.min(jnp.where(hit, lane_f, float(LANES)), axis=1, keepdims=True)
            pick = lane_f == idx
            sel = sel | pick
            g = jnp.where(pick, -jnp.inf, g)
        is_sel_lane = (blk >= 0) & (blk < _SEL_LANES)
        sel_val = jnp.where(sel | (blk == n), 0.0, NEG_INF)
        q_extra = jnp.where(is_sel_lane, sel_val, const_ref[hh:hh + 1, :])
        qa = jnp.where(own, q2.astype(F32), q_extra)
        qa_ref[:, hh * LANES:(hh + 1) * LANES] = qa.astype(BF16)
        a0 = off + _SEL_LANES
        is_fine = (lane >= a0) & (lane < a0 + _ALIBI_PIECES)
        is_coarse = (lane >= a0 + _ALIBI_PIECES) & (lane < a0 + 2 * _ALIBI_PIECES)
        coarse = (n % blocks_per_step).astype(F32)
        k_extra = jnp.where(blk == n, 1.0,
                            jnp.where(is_fine, rowi.astype(F32), jnp.where(is_coarse, coarse, 0.0)))
        ka = jnp.where(own, k2.astype(F32), k_extra)
        ka_ref[:, hh * LANES:(hh + 1) * LANES] = ka.astype(BF16)


def _moba_kernel(slope_ref, qa_ref, ka_ref, v_ref, o_ref, *, step):
    i = pl.program_id(2)
    row = lax.broadcasted_iota(jnp.int32, (step, step), 0)
    col = lax.broadcasted_iota(jnp.int32, (step, step), 1)
    causal = row >= col
    lane = lax.broadcasted_iota(jnp.int32, (step, LANES), 1)
    qa = [qa_ref[:, hh * LANES:(hh + 1) * LANES] for hh in range(2)]
    step_slope = [slope_ref[2 * pl.program_id(1) + hh] * float(step) for hh in range(2)]

    own_rows = pl.ds(pl.multiple_of(i * step, step), step)
    v_own = v_ref[own_rows, :]
    init = []
    for hh in range(2):
        s = _dot_nt(qa[hh], ka_ref[own_rows, hh * LANES:(hh + 1) * LANES])
        s = jnp.where(causal, s, NEG_INF)
        m = jnp.max(s, axis=1, keepdims=True)
        p = jnp.exp(s - m)
        init.append((m, jnp.sum(p, axis=1, keepdims=True), _dot(p.astype(BF16), v_own)))

    def body(j, carry):
        rows = pl.ds(pl.multiple_of(j * step, step), step)
        vb = v_ref[rows, :]
        new = []
        for hh in range(2):
            m, l, acc = carry[hh]
            shift = step_slope[hh] * (j - i).astype(F32)
            s = _dot_nt(qa[hh], ka_ref[rows, hh * LANES:(hh + 1) * LANES])
            m_new = jnp.maximum(m, jnp.max(s, axis=1, keepdims=True) + shift)
            p = jnp.exp(s - (m_new - shift))
            alpha = jnp.exp(m - m_new)
            l = alpha * l + jnp.sum(p, axis=1, keepdims=True)
            acc = alpha * acc + _dot(p.astype(BF16), vb)
            new.append((m_new, l, acc))
        return tuple(new)

    (_, l0, acc0), (_, l1, acc1) = lax.fori_loop(0, i, body, tuple(init))
    o_ref[...] = jnp.where(lane < HEAD_DIM, acc0 / l0, acc1 / l1)


def _moba_mixer(qkvm, kmean, *, batch, seq, step=512):
    nblk = seq // MOBA_BLOCK
    assert nblk <= _SEL_LANES and step % MOBA_BLOCK == 0 and seq % step == 0
    blocks_per_step = step // MOBA_BLOCK
    npair = MOBA_HEADS // 2
    x = qkvm.reshape(batch, seq, QKVM_WIDTH)
    km = kmean.reshape(batch, nblk, MIX_WIDTH)
    slopes = _alibi_slopes(MOBA_HEADS)
    q_const = np.zeros((npair, 8, LANES), np.float32)
    for p in range(npair):
        for hh in range(2):
            a0 = (1 - hh) * HEAD_DIM + _SEL_LANES
            pieces = _bf16_pieces(slopes[2 * p + hh], _ALIBI_PIECES)
            q_const[p, hh, a0:a0 + _ALIBI_PIECES] = pieces
            q_const[p, hh, a0 + _ALIBI_PIECES:a0 + 2 * _ALIBI_PIECES] = [
                v * MOBA_BLOCK for v in pieces]
    tq = MOBA_BLOCK
    kcol = MIX_WIDTH // LANES
    qa, ka = pl.pallas_call(
        functools.partial(_moba_prep_kernel, nblk=nblk, blocks_per_step=blocks_per_step),
        grid=(batch, npair, nblk),
        in_specs=[pl.BlockSpec((None, 8, LANES), lambda b, p, i: (p, 0, 0)),
                  pl.BlockSpec((None, tq, LANES), lambda b, p, i: (b, i, p)),
                  pl.BlockSpec((None, tq, LANES), lambda b, p, i: (b, i, kcol + p)),
                  pl.BlockSpec((None, nblk, LANES), lambda b, p, i: (b, 0, p))],
        out_specs=[pl.BlockSpec((None, tq, 2 * LANES), lambda b, p, i: (b, i, p))] * 2,
        out_shape=[jax.ShapeDtypeStruct((batch, seq, MOBA_HEADS * LANES), BF16)] * 2,
        compiler_params=_compiler_params(3),
        name="moba_prep",
    )(jnp.asarray(q_const), x, x, km)
    out = pl.pallas_call(
        functools.partial(_moba_kernel, step=step),
        grid=(batch, npair, seq // step),
        in_specs=[pl.BlockSpec(memory_space=pltpu.SMEM),
                  pl.BlockSpec((None, step, 2 * LANES), lambda b, p, i: (b, i, p)),
                  pl.BlockSpec((None, seq, 2 * LANES), lambda b, p, i: (b, 0, p)),
                  pl.BlockSpec((None, seq, LANES), lambda b, p, i: (b, 0, 2 * kcol + p))],
        out_specs=pl.BlockSpec((None, step, LANES), lambda b, p, i: (b, i, p)),
        out_shape=jax.ShapeDtypeStruct((batch, seq, MIX_WIDTH), F32),
        compiler_params=_compiler_params(3),
        name="moba_attn",
    )(jnp.asarray(slopes), qa, ka, x)
    return out.reshape(batch * seq, MIX_WIDTH)


def _diff_kernel(slope_ref, lam_ref, g_ref, q_ref, k_ref, v_ref, o_ref, *, tk, lambda_init):
    i = pl.program_id(2)
    tq = q_ref.shape[0]
    sub = tq // tk
    q2 = q_ref[...]
    lane = lax.broadcasted_iota(jnp.int32, (tq, LANES), 1)
    zero = jnp.zeros_like(q2)
    q_maps = (jnp.where(lane < DIFF_D, q2, zero), jnp.where(lane >= DIFF_D, q2, zero))
    slope = slope_ref[pl.program_id(1)]
    col = lax.broadcasted_iota(jnp.int32, (1, tk), 1)
    row2 = lax.broadcasted_iota(jnp.int32, (tq, tk), 0)
    col2 = lax.broadcasted_iota(jnp.int32, (tq, tk), 1)

    def step(j, carry, masked):
        rows = pl.ds(pl.multiple_of(j * tk, tk), tk)
        kb = k_ref[rows, :]
        vb = v_ref[rows, :]
        bias = slope * (col + (j * tk - i * tq)).astype(F32)
        new = []
        for mi in range(2):
            m, l, acc = carry[mi]
            s = _dot_nt(q_maps[mi], kb) + bias
            if masked:
                s = jnp.where(row2 + i * tq >= col2 + j * tk, s, NEG_INF)
            m_new = jnp.maximum(m, jnp.max(s, axis=1, keepdims=True))
            p = jnp.exp(s - m_new)
            alpha = jnp.exp(m - m_new)
            l = alpha * l + jnp.sum(p, axis=1, keepdims=True)
            acc = alpha * acc + _dot(p.astype(BF16), vb)
            new.append((m_new, l, acc))
        return tuple(new)

    init = tuple((jnp.full((tq, 1), NEG_INF, F32), jnp.zeros((tq, 1), F32),
                  jnp.zeros((tq, LANES), F32)) for _ in range(2))
    carry = lax.fori_loop(0, i * sub, lambda j, c: step(j, c, False), init)
    for d in range(sub):
        carry = step(i * sub + d, carry, True)
    (_, l1, acc1), (_, l2, acc2) = carry

    lq1, lk1, lq2, lk2 = (lam_ref[r:r + 1, :] for r in range(4))
    lam = (jnp.exp(jnp.sum(lq1 * lk1, axis=1, keepdims=True))
           - jnp.exp(jnp.sum(lq2 * lk2, axis=1, keepdims=True)) + lambda_init)
    o = acc1 / l1 - lam * (acc2 / l2)
    y = o * lax.rsqrt(jnp.mean(o * o, axis=-1, keepdims=True) + SUBLN_EPS)
    o_ref[...] = (y * g_ref[...]) * (1.0 - lambda_init)


def _diff_mixer(qkvm, lam_vecs, subln_g, lambda_init, *, batch, seq, tq=512, tk=512):
    x = qkvm.reshape(batch, seq, QKVM_WIDTH)
    slopes = _alibi_slopes(DIFF_HEADS)
    kcol = MIX_WIDTH // LANES
    out = pl.pallas_call(
        functools.partial(_diff_kernel, tk=tk, lambda_init=float(lambda_init)),
        grid=(batch, DIFF_HEADS, seq // tq),
        in_specs=[pl.BlockSpec(memory_space=pltpu.SMEM),
                  pl.BlockSpec((4, DIFF_D), lambda b, h, i: (0, 0)),
                  pl.BlockSpec((1, 2 * DIFF_D), lambda b, h, i: (0, 0)),
                  pl.BlockSpec((None, tq, LANES), lambda b, h, i: (b, i, h)),
                  pl.BlockSpec((None, seq, LANES), lambda b, h, i: (b, 0, kcol + h)),
                  pl.BlockSpec((None, seq, LANES), lambda b, h, i: (b, 0, 2 * kcol + h))],
        out_specs=pl.BlockSpec((None, tq, LANES), lambda b, h, i: (b, i, h)),
        out_shape=jax.ShapeDtypeStruct((batch, seq, MIX_WIDTH), F32),
        compiler_params=_compiler_params(3),
        name="diff_attn",
    )(jnp.asarray(slopes), lam_vecs, subln_g, x, x, x)
    return out.reshape(batch * seq, MIX_WIDTH)


def _post_kernel(*refs, n_mix, dilated, final):
    mix_refs = refs[:n_mix]
    x_ref, qm_ref, gate_ref, kv_ref, w_ref = refs[n_mix:n_mix + 5]
    rest = refs[n_mix + 5:]
    fg_ref = rest[0] if final else None
    o_ref = rest[-1]
    tm = x_ref.shape[0]

    if dilated:
        outs = [r[...] for r in mix_refs[:3]]
        lses = [r[...] for r in mix_refs[3:]]
        mx = jnp.maximum(jnp.maximum(lses[0], lses[1]), lses[2])
        es = [jnp.exp(v - mx) for v in lses]
        den = es[0] + es[1] + es[2]
        mix_parts = [o * (e / den) for o, e in zip(outs, es)]
    else:
        mix = mix_refs[0][...]
        mix_parts = [mix[:, g * DSW_GROUP_WIDTH:(g + 1) * DSW_GROUP_WIDTH] for g in range(3)]

    lane = lax.broadcasted_iota(jnp.int32, (tm, LANES), 1)
    mem_parts = []
    for pair in range(MEM_WIDTH // LANES):
        cs = slice(pair * LANES, (pair + 1) * LANES)
        q2 = qm_ref[:, cs]
        k2 = kv_ref[:, cs]
        v2 = kv_ref[:, MEM_WIDTH + pair * LANES:MEM_WIDTH + (pair + 1) * LANES]
        o_h = []
        for hh in range(2):
            own = (lane >= hh * HEAD_DIM) & (lane < (hh + 1) * HEAD_DIM)
            s = _dot_nt(jnp.where(own, q2, jnp.zeros_like(q2)), k2)
            p = jnp.exp(s - jnp.max(s, axis=1, keepdims=True))
            l = jnp.sum(p, axis=1, keepdims=True)
            o_h.append(_dot(p.astype(BF16), v2) / l)
        mem_parts.append(jnp.where(lane < HEAD_DIM, o_h[0], o_h[1]))

    acc = x_ref[...]
    parts = [(g * DSW_GROUP_WIDTH, mix_parts[g]) for g in range(3)]
    parts += [(MIX_WIDTH + p * LANES, mem_parts[p]) for p in range(len(mem_parts))]
    for c0, val in parts:
        w = val.shape[1]
        gt = gate_ref[:, c0:c0 + w]
        y = val * (gt * (1.0 / (1.0 + jnp.exp(-gt))))
        acc = acc + _dot(y.astype(BF16), w_ref[c0:c0 + w, :])
    if final:
        yn = acc * lax.rsqrt(jnp.mean(acc * acc, axis=-1, keepdims=True) + RMS_EPS)
        acc = yn * fg_ref[...]
    o_ref[...] = acc


def _post(mix_list, x, qkvm, gate, mem_kv, w_out, final_g, *, dilated, batch, seq, tm=512):
    T = x.shape[0]
    tiles_per_batch = seq // tm
    n_mix = len(mix_list)
    final = final_g is not None
    qm_block = (3 * MIX_WIDTH) // MEM_WIDTH
    in_specs = [pl.BlockSpec((tm, m.shape[1]), lambda i: (i, 0)) for m in mix_list]
    in_specs += [pl.BlockSpec((tm, D_MODEL), lambda i: (i, 0)),
                 pl.BlockSpec((tm, MEM_WIDTH), lambda i: (i, qm_block)),
                 pl.BlockSpec((tm, BRANCH), lambda i: (i, 0)),
                 pl.BlockSpec((None, N_MEM, 2 * MEM_WIDTH), lambda i: (i // tiles_per_batch, 0, 0)),
                 pl.BlockSpec((BRANCH, D_MODEL), lambda i: (0, 0))]
    args = list(mix_list) + [x, qkvm, gate, mem_kv, w_out]
    if final:
        in_specs.append(pl.BlockSpec((1, D_MODEL), lambda i: (0, 0)))
        args.append(final_g)
    return pl.pallas_call(
        functools.partial(_post_kernel, n_mix=n_mix, dilated=dilated, final=final),
        grid=(T // tm,),
        in_specs=in_specs,
        out_specs=pl.BlockSpec((tm, D_MODEL), lambda i: (i, 0)),
        out_shape=jax.ShapeDtypeStruct((T, D_MODEL), F32),
        compiler_params=_compiler_params(1),
        name="post",
    )(*args)


def kernel(x, mem, norm_g, w_in, w_out, mem_norm_g, w_mem_kv, diff_lambda_q1, diff_lambda_k1,
           diff_lambda_q2, diff_lambda_k2, diff_subln_g, final_norm_g):
    batch, seq, d = x.shape
    depth = w_in.shape[0]
    T = batch * seq
    xf = x.reshape(T, d)
    memf = mem.reshape(batch * mem.shape[1], d)

    scale = HEAD_DIM ** -0.5
    col_scale = np.ones((IN_WIDTH,), np.float32)
    col_scale[:MIX_WIDTH] = scale
    col_scale[3 * MIX_WIDTH:QKVM_WIDTH] = scale
    w_in_b = (w_in * col_scale).astype(BF16)
    w_out_b = w_out.astype(BF16)
    w_kv_b = w_mem_kv.astype(BF16)

    for i in range(depth):
        kind = i % N_MIXERS
        kmean_cols = (MIX_WIDTH, 2 * MIX_WIDTH) if kind == 1 else None
        proj = _rms_proj(xf, norm_g[i].reshape(1, d), w_in_b[i],
                         [(0, QKVM_WIDTH), (QKVM_WIDTH, IN_WIDTH)], [BF16, F32],
                         tm=512, kmean_cols=kmean_cols)
        qkvm, gate = proj[0], proj[1]
        (mem_kv,) = _rms_proj(memf, mem_norm_g[i].reshape(1, d), w_kv_b[i],
                              [(0, 2 * MEM_WIDTH)], [BF16], tm=N_MEM)
        mem_kv = mem_kv.reshape(batch, N_MEM, 2 * MEM_WIDTH)
        if kind == 0:
            res = [_dilated_group(qkvm, g, batch=batch, seq=seq) for g in range(len(DSW_GROUPS))]
            mix_list = [r[0] for r in res] + [r[1] for r in res]
        elif kind == 1:
            mix_list = [_moba_mixer(qkvm, proj[2], batch=batch, seq=seq)]
        else:
            c = i // N_MIXERS
            lambda_init = 0.8 - 0.6 * math.exp(-0.3 * i)
            lam_vecs = jnp.stack([diff_lambda_q1[c], diff_lambda_k1[c],
                                  diff_lambda_q2[c], diff_lambda_k2[c]]).astype(F32)
            mix_list = [_diff_mixer(qkvm, lam_vecs, diff_subln_g[c].reshape(1, 2 * DIFF_D),
                                    lambda_init, batch=batch, seq=seq)]
        fg = final_norm_g.reshape(1, d) if i == depth - 1 else None
        xf = _post(mix_list, xf, qkvm, gate, mem_kv, w_out_b[i], fg,
                   dilated=(kind == 0), batch=batch, seq=seq)
    return xf.reshape(batch, seq, d)
```

```python
import functools
import math

import numpy as np
import jax
import jax.numpy as jnp
from jax import lax
from jax.experimental import pallas as pl
from jax.experimental.pallas import tpu as pltpu

D_MODEL = 1024
DEPTH = 4
N_MIXERS = 3
HEAD_DIM = 64
MIX_WIDTH = 768
N_MEM_HEADS = 4
MEM_WIDTH = N_MEM_HEADS * HEAD_DIM
N_MEM = 256
BRANCH = MIX_WIDTH + MEM_WIDTH
IN_WIDTH = 3 * MIX_WIDTH + MEM_WIDTH + BRANCH
QKVM_WIDTH = 3 * MIX_WIDTH + MEM_WIDTH
RMS_EPS = 1e-6
SUBLN_EPS = 1e-5
NEG_INF = -1e30
LOG2E = math.log2(math.e)

DSW_GROUPS = ((128, 1), (512, 4), (2048, 16))
DSW_HEADS = 12
DSW_GROUP_WIDTH = 256
BAND_BLOCK = 128

MOBA_HEADS = 12
MOBA_BLOCK = 256
MOBA_TOPK = 3

DIFF_HEADS = 6
DIFF_D = 64

LANES = 128
QUERY_SUB = 256
VMEM_LIMIT_BYTES = 56 * 1024 * 1024

BF16 = jnp.bfloat16
F32 = jnp.float32


def _alibi_slopes(n):
    return np.asarray(2.0 ** (-8.0 * np.arange(1, n + 1) / n), dtype=np.float32)


def _log2_slopes(n):
    return np.asarray(_alibi_slopes(n).astype(np.float64) * LOG2E, dtype=np.float32)


def _dot(a, b):
    return jnp.dot(a, b, preferred_element_type=F32)


def _dot_nt(a, b):
    return lax.dot_general(a, b, (((1,), (1,)), ((), ())), preferred_element_type=F32)


def _compiler_params(n_axes):
    return pltpu.CompilerParams(dimension_semantics=("parallel",) * n_axes,
                                vmem_limit_bytes=VMEM_LIMIT_BYTES)


def _rms_proj_kernel(x_ref, g_ref, w_ref, *out_refs, splits, n_chunk, scales, vt_cols, kmean_cols):
    x = x_ref[...].astype(F32)
    y = x * lax.rsqrt(jnp.mean(x * x, axis=-1, keepdims=True) + RMS_EPS)
    h = (y * g_ref[...]).astype(BF16)
    tm = x.shape[0]
    for (lo, hi), o_ref in zip(splits, out_refs):
        for c in range(lo, hi, n_chunk):
            acc = _dot(h, w_ref[:, c:c + n_chunk])
            for s_lo, s_hi, factor in scales:
                if s_lo <= c < s_hi:
                    acc = acc * factor
            o_ref[:, c - lo:c - lo + n_chunk] = acc.astype(o_ref.dtype)
            if vt_cols is not None and vt_cols[0] <= c < vt_cols[1]:
                vt_ref = out_refs[len(splits)]
                vt_ref[c - vt_cols[0]:c - vt_cols[0] + n_chunk, :] = acc.T.astype(BF16)
            if kmean_cols is not None and kmean_cols[0] <= c < kmean_cols[1]:
                km_ref = out_refs[-1]
                for r in range(tm // MOBA_BLOCK):
                    blk = acc[r * MOBA_BLOCK:(r + 1) * MOBA_BLOCK]
                    km_ref[r, :, c - kmean_cols[0]:c - kmean_cols[0] + n_chunk] = (
                        jnp.mean(blk, axis=0, keepdims=True))


def _rms_proj(x, g, w, splits, dtypes, *, tm, n_chunk=256, scales=(), vt_cols=None,
              kmean_cols=None):
    T, D = x.shape
    N = w.shape[1]
    out_shape = [jax.ShapeDtypeStruct((T, hi - lo), dt) for (lo, hi), dt in zip(splits, dtypes)]
    out_specs = [pl.BlockSpec((tm, hi - lo), lambda i: (i, 0)) for (lo, hi) in splits]
    if vt_cols is not None:
        vw = vt_cols[1] - vt_cols[0]
        out_shape.append(jax.ShapeDtypeStruct((vw, T), BF16))
        out_specs.append(pl.BlockSpec((vw, tm), lambda i: (0, i)))
    if kmean_cols is not None:
        kw = kmean_cols[1] - kmean_cols[0]
        out_shape.append(jax.ShapeDtypeStruct((T // MOBA_BLOCK, 1, kw), F32))
        out_specs.append(pl.BlockSpec((tm // MOBA_BLOCK, 1, kw), lambda i: (i, 0, 0)))
    kern = functools.partial(_rms_proj_kernel, splits=tuple(splits), n_chunk=n_chunk,
                             scales=tuple(scales), vt_cols=vt_cols, kmean_cols=kmean_cols)
    return pl.pallas_call(
        kern,
        grid=(T // tm,),
        in_specs=[pl.BlockSpec((tm, D), lambda i: (i, 0)),
                  pl.BlockSpec((1, D), lambda i: (0, 0)),
                  pl.BlockSpec((D, N), lambda i: (0, 0))],
        out_specs=out_specs,
        out_shape=out_shape,
        compiler_params=_compiler_params(1),
        name="rms_proj",
    )(x, g, w)


def _dilated_kernel(q_ref, kc_ref, kp_ref, vc_ref, vp_ref, o_ref, lse_ref, *, tu, slopes):
    ui = pl.program_id(2)
    bb = BAND_BLOCK
    row = lax.broadcasted_iota(jnp.int32, (bb, bb), 0)
    col = lax.broadcasted_iota(jnp.int32, (bb, bb), 1)
    d_cur = (row - col).astype(F32)
    d_prev = d_cur + float(bb)
    valid_cur = row >= col
    valid_prev_static = col >= row
    lane = lax.broadcasted_iota(jnp.int32, (bb, LANES), 1)
    for pair in range(2):
        cs = slice(pair * LANES, (pair + 1) * LANES)
        for s in range(tu // bb):
            rs = slice(s * bb, (s + 1) * bb)
            qs = q_ref[rs, cs]
            k_cur = kc_ref[rs, cs]
            v_cur = vc_ref[rs, cs]
            if s == 0:
                k_prev = kp_ref[:, cs]
                v_prev = vp_ref[:, cs]
                valid_prev = jnp.logical_and(valid_prev_static, ui > 0)
            else:
                ps = slice((s - 1) * bb, s * bb)
                k_prev = kc_ref[ps, cs]
                v_prev = vc_ref[ps, cs]
                valid_prev = valid_prev_static
            o_h, lse_h = [], []
            for hh in range(2):
                own = (lane >= hh * HEAD_DIM) & (lane < (hh + 1) * HEAD_DIM)
                slope = slopes[pair * 2 + hh]
                qh = jnp.where(own, qs, jnp.zeros_like(qs))
                s_cur = jnp.where(valid_cur, _dot_nt(qh, k_cur) - slope * d_cur, NEG_INF)
                s_prev = jnp.where(valid_prev, _dot_nt(qh, k_prev) - slope * d_prev, NEG_INF)
                m = jnp.maximum(jnp.max(s_cur, axis=1, keepdims=True),
                                jnp.max(s_prev, axis=1, keepdims=True))
                p_cur = jnp.exp(s_cur - m)
                p_prev = jnp.exp(s_prev - m)
                l = jnp.sum(p_cur, axis=1, keepdims=True) + jnp.sum(p_prev, axis=1, keepdims=True)
                acc = _dot(p_cur.astype(BF16), v_cur) + _dot(p_prev.astype(BF16), v_prev)
                o_h.append(acc / l)
                lse_h.append(m + jnp.log(l))
            first = lane < HEAD_DIM
            o_ref[rs, cs] = jnp.where(first, o_h[0], o_h[1])
            lse_ref[rs, cs] = jnp.where(first, lse_h[0], lse_h[1])


def _dilated_group(qkvm, g, *, batch, seq, tu=512):
    window, dil = DSW_GROUPS[g]
    assert window // dil == BAND_BLOCK
    U = seq // dil
    tu = min(tu, U)
    gw = DSW_GROUP_WIDTH
    row_w = QKVM_WIDTH // gw
    x = qkvm.reshape(batch, U, dil * QKVM_WIDTH)
    sub = tu // BAND_BLOCK
    slopes = tuple(float(v) for v in _alibi_slopes(DSW_HEADS)[g * 4:(g + 1) * 4] * np.float32(dil))

    def cur(off):
        return pl.BlockSpec((None, tu, gw), lambda b, r, u: (b, u, r * row_w + off + g))

    def prev(off):
        return pl.BlockSpec((None, BAND_BLOCK, gw),
                            lambda b, r, u: (b, jnp.maximum(u * sub - 1, 0), r * row_w + off + g))

    out_spec = pl.BlockSpec((None, tu, gw), lambda b, r, u: (b, u, r))
    o, lse = pl.pallas_call(
        functools.partial(_dilated_kernel, tu=tu, slopes=slopes),
        grid=(batch, dil, U // tu),
        in_specs=[cur(0), cur(3), prev(3), cur(6), prev(6)],
        out_specs=[out_spec, out_spec],
        out_shape=[jax.ShapeDtypeStruct((batch, U, dil * gw), F32)] * 2,
        compiler_params=_compiler_params(3),
        name=f"dilated_g{g}",
    )(x, x, x, x, x)
    return o.reshape(batch * seq, gw), lse.reshape(batch * seq, gw)


_SEL_LANES = 32
_ALIBI_PIECES = 3


def _bf16_pieces(v, n):
    out, rest = [], np.float32(v)
    for _ in range(n):
        piece = np.asarray(rest, np.float32).astype(jnp.bfloat16).astype(np.float32)
        out.append(float(piece))
        rest = np.float32(rest - piece)
    assert rest == 0.0, "slope does not split exactly into bf16 pieces"
    return out


def _moba_prep_kernel(const_ref, q_ref, k_ref, km_ref, qa_ref, ka_ref, *, nblk, blocks_per_step):
    n = pl.program_id(2)
    tq = q_ref.shape[0]
    q2 = q_ref[...]
    k2 = k_ref[...]
    lane = lax.broadcasted_iota(jnp.int32, (tq, LANES), 1)
    rowi = lax.broadcasted_iota(jnp.int32, (tq, LANES), 0)
    km_rows = lax.broadcasted_iota(jnp.int32, (LANES, LANES), 0)
    km_lane = lax.broadcasted_iota(jnp.int32, (LANES, LANES), 1)
    km = km_ref[...]
    for hh in range(2):
        own_lo = hh * HEAD_DIM
        off = (1 - hh) * HEAD_DIM
        own = (lane >= own_lo) & (lane < own_lo + HEAD_DIM)
        pieces = []
        if off > 0:
            pieces.append(jnp.zeros((off, LANES), F32))
        pieces.append(km)
        if LANES - off - nblk > 0:
            pieces.append(jnp.zeros((LANES - off - nblk, LANES), F32))
        km_full = jnp.concatenate(pieces, axis=0)
        km_own = (km_lane >= own_lo) & (km_lane < own_lo + HEAD_DIM)
        km_full = jnp.where(km_own, km_full, 0.0).astype(BF16)
        qh = jnp.where(own, q2, jnp.zeros_like(q2))
        gate = _dot_nt(qh, km_full)
        blk = lane - off
        cand = (blk >= 0) & (blk < n)
        lane_f = lane.astype(F32)
        g = jnp.where(cand, gate, -jnp.inf)
        sel = jnp.zeros((tq, LANES), jnp.bool_)
        for _ in range(MOBA_TOPK):
            mx = jnp.max(g, axis=1, keepdims=True)
            hit = (g == mx) & (mx > -jnp.inf)
            idx = jnp.min(jnp.where(hit, lane_f, float(LANES)), axis=1, keepdims=True)
            pick = lane_f == idx
            sel = sel | pick
            g = jnp.where(pick, -jnp.inf, g)
        is_sel_lane = (blk >= 0) & (blk < _SEL_LANES)
        sel_val = jnp.where(sel | (blk == n), 0.0, NEG_INF)
        q_extra = jnp.where(is_sel_lane, sel_val, const_ref[hh:hh + 1, :])
        qa = jnp.where(own, q2.astype(F32), q_extra)
        qa_ref[:, hh * LANES:(hh + 1) * LANES] = qa.astype(BF16)
        a0 = off + _SEL_LANES
        is_fine = (lane >= a0) & (lane < a0 + _ALIBI_PIECES)
        is_coarse = (lane >= a0 + _ALIBI_PIECES) & (lane < a0 + 2 * _ALIBI_PIECES)
        coarse = (n % blocks_per_step).astype(F32)
        k_extra = jnp.where(blk == n, 1.0,
                            jnp.where(is_fine, rowi.astype(F32), jnp.where(is_coarse, coarse, 0.0)))
        ka = jnp.where(own, k2.astype(F32), k_extra)
        ka_ref[:, hh * LANES:(hh + 1) * LANES] = ka.astype(BF16)


def _softmax_step(st, vt, carry, shift):
    m, l, acc = carry
    m_new = jnp.maximum(m, jnp.max(st, axis=0, keepdims=True) + shift)
    p = jnp.exp2(st - (m_new - shift))
    alpha = jnp.exp2(m - m_new)
    l = alpha * l + jnp.sum(p, axis=0, keepdims=True)
    acc = alpha * acc + _dot(vt, p.astype(BF16))
    return m_new, l, acc


def _causal_key_steps(scores_fn, vt_ref, step_slopes, i, step):
    nsub = step // QUERY_SUB
    n_chain = len(step_slopes)

    def rows_of(j):
        return pl.ds(pl.multiple_of(j * step, step), step)

    def body(j, carry):
        sts = scores_fn(rows_of(j))
        vt = vt_ref[:, rows_of(j)]
        dj = (j - i).astype(F32)
        return tuple(_softmax_step(sts[c], vt, carry[c], step_slopes[c] * dj)
                     for c in range(n_chain))

    init = tuple((jnp.full((1, QUERY_SUB), NEG_INF, F32), jnp.zeros((1, QUERY_SUB), F32),
                  jnp.zeros((LANES, QUERY_SUB), F32)) for _ in range(n_chain))
    carry = lax.fori_loop(0, i, body, init)
    sts = scores_fn(rows_of(i))
    key = lax.broadcasted_iota(jnp.int32, (step, QUERY_SUB), 0)
    qry = lax.broadcasted_iota(jnp.int32, (step, QUERY_SUB), 1)
    vt = vt_ref[:, rows_of(i)]
    out = []
    for c in range(n_chain):
        st = jnp.where(qry + (c % nsub) * QUERY_SUB >= key, sts[c], NEG_INF)
        _, l, acc = _softmax_step(st, vt, carry[c], 0.0)
        out.append((l, acc))
    return out


def _moba_kernel(slope_ref, qa_ref, ka_ref, vt_ref, o_ref, *, step):
    i = pl.program_id(2)
    nsub = step // QUERY_SUB
    chains = [(hh, qs) for hh in range(2) for qs in range(nsub)]
    qa = [qa_ref[qs * QUERY_SUB:(qs + 1) * QUERY_SUB, hh * LANES:(hh + 1) * LANES]
          for hh, qs in chains]
    step_slopes = [slope_ref[2 * pl.program_id(1) + hh] * float(step) for hh, _ in chains]

    def scores_fn(rows):
        return tuple(_dot_nt(ka_ref[rows, hh * LANES:(hh + 1) * LANES], qa[c])
                     for c, (hh, _) in enumerate(chains))

    res = _causal_key_steps(scores_fn, vt_ref, step_slopes, i, step)
    chan = lax.broadcasted_iota(jnp.int32, (LANES, QUERY_SUB), 0)
    for qs in range(nsub):
        (l0, acc0), (l1, acc1) = res[qs], res[nsub + qs]
        o_ref[qs * QUERY_SUB:(qs + 1) * QUERY_SUB, :] = (
            jnp.where(chan < HEAD_DIM, acc0 / l0, acc1 / l1).T)


def _moba_mixer(qkvm, vt, kmean, *, batch, seq, step=512):
    nblk = seq // MOBA_BLOCK
    assert nblk <= _SEL_LANES and step % MOBA_BLOCK == 0 and seq % step == 0
    blocks_per_step = step // MOBA_BLOCK
    npair = MOBA_HEADS // 2
    x = qkvm.reshape(batch, seq, QKVM_WIDTH)
    km = kmean.reshape(batch, nblk, MIX_WIDTH)
    slopes = _log2_slopes(MOBA_HEADS)
    q_const = np.zeros((npair, 8, LANES), np.float32)
    for p in range(npair):
        for hh in range(2):
            a0 = (1 - hh) * HEAD_DIM + _SEL_LANES
            pieces = _bf16_pieces(slopes[2 * p + hh], _ALIBI_PIECES)
            q_const[p, hh, a0:a0 + _ALIBI_PIECES] = pieces
            q_const[p, hh, a0 + _ALIBI_PIECES:a0 + 2 * _ALIBI_PIECES] = [
                v * MOBA_BLOCK for v in pieces]
    tq = MOBA_BLOCK
    kcol = MIX_WIDTH // LANES
    qa, ka = pl.pallas_call(
        functools.partial(_moba_prep_kernel, nblk=nblk, blocks_per_step=blocks_per_step),
        grid=(batch, npair, nblk),
        in_specs=[pl.BlockSpec((None, 8, LANES), lambda b, p, i: (p, 0, 0)),
                  pl.BlockSpec((None, tq, LANES), lambda b, p, i: (b, i, p)),
                  pl.BlockSpec((None, tq, LANES), lambda b, p, i: (b, i, kcol + p)),
                  pl.BlockSpec((None, nblk, LANES), lambda b, p, i: (b, 0, p))],
        out_specs=[pl.BlockSpec((None, tq, 2 * LANES), lambda b, p, i: (b, i, p))] * 2,
        out_shape=[jax.ShapeDtypeStruct((batch, seq, MOBA_HEADS * LANES), BF16)] * 2,
        compiler_params=_compiler_params(3),
        name="moba_prep",
    )(jnp.asarray(q_const), x, x, km)
    out = pl.pallas_call(
        functools.partial(_moba_kernel, step=step),
        grid=(batch, npair, seq // step),
        in_specs=[pl.BlockSpec(memory_space=pltpu.SMEM),
                  pl.BlockSpec((None, step, 2 * LANES), lambda b, p, i: (b, i, p)),
                  pl.BlockSpec((None, seq, 2 * LANES), lambda b, p, i: (b, 0, p)),
                  pl.BlockSpec((LANES, seq), lambda b, p, i: (p, b))],
        out_specs=pl.BlockSpec((None, step, LANES), lambda b, p, i: (b, i, p)),
        out_shape=jax.ShapeDtypeStruct((batch, seq, MIX_WIDTH), F32),
        compiler_params=_compiler_params(3),
        name="moba_attn",
    )(jnp.asarray(slopes), qa, ka, vt)
    return out.reshape(batch * seq, MIX_WIDTH)


def _position_lanes(step):
    tab = np.zeros((step, LANES), np.float32)
    r = np.arange(step)
    tab[:, 0:_ALIBI_PIECES] = (r % MOBA_BLOCK)[:, None]
    tab[:, _ALIBI_PIECES:2 * _ALIBI_PIECES] = (r // MOBA_BLOCK)[:, None]
    return jnp.asarray(tab, dtype=BF16)


def _diff_kernel(slope_ref, qx_ref, kx_ref, lam_ref, g_ref, q_ref, k_ref, vt_ref, o_ref, *,
                 step, lambda_init):
    i = pl.program_id(2)
    nsub = step // QUERY_SUB
    lane = lax.broadcasted_iota(jnp.int32, (QUERY_SUB, LANES), 1)
    qx = jnp.broadcast_to(qx_ref[0:1, :], (QUERY_SUB, LANES)).astype(BF16)
    chains = [(mi, qs) for mi in range(2) for qs in range(nsub)]
    qa = []
    for mi, qs in chains:
        q2 = q_ref[qs * QUERY_SUB:(qs + 1) * QUERY_SUB, :]
        own_map = (lane >= mi * DIFF_D) & (lane < (mi + 1) * DIFF_D)
        qa.append(jnp.concatenate([jnp.where(own_map, q2, jnp.zeros_like(q2)), qx], axis=1))
    kx = kx_ref[...]
    step_slopes = [slope_ref[pl.program_id(1)] * float(step)] * len(chains)

    def scores_fn(rows):
        ka = jnp.concatenate([k_ref[rows, :], kx], axis=1)
        return tuple(_dot_nt(ka, qa[c]) for c in range(len(chains)))

    res = _causal_key_steps(scores_fn, vt_ref, step_slopes, i, step)

    lq1, lk1, lq2, lk2 = (lam_ref[r:r + 1, :] for r in range(4))
    lam = (jnp.exp(jnp.sum(lq1 * lk1, axis=1, keepdims=True))
           - jnp.exp(jnp.sum(lq2 * lk2, axis=1, keepdims=True)) + lambda_init)
    for qs in range(nsub):
        (l1, acc1), (l2, acc2) = res[qs], res[nsub + qs]
        o = acc1 / l1 - lam * (acc2 / l2)
        y = o * lax.rsqrt(jnp.mean(o * o, axis=0, keepdims=True) + SUBLN_EPS)
        o_ref[qs * QUERY_SUB:(qs + 1) * QUERY_SUB, :] = (
            (y * g_ref[...]) * (1.0 - lambda_init)).T


def _diff_mixer(qkvm, vt, lam_vecs, subln_g, lambda_init, *, batch, seq, step=512):
    assert step % MOBA_BLOCK == 0 and seq % step == 0
    x = qkvm.reshape(batch, seq, QKVM_WIDTH)
    slopes = _log2_slopes(DIFF_HEADS)
    q_extra = np.zeros((DIFF_HEADS, 8, LANES), np.float32)
    for h in range(DIFF_HEADS):
        pieces = _bf16_pieces(slopes[h], _ALIBI_PIECES)
        q_extra[h, 0, 0:_ALIBI_PIECES] = pieces
        q_extra[h, 0, _ALIBI_PIECES:2 * _ALIBI_PIECES] = [v * MOBA_BLOCK for v in pieces]
    kcol = MIX_WIDTH // LANES
    out = pl.pallas_call(
        functools.partial(_diff_kernel, step=step, lambda_init=float(lambda_init)),
        grid=(batch, DIFF_HEADS, seq // step),
        in_specs=[pl.BlockSpec(memory_space=pltpu.SMEM),
                  pl.BlockSpec((None, 8, LANES), lambda b, h, i: (h, 0, 0)),
                  pl.BlockSpec((step, LANES), lambda b, h, i: (0, 0)),
                  pl.BlockSpec((4, DIFF_D), lambda b, h, i: (0, 0)),
                  pl.BlockSpec((2 * DIFF_D, 1), lambda b, h, i: (0, 0)),
                  pl.BlockSpec((None, step, LANES), lambda b, h, i: (b, i, h)),
                  pl.BlockSpec((None, seq, LANES), lambda b, h, i: (b, 0, kcol + h)),
                  pl.BlockSpec((LANES, seq), lambda b, h, i: (h, b))],
        out_specs=pl.BlockSpec((None, step, LANES), lambda b, h, i: (b, i, h)),
        out_shape=jax.ShapeDtypeStruct((batch, seq, MIX_WIDTH), F32),
        compiler_params=_compiler_params(3),
        name="diff_attn",
    )(jnp.asarray(slopes), jnp.asarray(q_extra), _position_lanes(step), lam_vecs,
      subln_g.reshape(2 * DIFF_D, 1), x, x, vt)
    return out.reshape(batch * seq, MIX_WIDTH)


def _post_kernel(*refs, n_mix, dilated, final):
    mix_refs = refs[:n_mix]
    x_ref, qm_ref, gate_ref, kv_ref, w_ref = refs[n_mix:n_mix + 5]
    rest = refs[n_mix + 5:]
    fg_ref = rest[0] if final else None
    o_ref = rest[-1]
    tm = x_ref.shape[0]

    if dilated:
        outs = [r[...] for r in mix_refs[:3]]
        lses = [r[...] for r in mix_refs[3:]]
        mx = jnp.maximum(jnp.maximum(lses[0], lses[1]), lses[2])
        es = [jnp.exp(v - mx) for v in lses]
        den = es[0] + es[1] + es[2]
        mix_parts = [o * (e / den) for o, e in zip(outs, es)]
    else:
        mix = mix_refs[0][...]
        mix_parts = [mix[:, g * DSW_GROUP_WIDTH:(g + 1) * DSW_GROUP_WIDTH] for g in range(3)]

    lane = lax.broadcasted_iota(jnp.int32, (tm, LANES), 1)
    mem_parts = []
    for pair in range(MEM_WIDTH // LANES):
        cs = slice(pair * LANES, (pair + 1) * LANES)
        q2 = qm_ref[:, cs]
        k2 = kv_ref[:, cs]
        v2 = kv_ref[:, MEM_WIDTH + pair * LANES:MEM_WIDTH + (pair + 1) * LANES]
        o_h = []
        for hh in range(2):
            own = (lane >= hh * HEAD_DIM) & (lane < (hh + 1) * HEAD_DIM)
            s = _dot_nt(jnp.where(own, q2, jnp.zeros_like(q2)), k2)
            p = jnp.exp(s - jnp.max(s, axis=1, keepdims=True))
            l = jnp.sum(p, axis=1, keepdims=True)
            o_h.append(_dot(p.astype(BF16), v2) / l)
        mem_parts.append(jnp.where(lane < HEAD_DIM, o_h[0], o_h[1]))

    acc = x_ref[...]
    parts = [(g * DSW_GROUP_WIDTH, mix_parts[g]) for g in range(3)]
    parts += [(MIX_WIDTH + p * LANES, mem_parts[p]) for p in range(len(mem_parts))]
    for c0, val in parts:
        w = val.shape[1]
        gt = gate_ref[:, c0:c0 + w]
        y = val * (gt * (1.0 / (1.0 + jnp.exp(-gt))))
        acc = acc + _dot(y.astype(BF16), w_ref[c0:c0 + w, :])
    if final:
        yn = acc * lax.rsqrt(jnp.mean(acc * acc, axis=-1, keepdims=True) + RMS_EPS)
        acc = yn * fg_ref[...]
    o_ref[...] = acc


def _post(mix_list, x, qkvm, gate, mem_kv, w_out, final_g, *, dilated, batch, seq, tm=512):
    T = x.shape[0]
    tiles_per_batch = seq // tm
    n_mix = len(mix_list)
    final = final_g is not None
    qm_block = (3 * MIX_WIDTH) // MEM_WIDTH
    in_specs = [pl.BlockSpec((tm, m.shape[1]), lambda i: (i, 0)) for m in mix_list]
    in_specs += [pl.BlockSpec((tm, D_MODEL), lambda i: (i, 0)),
                 pl.BlockSpec((tm, MEM_WIDTH), lambda i: (i, qm_block)),
                 pl.BlockSpec((tm, BRANCH), lambda i: (i, 0)),
                 pl.BlockSpec((None, N_MEM, 2 * MEM_WIDTH), lambda i: (i // tiles_per_batch, 0, 0)),
                 pl.BlockSpec((BRANCH, D_MODEL), lambda i: (0, 0))]
    args = list(mix_list) + [x, qkvm, gate, mem_kv, w_out]
    if final:
        in_specs.append(pl.BlockSpec((1, D_MODEL), lambda i: (0, 0)))
        args.append(final_g)
    return pl.pallas_call(
        functools.partial(_post_kernel, n_mix=n_mix, dilated=dilated, final=final),
        grid=(T // tm,),
        in_specs=in_specs,
        out_specs=pl.BlockSpec((tm, D_MODEL), lambda i: (i, 0)),
        out_shape=jax.ShapeDtypeStruct((T, D_MODEL), F32),
        compiler_params=_compiler_params(1),
        name="post",
    )(*args)


def kernel(x, mem, norm_g, w_in, w_out, mem_norm_g, w_mem_kv, diff_lambda_q1, diff_lambda_k1,
           diff_lambda_q2, diff_lambda_k2, diff_subln_g, final_norm_g):
    batch, seq, d = x.shape
    depth = w_in.shape[0]
    T = batch * seq
    xf = x.reshape(T, d)
    memf = mem.reshape(batch * mem.shape[1], d)

    w_in_b = w_in.astype(BF16)
    w_out_b = w_out.astype(BF16)
    w_kv_b = w_mem_kv.astype(BF16)
    scale = HEAD_DIM ** -0.5
    v_cols = (2 * MIX_WIDTH, 3 * MIX_WIDTH)

    for i in range(depth):
        kind = i % N_MIXERS
        q_scale = scale if kind == 0 else scale * LOG2E
        scales = [(0, MIX_WIDTH, q_scale), (3 * MIX_WIDTH, QKVM_WIDTH, scale)]
        kmean_cols = (MIX_WIDTH, 2 * MIX_WIDTH) if kind == 1 else None
        proj = _rms_proj(xf, norm_g[i].reshape(1, d), w_in_b[i],
                         [(0, QKVM_WIDTH), (QKVM_WIDTH, IN_WIDTH)], [BF16, F32],
                         tm=512, scales=scales, vt_cols=None if kind == 0 else v_cols,
                         kmean_cols=kmean_cols)
        qkvm, gate = proj[0], proj[1]
        (mem_kv,) = _rms_proj(memf, mem_norm_g[i].reshape(1, d), w_kv_b[i],
                              [(0, 2 * MEM_WIDTH)], [BF16], tm=N_MEM)
        mem_kv = mem_kv.reshape(batch, N_MEM, 2 * MEM_WIDTH)
        if kind == 0:
            res = [_dilated_group(qkvm, g, batch=batch, seq=seq) for g in range(len(DSW_GROUPS))]
            mix_list = [r[0] for r in res] + [r[1] for r in res]
        elif kind == 1:
            mix_list = [_moba_mixer(qkvm, proj[2], proj[3], batch=batch, seq=seq)]
        else:
            c = i // N_MIXERS
            lambda_init = 0.8 - 0.6 * math.exp(-0.3 * i)
            lam_vecs = jnp.stack([diff_lambda_q1[c], diff_lambda_k1[c],
                                  diff_lambda_q2[c], diff_lambda_k2[c]]).astype(F32)
            mix_list = [_diff_mixer(qkvm, proj[2], lam_vecs, diff_subln_g[c], lambda_init,
                                    batch=batch, seq=seq)]
        fg = final_norm_g.reshape(1, d) if i == depth - 1 else None
        xf = _post(mix_list, xf, qkvm, gate, mem_kv, w_out_b[i], fg,
                   dilated=(kind == 0), batch=batch, seq=seq)
    return xf.reshape(batch, seq, d)
```

```python
import functools
import math

import numpy as np
import jax
import jax.numpy as jnp
from jax import lax
from jax.experimental import pallas as pl
from jax.experimental.pallas import tpu as pltpu

D_MODEL = 1024
DEPTH = 4
N_MIXERS = 3
HEAD_DIM = 64
MIX_WIDTH = 768
N_MEM_HEADS = 4
MEM_WIDTH = N_MEM_HEADS * HEAD_DIM
N_MEM = 256
BRANCH = MIX_WIDTH + MEM_WIDTH
IN_WIDTH = 3 * MIX_WIDTH + MEM_WIDTH + BRANCH
QKVM_WIDTH = 3 * MIX_WIDTH + MEM_WIDTH
RMS_EPS = 1e-6
SUBLN_EPS = 1e-5
NEG_INF = -1e30
LOG2E = math.log2(math.e)

DSW_GROUPS = ((128, 1), (512, 4), (2048, 16))
DSW_HEADS = 12
DSW_GROUP_WIDTH = 256
BAND_BLOCK = 128

MOBA_HEADS = 12
MOBA_BLOCK = 256
MOBA_TOPK = 3

DIFF_HEADS = 6
DIFF_D = 64

LANES = 128
QUERY_SUB = 256
MOBA_PREP_ROWS = 1024
DILATED_BLOCK_ROWS = 2048
ATTN_PAIRS_PER_CALL = 2
VMEM_LIMIT_BYTES = 56 * 1024 * 1024

BF16 = jnp.bfloat16
F32 = jnp.float32


def _alibi_slopes(n):
    return np.asarray(2.0 ** (-8.0 * np.arange(1, n + 1) / n), dtype=np.float32)


def _log2_slopes(n):
    return np.asarray(_alibi_slopes(n).astype(np.float64) * LOG2E, dtype=np.float32)


def _dot(a, b):
    return jnp.dot(a, b, preferred_element_type=F32)


def _dot_nt(a, b):
    return lax.dot_general(a, b, (((1,), (1,)), ((), ())), preferred_element_type=F32)


def _dot_tn(a, b):
    return lax.dot_general(a, b, (((0,), (0,)), ((), ())), preferred_element_type=F32)


def _compiler_params(n_axes):
    return pltpu.CompilerParams(dimension_semantics=("parallel",) * n_axes,
                                vmem_limit_bytes=VMEM_LIMIT_BYTES)


def _rms_proj_kernel(x_ref, g_ref, w_ref, *out_refs, splits, n_chunk, scales, vt_cols, kmean_cols):
    x = x_ref[...].astype(F32)
    y = x * lax.rsqrt(jnp.mean(x * x, axis=-1, keepdims=True) + RMS_EPS)
    h = (y * g_ref[...]).astype(BF16)
    tm = x.shape[0]
    for (lo, hi), o_ref in zip(splits, out_refs):
        for c in range(lo, hi, n_chunk):
            acc = _dot(h, w_ref[:, c:c + n_chunk])
            for s_lo, s_hi, factor in scales:
                if s_lo <= c < s_hi:
                    acc = acc * factor
            o_ref[:, c - lo:c - lo + n_chunk] = acc.astype(o_ref.dtype)
            if vt_cols is not None and vt_cols[0] <= c < vt_cols[1]:
                vt_ref = out_refs[len(splits)]
                vt_ref[c - vt_cols[0]:c - vt_cols[0] + n_chunk, :] = acc.T.astype(BF16)
            if kmean_cols is not None and kmean_cols[0] <= c < kmean_cols[1]:
                km_ref = out_refs[-1]
                for r in range(tm // MOBA_BLOCK):
                    blk = acc[r * MOBA_BLOCK:(r + 1) * MOBA_BLOCK]
                    km_ref[r, :, c - kmean_cols[0]:c - kmean_cols[0] + n_chunk] = (
                        jnp.mean(blk, axis=0, keepdims=True))


def _rms_proj(x, g, w, splits, dtypes, *, tm, n_chunk=256, scales=(), vt_cols=None,
              kmean_cols=None):
    T, D = x.shape
    N = w.shape[1]
    out_shape = [jax.ShapeDtypeStruct((T, hi - lo), dt) for (lo, hi), dt in zip(splits, dtypes)]
    out_specs = [pl.BlockSpec((tm, hi - lo), lambda i: (i, 0)) for (lo, hi) in splits]
    if vt_cols is not None:
        vw = vt_cols[1] - vt_cols[0]
        out_shape.append(jax.ShapeDtypeStruct((vw, T), BF16))
        out_specs.append(pl.BlockSpec((vw, tm), lambda i: (0, i)))
    if kmean_cols is not None:
        kw = kmean_cols[1] - kmean_cols[0]
        out_shape.append(jax.ShapeDtypeStruct((T // MOBA_BLOCK, 1, kw), F32))
        out_specs.append(pl.BlockSpec((tm // MOBA_BLOCK, 1, kw), lambda i: (i, 0, 0)))
    kern = functools.partial(_rms_proj_kernel, splits=tuple(splits), n_chunk=n_chunk,
                             scales=tuple(scales), vt_cols=vt_cols, kmean_cols=kmean_cols)
    return pl.pallas_call(
        kern,
        grid=(T // tm,),
        in_specs=[pl.BlockSpec((tm, D), lambda i: (i, 0)),
                  pl.BlockSpec((1, D), lambda i: (0, 0)),
                  pl.BlockSpec((D, N), lambda i: (0, 0))],
        out_specs=out_specs,
        out_shape=out_shape,
        compiler_params=_compiler_params(1),
        name="rms_proj",
    )(x, g, w)


def _dilated_kernel(q_ref, kc_ref, kp_ref, vc_ref, vp_ref, o_ref, lse_ref, *, tu, dil, slopes):
    ui = pl.program_id(1)
    pair = pl.program_id(2)
    bb = BAND_BLOCK
    key = lax.broadcasted_iota(jnp.int32, (bb, bb), 0)
    qry = lax.broadcasted_iota(jnp.int32, (bb, bb), 1)
    d_cur = (qry - key).astype(F32)
    d_prev = d_cur + float(bb)
    valid_cur = qry >= key
    valid_prev_static = key >= qry
    lane = lax.broadcasted_iota(jnp.int32, (bb, LANES), 1)
    chan = lax.broadcasted_iota(jnp.int32, (LANES, bb), 0)
    head_slopes = [jnp.where(pair == 0, slopes[hh], slopes[2 + hh]) for hh in range(2)]

    def rows_of(blk, r):
        return pl.ds(blk * bb * dil + r, bb, stride=dil)

    def steps(ref, blk, r):
        return ref[rows_of(blk, r), :].astype(BF16)

    work = []
    for r in range(dil):
        for s in range(tu // bb):
            qs = steps(q_ref, s, r)
            v_cur = steps(vc_ref, s, r)
            if s == 0:
                k_prev, v_prev = steps(kp_ref, 0, r), steps(vp_ref, 0, r)
                valid_prev = jnp.logical_and(valid_prev_static, ui > 0)
            else:
                k_prev, v_prev = steps(kc_ref, s - 1, r), steps(vc_ref, s - 1, r)
                valid_prev = valid_prev_static
            k_cur = steps(kc_ref, s, r)
            scores = []
            for hh in range(2):
                own = (lane >= hh * HEAD_DIM) & (lane < (hh + 1) * HEAD_DIM)
                qh = jnp.where(own, qs, jnp.zeros_like(qs))
                scores.append((_dot_nt(k_cur, qh), _dot_nt(k_prev, qh)))
            work.append((r, s, v_cur, v_prev, valid_prev, scores))

    for r, s, v_cur, v_prev, valid_prev, scores in work:
        o_h, lse_h = [], []
        for hh in range(2):
            s_cur = jnp.where(valid_cur, scores[hh][0] - head_slopes[hh] * d_cur, NEG_INF)
            s_prev = jnp.where(valid_prev, scores[hh][1] - head_slopes[hh] * d_prev, NEG_INF)
            m = jnp.maximum(jnp.max(s_cur, axis=0, keepdims=True),
                            jnp.max(s_prev, axis=0, keepdims=True))
            p_cur = jnp.exp(s_cur - m)
            p_prev = jnp.exp(s_prev - m)
            l = jnp.sum(p_cur, axis=0, keepdims=True) + jnp.sum(p_prev, axis=0, keepdims=True)
            acc = _dot_tn(v_cur, p_cur.astype(BF16)) + _dot_tn(v_prev, p_prev.astype(BF16))
            o_h.append(acc / l)
            lse_h.append(jnp.broadcast_to(m + jnp.log(l), (LANES, bb)))
        first = chan < HEAD_DIM
        o_ref[rows_of(s, r), :] = jnp.where(first, o_h[0], o_h[1]).T
        lse_ref[rows_of(s, r), :] = jnp.where(first, lse_h[0], lse_h[1]).T


def _dilated_group(qkvm, g, *, batch, seq):
    window, dil = DSW_GROUPS[g]
    assert window // dil == BAND_BLOCK
    U = seq // dil
    tu = min(U, DILATED_BLOCK_ROWS // dil)
    gw = DSW_GROUP_WIDTH
    tiles = gw // LANES
    kcol = MIX_WIDTH // LANES
    x = qkvm.reshape(batch, seq, QKVM_WIDTH)
    sub = tu // BAND_BLOCK
    slopes = tuple(float(v) for v in _alibi_slopes(DSW_HEADS)[g * 4:(g + 1) * 4] * np.float32(dil))

    def cur(off):
        return pl.BlockSpec((None, tu * dil, LANES), lambda b, u, p: (b, u, off + g * tiles + p))

    def prev(off):
        return pl.BlockSpec((None, BAND_BLOCK * dil, LANES),
                            lambda b, u, p: (b, jnp.maximum(u * sub - 1, 0), off + g * tiles + p))

    out_spec = pl.BlockSpec((None, tu * dil, LANES), lambda b, u, p: (b, u, p))
    o, lse = pl.pallas_call(
        functools.partial(_dilated_kernel, tu=tu, dil=dil, slopes=slopes),
        grid=(batch, U // tu, tiles),
        in_specs=[cur(0), cur(kcol), prev(kcol), cur(2 * kcol), prev(2 * kcol)],
        out_specs=[out_spec, out_spec],
        out_shape=[jax.ShapeDtypeStruct((batch, seq, gw), F32)] * 2,
        compiler_params=_compiler_params(3),
        name=f"dilated_g{g}",
    )(x, x, x, x, x)
    return o.reshape(batch * seq, gw), lse.reshape(batch * seq, gw)


_SEL_LANES = 32
_ALIBI_PIECES = 3


def _bf16_pieces(v, n):
    out, rest = [], np.float32(v)
    for _ in range(n):
        piece = np.asarray(rest, np.float32).astype(jnp.bfloat16).astype(np.float32)
        out.append(float(piece))
        rest = np.float32(rest - piece)
    assert rest == 0.0, "slope does not split exactly into bf16 pieces"
    return out


def _moba_prep_kernel(const_ref, q_ref, k_ref, km_ref, qa_ref, ka_ref, *, nblk, blocks_per_step):
    tq = MOBA_BLOCK
    nsb = q_ref.shape[0] // tq
    km = km_ref[...]
    lane = lax.broadcasted_iota(jnp.int32, (tq, LANES), 1)
    rowi = lax.broadcasted_iota(jnp.int32, (tq, LANES), 0)
    km_lane = lax.broadcasted_iota(jnp.int32, (nblk, LANES), 1)
    km_row = lax.broadcasted_iota(jnp.int32, (nblk, LANES), 0)
    cand_blk = lax.broadcasted_iota(jnp.int32, (nblk, tq), 0)
    cand_blk_f = cand_blk.astype(F32)
    chains = [(sb, hh) for sb in range(nsb) for hh in range(2)]
    rows = [slice(sb * tq, (sb + 1) * tq) for sb in range(nsb)]
    own_blk = [pl.program_id(2) * nsb + sb for sb in range(nsb)]
    own = [(lane >= hh * HEAD_DIM) & (lane < (hh + 1) * HEAD_DIM) for hh in range(2)]
    off = [(1 - hh) * HEAD_DIM for hh in range(2)]
    km_h = [jnp.where((km_lane >= hh * HEAD_DIM) & (km_lane < (hh + 1) * HEAD_DIM), km, 0.0)
            .astype(BF16) for hh in range(2)]

    gates = []
    for sb, hh in chains:
        q2 = q_ref[rows[sb], :]
        gates.append(_dot_nt(km_h[hh], jnp.where(own[hh], q2, jnp.zeros_like(q2))))

    picked = []
    for c, (sb, hh) in enumerate(chains):
        g = jnp.where(cand_blk < own_blk[sb], gates[c], -jnp.inf)
        sel = jnp.zeros((nblk, tq), jnp.bool_)
        for _ in range(MOBA_TOPK):
            mx = jnp.max(g, axis=0, keepdims=True)
            hit = (g == mx) & (mx > -jnp.inf)
            idx = jnp.min(jnp.where(hit, cand_blk_f, float(nblk)), axis=0, keepdims=True)
            pick = cand_blk_f == idx
            sel = sel | pick
            g = jnp.where(pick, -jnp.inf, g)
        picked.append(jnp.where(sel, 1.0, 0.0).astype(BF16))

    place = [jnp.where(km_lane == km_row + off[hh], 1.0, 0.0).astype(BF16) for hh in range(2)]
    placed = [_dot_tn(picked[c], place[hh]) for c, (sb, hh) in enumerate(chains)]

    for c, (sb, hh) in enumerate(chains):
        n = own_blk[sb]
        blk = lane - off[hh]
        is_sel_lane = (blk >= 0) & (blk < _SEL_LANES)
        sel_val = jnp.where((placed[c] > 0.5) | (blk == n), 0.0, NEG_INF)
        q_extra = jnp.where(is_sel_lane, sel_val, const_ref[hh:hh + 1, :])
        qa = jnp.where(own[hh], q_ref[rows[sb], :].astype(F32), q_extra)
        a0 = off[hh] + _SEL_LANES
        is_fine = (lane >= a0) & (lane < a0 + _ALIBI_PIECES)
        is_coarse = (lane >= a0 + _ALIBI_PIECES) & (lane < a0 + 2 * _ALIBI_PIECES)
        coarse = (n % blocks_per_step).astype(F32)
        k_extra = jnp.where(blk == n, 1.0,
                            jnp.where(is_fine, rowi.astype(F32), jnp.where(is_coarse, coarse, 0.0)))
        ka = jnp.where(own[hh], k_ref[rows[sb], :].astype(F32), k_extra)
        qa_ref[rows[sb], hh * LANES:(hh + 1) * LANES] = qa.astype(BF16)
        ka_ref[rows[sb], hh * LANES:(hh + 1) * LANES] = ka.astype(BF16)


SUM_ROWS = 16


def _softmax_step(st, vt1, carry, shift):
    m, acc = carry
    m_new = jnp.maximum(m, jnp.max(st, axis=0, keepdims=True) + shift)
    p = jnp.exp2((st - (m_new - shift)).astype(BF16))
    acc = jnp.exp2(m - m_new) * acc + _dot(vt1, p)
    return m_new, acc


def _causal_key_steps(scores_fn, vt_fn, step_slopes, i, step):
    nsub = step // QUERY_SUB
    n_chain = len(step_slopes)

    def rows_of(j):
        return pl.ds(pl.multiple_of(j * step, step), step)

    def body(j, carry):
        sts = scores_fn(rows_of(j))
        vts = vt_fn(rows_of(j))
        dj = (j - i).astype(F32)
        return tuple(_softmax_step(sts[c], vts[c], carry[c], step_slopes[c] * dj)
                     for c in range(n_chain))

    init = tuple((jnp.full((1, QUERY_SUB), NEG_INF, F32),
                  jnp.zeros((LANES + SUM_ROWS, QUERY_SUB), F32)) for _ in range(n_chain))
    carry = lax.fori_loop(0, i, body, init)
    sts = scores_fn(rows_of(i))
    vts = vt_fn(rows_of(i))
    key = lax.broadcasted_iota(jnp.int32, (step, QUERY_SUB), 0)
    qry = lax.broadcasted_iota(jnp.int32, (step, QUERY_SUB), 1)
    out = []
    for c in range(n_chain):
        st = jnp.where(qry + (c % nsub) * QUERY_SUB >= key, sts[c], NEG_INF)
        _, acc = _softmax_step(st, vts[c], carry[c], 0.0)
        out.append((acc[LANES:LANES + 1], acc[:LANES]))
    return out


def _with_sum_rows(vt):
    return jnp.concatenate([vt, jnp.ones((SUM_ROWS, vt.shape[1]), vt.dtype)], axis=0)


def _moba_kernel(slope_ref, qa_ref, ka_ref, vt_ref, o_ref, *, step):
    i = pl.program_id(2)
    nsub = step // QUERY_SUB
    n_heads = qa_ref.shape[1] // LANES
    chains = [(h, qs) for h in range(n_heads) for qs in range(nsub)]
    qa = [qa_ref[qs * QUERY_SUB:(qs + 1) * QUERY_SUB, h * LANES:(h + 1) * LANES]
          for h, qs in chains]
    step_slopes = [slope_ref[n_heads * pl.program_id(1) + h] * float(step) for h, _ in chains]

    def scores_fn(rows):
        return tuple(_dot_nt(ka_ref[rows, h * LANES:(h + 1) * LANES], qa[c])
                     for c, (h, _) in enumerate(chains))

    def vt_fn(rows):
        per_pair = [_with_sum_rows(vt_ref[pp * LANES:(pp + 1) * LANES, rows])
                    for pp in range(n_heads // 2)]
        return tuple(per_pair[h // 2] for h, _ in chains)

    res = _causal_key_steps(scores_fn, vt_fn, step_slopes, i, step)
    chan = lax.broadcasted_iota(jnp.int32, (LANES, QUERY_SUB), 0)
    for pp in range(n_heads // 2):
        for qs in range(nsub):
            (l0, acc0), (l1, acc1) = res[2 * pp * nsub + qs], res[(2 * pp + 1) * nsub + qs]
            o_ref[qs * QUERY_SUB:(qs + 1) * QUERY_SUB, pp * LANES:(pp + 1) * LANES] = (
                jnp.where(chan < HEAD_DIM, acc0 / l0, acc1 / l1).T)


def _moba_mixer(qkvm, vt, kmean, *, batch, seq, step=512):
    nblk = seq // MOBA_BLOCK
    assert nblk <= _SEL_LANES and step % MOBA_BLOCK == 0 and seq % step == 0
    blocks_per_step = step // MOBA_BLOCK
    npair = MOBA_HEADS // 2
    x = qkvm.reshape(batch, seq, QKVM_WIDTH)
    km = kmean.reshape(batch, nblk, MIX_WIDTH)
    slopes = _log2_slopes(MOBA_HEADS)
    q_const = np.zeros((npair, 8, LANES), np.float32)
    for p in range(npair):
        for hh in range(2):
            a0 = (1 - hh) * HEAD_DIM + _SEL_LANES
            pieces = _bf16_pieces(slopes[2 * p + hh], _ALIBI_PIECES)
            q_const[p, hh, a0:a0 + _ALIBI_PIECES] = pieces
            q_const[p, hh, a0 + _ALIBI_PIECES:a0 + 2 * _ALIBI_PIECES] = [
                v * MOBA_BLOCK for v in pieces]
    tq = min(seq, MOBA_PREP_ROWS)
    kcol = MIX_WIDTH // LANES
    qa, ka = pl.pallas_call(
        functools.partial(_moba_prep_kernel, nblk=nblk, blocks_per_step=blocks_per_step),
        grid=(batch, npair, seq // tq),
        in_specs=[pl.BlockSpec((None, 8, LANES), lambda b, p, i: (p, 0, 0)),
                  pl.BlockSpec((None, tq, LANES), lambda b, p, i: (b, i, p)),
                  pl.BlockSpec((None, tq, LANES), lambda b, p, i: (b, i, kcol + p)),
                  pl.BlockSpec((None, nblk, LANES), lambda b, p, i: (b, 0, p))],
        out_specs=[pl.BlockSpec((None, tq, 2 * LANES), lambda b, p, i: (b, i, p))] * 2,
        out_shape=[jax.ShapeDtypeStruct((batch, seq, MOBA_HEADS * LANES), BF16)] * 2,
        compiler_params=_compiler_params(3),
        name="moba_prep",
    )(jnp.asarray(q_const), x, x, km)
    ppc = ATTN_PAIRS_PER_CALL
    out = pl.pallas_call(
        functools.partial(_moba_kernel, step=step),
        grid=(batch, npair // ppc, seq // step),
        in_specs=[pl.BlockSpec(memory_space=pltpu.SMEM),
                  pl.BlockSpec((None, step, 2 * ppc * LANES), lambda b, p, i: (b, i, p)),
                  pl.BlockSpec((None, seq, 2 * ppc * LANES), lambda b, p, i: (b, 0, p)),
                  pl.BlockSpec((ppc * LANES, seq), lambda b, p, i: (p, b))],
        out_specs=pl.BlockSpec((None, step, ppc * LANES), lambda b, p, i: (b, i, p)),
        out_shape=jax.ShapeDtypeStruct((batch, seq, MIX_WIDTH), F32),
        compiler_params=_compiler_params(3),
        name="moba_attn",
    )(jnp.asarray(slopes), qa, ka, vt)
    return out.reshape(batch * seq, MIX_WIDTH)


def _position_lanes(step):
    tab = np.zeros((step, LANES), np.float32)
    r = np.arange(step)
    tab[:, 0:_ALIBI_PIECES] = (r % MOBA_BLOCK)[:, None]
    tab[:, _ALIBI_PIECES:2 * _ALIBI_PIECES] = (r // MOBA_BLOCK)[:, None]
    return jnp.asarray(tab, dtype=BF16)


def _diff_kernel(slope_ref, qx_ref, kx_ref, lam_ref, g_ref, q_ref, k_ref, vt_ref, o_ref, *,
                 step, lambda_init):
    i = pl.program_id(2)
    nsub = step // QUERY_SUB
    n_heads = q_ref.shape[1] // LANES
    lane = lax.broadcasted_iota(jnp.int32, (QUERY_SUB, LANES), 1)
    chains = [(h, mi, qs) for h in range(n_heads) for mi in range(2) for qs in range(nsub)]
    qa = []
    for h, mi, qs in chains:
        q2 = q_ref[qs * QUERY_SUB:(qs + 1) * QUERY_SUB, h * LANES:(h + 1) * LANES]
        qx = jnp.broadcast_to(qx_ref[h, 0:1, :], (QUERY_SUB, LANES)).astype(BF16)
        own_map = (lane >= mi * DIFF_D) & (lane < (mi + 1) * DIFF_D)
        qa.append(jnp.concatenate([jnp.where(own_map, q2, jnp.zeros_like(q2)), qx], axis=1))
    kx = kx_ref[...]
    step_slopes = [slope_ref[n_heads * pl.program_id(1) + h] * float(step) for h, _, _ in chains]

    def scores_fn(rows):
        ka = [jnp.concatenate([k_ref[rows, h * LANES:(h + 1) * LANES], kx], axis=1)
              for h in range(n_heads)]
        return tuple(_dot_nt(ka[h], qa[c]) for c, (h, _, _) in enumerate(chains))

    def vt_fn(rows):
        per_head = [_with_sum_rows(vt_ref[h * LANES:(h + 1) * LANES, rows]) for h in range(n_heads)]
        return tuple(per_head[h] for h, _, _ in chains)

    res = _causal_key_steps(scores_fn, vt_fn, step_slopes, i, step)

    lq1, lk1, lq2, lk2 = (lam_ref[r:r + 1, :] for r in range(4))
    lam = (jnp.exp(jnp.sum(lq1 * lk1, axis=1, keepdims=True))
           - jnp.exp(jnp.sum(lq2 * lk2, axis=1, keepdims=True)) + lambda_init)
    for h in range(n_heads):
        for qs in range(nsub):
            (l1, acc1), (l2, acc2) = res[(2 * h) * nsub + qs], res[(2 * h + 1) * nsub + qs]
            o = acc1 / l1 - lam * (acc2 / l2)
            y = o * lax.rsqrt(jnp.mean(o * o, axis=0, keepdims=True) + SUBLN_EPS)
            o_ref[qs * QUERY_SUB:(qs + 1) * QUERY_SUB, h * LANES:(h + 1) * LANES] = (
                (y * g_ref[...]) * (1.0 - lambda_init)).T


def _diff_mixer(qkvm, vt, lam_vecs, subln_g, lambda_init, *, batch, seq, step=512):
    assert step % MOBA_BLOCK == 0 and seq % step == 0
    x = qkvm.reshape(batch, seq, QKVM_WIDTH)
    slopes = _log2_slopes(DIFF_HEADS)
    q_extra = np.zeros((DIFF_HEADS, 8, LANES), np.float32)
    for h in range(DIFF_HEADS):
        pieces = _bf16_pieces(slopes[h], _ALIBI_PIECES)
        q_extra[h, 0, 0:_ALIBI_PIECES] = pieces
        q_extra[h, 0, _ALIBI_PIECES:2 * _ALIBI_PIECES] = [v * MOBA_BLOCK for v in pieces]
    kcol = MIX_WIDTH // LANES
    hpc = ATTN_PAIRS_PER_CALL
    assert DIFF_HEADS % hpc == 0 and kcol % hpc == 0
    out = pl.pallas_call(
        functools.partial(_diff_kernel, step=step, lambda_init=float(lambda_init)),
        grid=(batch, DIFF_HEADS // hpc, seq // step),
        in_specs=[pl.BlockSpec(memory_space=pltpu.SMEM),
                  pl.BlockSpec((hpc, 8, LANES), lambda b, h, i: (h, 0, 0)),
                  pl.BlockSpec((step, LANES), lambda b, h, i: (0, 0)),
                  pl.BlockSpec((4, DIFF_D), lambda b, h, i: (0, 0)),
                  pl.BlockSpec((2 * DIFF_D, 1), lambda b, h, i: (0, 0)),
                  pl.BlockSpec((None, step, hpc * LANES), lambda b, h, i: (b, i, h)),
                  pl.BlockSpec((None, seq, hpc * LANES), lambda b, h, i: (b, 0, kcol // hpc + h)),
                  pl.BlockSpec((hpc * LANES, seq), lambda b, h, i: (h, b))],
        out_specs=pl.BlockSpec((None, step, hpc * LANES), lambda b, h, i: (b, i, h)),
        out_shape=jax.ShapeDtypeStruct((batch, seq, MIX_WIDTH), F32),
        compiler_params=_compiler_params(3),
        name="diff_attn",
    )(jnp.asarray(slopes), jnp.asarray(q_extra), _position_lanes(step), lam_vecs,
      subln_g.reshape(2 * DIFF_D, 1), x, x, vt)
    return out.reshape(batch * seq, MIX_WIDTH)


def _post_kernel(*refs, n_mix, dilated, final):
    mix_refs = refs[:n_mix]
    x_ref, qm_ref, gate_ref, kv_ref, w_ref = refs[n_mix:n_mix + 5]
    rest = refs[n_mix + 5:]
    fg_ref = rest[0] if final else None
    o_ref = rest[-1]
    tm = x_ref.shape[0]

    if dilated:
        outs = [r[...] for r in mix_refs[:3]]
        lses = [r[...] for r in mix_refs[3:]]
        mx = jnp.maximum(jnp.maximum(lses[0], lses[1]), lses[2])
        es = [jnp.exp(v - mx) for v in lses]
        den = es[0] + es[1] + es[2]
        mix_parts = [o * (e / den) for o, e in zip(outs, es)]
    else:
        mix = mix_refs[0][...]
        mix_parts = [mix[:, g * DSW_GROUP_WIDTH:(g + 1) * DSW_GROUP_WIDTH] for g in range(3)]

    lane = lax.broadcasted_iota(jnp.int32, (tm, LANES), 1)
    mem_parts = []
    for pair in range(MEM_WIDTH // LANES):
        cs = slice(pair * LANES, (pair + 1) * LANES)
        q2 = qm_ref[:, cs].astype(BF16)
        k2 = kv_ref[:, cs]
        v2 = kv_ref[:, MEM_WIDTH + pair * LANES:MEM_WIDTH + (pair + 1) * LANES]
        o_h = []
        for hh in range(2):
            own = (lane >= hh * HEAD_DIM) & (lane < (hh + 1) * HEAD_DIM)
            s = _dot_nt(jnp.where(own, q2, jnp.zeros_like(q2)), k2)
            p = jnp.exp(s - jnp.max(s, axis=1, keepdims=True))
            l = jnp.sum(p, axis=1, keepdims=True)
            o_h.append(_dot(p.astype(BF16), v2) / l)
        mem_parts.append(jnp.where(lane < HEAD_DIM, o_h[0], o_h[1]))

    acc = x_ref[...]
    parts = [(g * DSW_GROUP_WIDTH, mix_parts[g]) for g in range(3)]
    parts += [(MIX_WIDTH + p * LANES, mem_parts[p]) for p in range(len(mem_parts))]
    for c0, val in parts:
        w = val.shape[1]
        gt = gate_ref[:, c0:c0 + w]
        y = val * (gt * (1.0 / (1.0 + jnp.exp(-gt))))
        acc = acc + _dot(y.astype(BF16), w_ref[c0:c0 + w, :])
    if final:
        yn = acc * lax.rsqrt(jnp.mean(acc * acc, axis=-1, keepdims=True) + RMS_EPS)
        acc = yn * fg_ref[...]
    o_ref[...] = acc


def _post(mix_list, x, qkvm, gate, mem_kv, w_out, final_g, *, dilated, batch, seq, tm=512):
    T = x.shape[0]
    tiles_per_batch = seq // tm
    n_mix = len(mix_list)
    final = final_g is not None
    qm_block = (3 * MIX_WIDTH) // MEM_WIDTH
    in_specs = [pl.BlockSpec((tm, m.shape[1]), lambda i: (i, 0)) for m in mix_list]
    in_specs += [pl.BlockSpec((tm, D_MODEL), lambda i: (i, 0)),
                 pl.BlockSpec((tm, MEM_WIDTH), lambda i: (i, qm_block)),
                 pl.BlockSpec((tm, BRANCH), lambda i: (i, 0)),
                 pl.BlockSpec((None, N_MEM, 2 * MEM_WIDTH), lambda i: (i // tiles_per_batch, 0, 0)),
                 pl.BlockSpec((BRANCH, D_MODEL), lambda i: (0, 0))]
    args = list(mix_list) + [x, qkvm, gate, mem_kv, w_out]
    if final:
        in_specs.append(pl.BlockSpec((1, D_MODEL), lambda i: (0, 0)))
        args.append(final_g)
    return pl.pallas_call(
        functools.partial(_post_kernel, n_mix=n_mix, dilated=dilated, final=final),
        grid=(T // tm,),
        in_specs=in_specs,
        out_specs=pl.BlockSpec((tm, D_MODEL), lambda i: (i, 0)),
        out_shape=jax.ShapeDtypeStruct((T, D_MODEL), F32),
        compiler_params=_compiler_params(1),
        name="post",
    )(*args)


def kernel(x, mem, norm_g, w_in, w_out, mem_norm_g, w_mem_kv, diff_lambda_q1, diff_lambda_k1,
           diff_lambda_q2, diff_lambda_k2, diff_subln_g, final_norm_g):
    batch, seq, d = x.shape
    depth = w_in.shape[0]
    T = batch * seq
    xf = x.reshape(T, d)
    memf = mem.reshape(batch * mem.shape[1], d)

    w_in_b = w_in.astype(BF16)
    w_out_b = w_out.astype(BF16)
    w_kv_b = w_mem_kv.astype(BF16)
    scale = HEAD_DIM ** -0.5
    v_cols = (2 * MIX_WIDTH, 3 * MIX_WIDTH)

    for i in range(depth):
        kind = i % N_MIXERS
        q_scale = scale if kind == 0 else scale * LOG2E
        scales = [(0, MIX_WIDTH, q_scale), (3 * MIX_WIDTH, QKVM_WIDTH, scale)]
        kmean_cols = (MIX_WIDTH, 2 * MIX_WIDTH) if kind == 1 else None
        proj = _rms_proj(xf, norm_g[i].reshape(1, d), w_in_b[i],
                         [(0, QKVM_WIDTH), (QKVM_WIDTH, IN_WIDTH)],
                         [F32 if kind == 0 else BF16, F32], tm=512, scales=scales,
                         vt_cols=None if kind == 0 else v_cols, kmean_cols=kmean_cols)
        qkvm, gate = proj[0], proj[1]
        (mem_kv,) = _rms_proj(memf, mem_norm_g[i].reshape(1, d), w_kv_b[i],
                              [(0, 2 * MEM_WIDTH)], [BF16], tm=N_MEM)
        mem_kv = mem_kv.reshape(batch, N_MEM, 2 * MEM_WIDTH)
        if kind == 0:
            res = [_dilated_group(qkvm, g, batch=batch, seq=seq) for g in range(len(DSW_GROUPS))]
            mix_list = [r[0] for r in res] + [r[1] for r in res]
        elif kind == 1:
            mix_list = [_moba_mixer(qkvm, proj[2], proj[3], batch=batch, seq=seq)]
        else:
            c = i // N_MIXERS
            lambda_init = 0.8 - 0.6 * math.exp(-0.3 * i)
            lam_vecs = jnp.stack([diff_lambda_q1[c], diff_lambda_k1[c],
                                  diff_lambda_q2[c], diff_lambda_k2[c]]).astype(F32)
            mix_list = [_diff_mixer(qkvm, proj[2], lam_vecs, diff_subln_g[c], lambda_init,
                                    batch=batch, seq=seq)]
        fg = final_norm_g.reshape(1, d) if i == depth - 1 else None
        xf = _post(mix_list, xf, qkvm, gate, mem_kv, w_out_b[i], fg,
                   dilated=(kind == 0), batch=batch, seq=seq)
    return xf.reshape(batch, seq, d)
```

```python
import functools
import math

import numpy as np
import jax
import jax.numpy as jnp
from jax import lax
from jax.experimental import pallas as pl
from jax.experimental.pallas import tpu as pltpu

D_MODEL = 1024
DEPTH = 4
N_MIXERS = 3
HEAD_DIM = 64
MIX_WIDTH = 768
N_MEM_HEADS = 4
MEM_WIDTH = N_MEM_HEADS * HEAD_DIM
N_MEM = 256
BRANCH = MIX_WIDTH + MEM_WIDTH
IN_WIDTH = 3 * MIX_WIDTH + MEM_WIDTH + BRANCH
QKVM_WIDTH = 3 * MIX_WIDTH + MEM_WIDTH
RMS_EPS = 1e-6
SUBLN_EPS = 1e-5
NEG_INF = -1e30
LOG2E = math.log2(math.e)

DSW_GROUPS = ((128, 1), (512, 4), (2048, 16))
DSW_HEADS = 12
DSW_GROUP_WIDTH = 256
BAND_BLOCK = 128

MOBA_HEADS = 12
MOBA_BLOCK = 256
MOBA_TOPK = 3

DIFF_HEADS = 6
DIFF_D = 64

LANES = 128
QUERY_SUB = 256
MOBA_PREP_ROWS = 1024
DILATED_BLOCK_ROWS = 2048
ATTN_PAIRS_PER_CALL = 2
VMEM_LIMIT_BYTES = 56 * 1024 * 1024

BF16 = jnp.bfloat16
F32 = jnp.float32


def _alibi_slopes(n):
    return np.asarray(2.0 ** (-8.0 * np.arange(1, n + 1) / n), dtype=np.float32)


def _log2_slopes(n):
    return np.asarray(_alibi_slopes(n).astype(np.float64) * LOG2E, dtype=np.float32)


def _dot(a, b):
    return jnp.dot(a, b, preferred_element_type=F32)


def _dot_nt(a, b):
    return lax.dot_general(a, b, (((1,), (1,)), ((), ())), preferred_element_type=F32)


def _dot_tn(a, b):
    return lax.dot_general(a, b, (((0,), (0,)), ((), ())), preferred_element_type=F32)


def _compiler_params(n_axes):
    return pltpu.CompilerParams(dimension_semantics=("parallel",) * n_axes,
                                vmem_limit_bytes=VMEM_LIMIT_BYTES)


def _rms_proj_kernel(x_ref, g_ref, w_ref, *out_refs, splits, n_chunk, scales, vt_cols, kmean_cols):
    x = x_ref[...].astype(F32)
    y = x * lax.rsqrt(jnp.mean(x * x, axis=-1, keepdims=True) + RMS_EPS)
    h = (y * g_ref[...]).astype(BF16)
    tm = x.shape[0]
    for (lo, hi), o_ref in zip(splits, out_refs):
        for c in range(lo, hi, n_chunk):
            acc = _dot(h, w_ref[:, c:c + n_chunk])
            for s_lo, s_hi, factor in scales:
                if s_lo <= c < s_hi:
                    acc = acc * factor
            o_ref[:, c - lo:c - lo + n_chunk] = acc.astype(o_ref.dtype)
            if vt_cols is not None and vt_cols[0] <= c < vt_cols[1]:
                vt_ref = out_refs[len(splits)]
                vt_ref[c - vt_cols[0]:c - vt_cols[0] + n_chunk, :] = acc.T.astype(BF16)
            if kmean_cols is not None and kmean_cols[0] <= c < kmean_cols[1]:
                km_ref = out_refs[-1]
                for r in range(tm // MOBA_BLOCK):
                    blk = acc[r * MOBA_BLOCK:(r + 1) * MOBA_BLOCK]
                    km_ref[r, :, c - kmean_cols[0]:c - kmean_cols[0] + n_chunk] = (
                        jnp.mean(blk, axis=0, keepdims=True))


def _rms_proj(x, g, w, splits, dtypes, *, tm, n_chunk=256, scales=(), vt_cols=None,
              kmean_cols=None):
    T, D = x.shape
    N = w.shape[1]
    out_shape = [jax.ShapeDtypeStruct((T, hi - lo), dt) for (lo, hi), dt in zip(splits, dtypes)]
    out_specs = [pl.BlockSpec((tm, hi - lo), lambda i: (i, 0)) for (lo, hi) in splits]
    if vt_cols is not None:
        vw = vt_cols[1] - vt_cols[0]
        out_shape.append(jax.ShapeDtypeStruct((vw, T), BF16))
        out_specs.append(pl.BlockSpec((vw, tm), lambda i: (0, i)))
    if kmean_cols is not None:
        kw = kmean_cols[1] - kmean_cols[0]
        out_shape.append(jax.ShapeDtypeStruct((T // MOBA_BLOCK, 1, kw), F32))
        out_specs.append(pl.BlockSpec((tm // MOBA_BLOCK, 1, kw), lambda i: (i, 0, 0)))
    kern = functools.partial(_rms_proj_kernel, splits=tuple(splits), n_chunk=n_chunk,
                             scales=tuple(scales), vt_cols=vt_cols, kmean_cols=kmean_cols)
    return pl.pallas_call(
        kern,
        grid=(T // tm,),
        in_specs=[pl.BlockSpec((tm, D), lambda i: (i, 0)),
                  pl.BlockSpec((1, D), lambda i: (0, 0)),
                  pl.BlockSpec((D, N), lambda i: (0, 0))],
        out_specs=out_specs,
        out_shape=out_shape,
        compiler_params=_compiler_params(1),
        name="rms_proj",
    )(x, g, w)


def _dilated_kernel(q_ref, kc_ref, kp_ref, vc_ref, vp_ref, o_ref, lse_ref, *, tu, dil, slopes):
    ui = pl.program_id(1)
    pair = pl.program_id(2)
    bb = BAND_BLOCK
    key = lax.broadcasted_iota(jnp.int32, (bb, bb), 0)
    qry = lax.broadcasted_iota(jnp.int32, (bb, bb), 1)
    d_cur = (qry - key).astype(F32)
    d_prev = d_cur + float(bb)
    valid_cur = qry >= key
    valid_prev_static = key >= qry
    lane = lax.broadcasted_iota(jnp.int32, (bb, LANES), 1)
    chan = lax.broadcasted_iota(jnp.int32, (LANES, bb), 0)
    head_slopes = [jnp.where(pair == 0, slopes[hh], slopes[2 + hh]) for hh in range(2)]

    def rows_of(blk, r):
        return pl.ds(blk * bb * dil + r, bb, stride=dil)

    def steps(ref, blk, r):
        return ref[rows_of(blk, r), :].astype(BF16)

    work = []
    for r in range(dil):
        for s in range(tu // bb):
            qs = steps(q_ref, s, r)
            v_cur = steps(vc_ref, s, r)
            if s == 0:
                k_prev, v_prev = steps(kp_ref, 0, r), steps(vp_ref, 0, r)
                valid_prev = jnp.logical_and(valid_prev_static, ui > 0)
            else:
                k_prev, v_prev = steps(kc_ref, s - 1, r), steps(vc_ref, s - 1, r)
                valid_prev = valid_prev_static
            k_cur = steps(kc_ref, s, r)
            scores = []
            for hh in range(2):
                own = (lane >= hh * HEAD_DIM) & (lane < (hh + 1) * HEAD_DIM)
                qh = jnp.where(own, qs, jnp.zeros_like(qs))
                scores.append((_dot_nt(k_cur, qh), _dot_nt(k_prev, qh)))
            work.append((r, s, v_cur, v_prev, valid_prev, scores))

    for r, s, v_cur, v_prev, valid_prev, scores in work:
        o_h, lse_h = [], []
        for hh in range(2):
            s_cur = jnp.where(valid_cur, scores[hh][0] - head_slopes[hh] * d_cur, NEG_INF)
            s_prev = jnp.where(valid_prev, scores[hh][1] - head_slopes[hh] * d_prev, NEG_INF)
            m = jnp.maximum(jnp.max(s_cur, axis=0, keepdims=True),
                            jnp.max(s_prev, axis=0, keepdims=True))
            p_cur = jnp.exp(s_cur - m)
            p_prev = jnp.exp(s_prev - m)
            l = jnp.sum(p_cur, axis=0, keepdims=True) + jnp.sum(p_prev, axis=0, keepdims=True)
            acc = _dot_tn(v_cur, p_cur.astype(BF16)) + _dot_tn(v_prev, p_prev.astype(BF16))
            o_h.append(acc / l)
            lse_h.append(jnp.broadcast_to(m + jnp.log(l), (LANES, bb)))
        first = chan < HEAD_DIM
        o_ref[rows_of(s, r), :] = jnp.where(first, o_h[0], o_h[1]).T
        lse_ref[rows_of(s, r), :] = jnp.where(first, lse_h[0], lse_h[1]).T


def _dilated_group(qkvm, g, *, batch, seq):
    window, dil = DSW_GROUPS[g]
    assert window // dil == BAND_BLOCK
    U = seq // dil
    tu = min(U, DILATED_BLOCK_ROWS // dil)
    gw = DSW_GROUP_WIDTH
    tiles = gw // LANES
    kcol = MIX_WIDTH // LANES
    x = qkvm.reshape(batch, seq, QKVM_WIDTH)
    sub = tu // BAND_BLOCK
    slopes = tuple(float(v) for v in _alibi_slopes(DSW_HEADS)[g * 4:(g + 1) * 4] * np.float32(dil))

    def cur(off):
        return pl.BlockSpec((None, tu * dil, LANES), lambda b, u, p: (b, u, off + g * tiles + p))

    def prev(off):
        return pl.BlockSpec((None, BAND_BLOCK * dil, LANES),
                            lambda b, u, p: (b, jnp.maximum(u * sub - 1, 0), off + g * tiles + p))

    out_spec = pl.BlockSpec((None, tu * dil, LANES), lambda b, u, p: (b, u, p))
    o, lse = pl.pallas_call(
        functools.partial(_dilated_kernel, tu=tu, dil=dil, slopes=slopes),
        grid=(batch, U // tu, tiles),
        in_specs=[cur(0), cur(kcol), prev(kcol), cur(2 * kcol), prev(2 * kcol)],
        out_specs=[out_spec, out_spec],
        out_shape=[jax.ShapeDtypeStruct((batch, seq, gw), F32)] * 2,
        compiler_params=_compiler_params(3),
        name=f"dilated_g{g}",
    )(x, x, x, x, x)
    return o.reshape(batch * seq, gw), lse.reshape(batch * seq, gw)


_SEL_LANES = 32
_ALIBI_PIECES = 3


def _bf16_pieces(v, n):
    out, rest = [], np.float32(v)
    for _ in range(n):
        piece = np.asarray(rest, np.float32).astype(jnp.bfloat16).astype(np.float32)
        out.append(float(piece))
        rest = np.float32(rest - piece)
    assert rest == 0.0, "slope does not split exactly into bf16 pieces"
    return out


def _moba_prep_kernel(const_ref, q_ref, k_ref, km_ref, qa_ref, ka_ref, *, nblk, blocks_per_step):
    tq = MOBA_BLOCK
    nsb = q_ref.shape[0] // tq
    km = km_ref[...]
    lane = lax.broadcasted_iota(jnp.int32, (tq, LANES), 1)
    rowi = lax.broadcasted_iota(jnp.int32, (tq, LANES), 0)
    km_lane = lax.broadcasted_iota(jnp.int32, (nblk, LANES), 1)
    km_row = lax.broadcasted_iota(jnp.int32, (nblk, LANES), 0)
    cand_blk = lax.broadcasted_iota(jnp.int32, (nblk, tq), 0)
    cand_blk_f = cand_blk.astype(F32)
    chains = [(sb, hh) for sb in range(nsb) for hh in range(2)]
    rows = [slice(sb * tq, (sb + 1) * tq) for sb in range(nsb)]
    own_blk = [pl.program_id(2) * nsb + sb for sb in range(nsb)]
    own = [(lane >= hh * HEAD_DIM) & (lane < (hh + 1) * HEAD_DIM) for hh in range(2)]
    off = [(1 - hh) * HEAD_DIM for hh in range(2)]
    km_h = [jnp.where((km_lane >= hh * HEAD_DIM) & (km_lane < (hh + 1) * HEAD_DIM), km, 0.0)
            .astype(BF16) for hh in range(2)]

    gates = []
    for sb, hh in chains:
        q2 = q_ref[rows[sb], :]
        gates.append(_dot_nt(km_h[hh], jnp.where(own[hh], q2, jnp.zeros_like(q2))))

    picked = []
    for c, (sb, hh) in enumerate(chains):
        g = jnp.where(cand_blk < own_blk[sb], gates[c], -jnp.inf)
        sel = jnp.zeros((nblk, tq), jnp.bool_)
        for _ in range(MOBA_TOPK):
            mx = jnp.max(g, axis=0, keepdims=True)
            hit = (g == mx) & (mx > -jnp.inf)
            idx = jnp.min(jnp.where(hit, cand_blk_f, float(nblk)), axis=0, keepdims=True)
            pick = cand_blk_f == idx
            sel = sel | pick
            g = jnp.where(pick, -jnp.inf, g)
        picked.append(jnp.where(sel, 1.0, 0.0).astype(BF16))

    place = [jnp.where(km_lane == km_row + off[hh], 1.0, 0.0).astype(BF16) for hh in range(2)]
    placed = [_dot_tn(picked[c], place[hh]) for c, (sb, hh) in enumerate(chains)]

    for c, (sb, hh) in enumerate(chains):
        n = own_blk[sb]
        blk = lane - off[hh]
        is_sel_lane = (blk >= 0) & (blk < _SEL_LANES)
        sel_val = jnp.where((placed[c] > 0.5) | (blk == n), 0.0, NEG_INF)
        q_extra = jnp.where(is_sel_lane, sel_val, const_ref[hh:hh + 1, :])
        qa = jnp.where(own[hh], q_ref[rows[sb], :].astype(F32), q_extra)
        a0 = off[hh] + _SEL_LANES
        is_fine = (lane >= a0) & (lane < a0 + _ALIBI_PIECES)
        is_coarse = (lane >= a0 + _ALIBI_PIECES) & (lane < a0 + 2 * _ALIBI_PIECES)
        coarse = (n % blocks_per_step).astype(F32)
        k_extra = jnp.where(blk == n, 1.0,
                            jnp.where(is_fine, rowi.astype(F32), jnp.where(is_coarse, coarse, 0.0)))
        ka = jnp.where(own[hh], k_ref[rows[sb], :].astype(F32), k_extra)
        qa_ref[rows[sb], hh * LANES:(hh + 1) * LANES] = qa.astype(BF16)
        ka_ref[rows[sb], hh * LANES:(hh + 1) * LANES] = ka.astype(BF16)


SUM_ROWS = 16


def _softmax_step(st, vt1, carry, shift):
    m, acc = carry
    m_new = jnp.maximum(m, jnp.max(st, axis=0, keepdims=True) + shift)
    p = jnp.exp2((st - (m_new - shift)).astype(BF16))
    acc = jnp.exp2(m - m_new) * acc + _dot(vt1, p)
    return m_new, acc


STALE_MAX_CAP = 64.0


def _stale_max_step(st, vt1, carry, shift):
    m, acc, bad = carry
    p = jnp.exp2((st - (m - shift)).astype(BF16))
    cmax = jnp.max(st, axis=0, keepdims=True) + shift
    m_new = jnp.maximum(m, cmax)
    bad = jnp.maximum(bad, jnp.where(cmax - m > STALE_MAX_CAP, 1.0, 0.0))
    acc = (acc + _dot(vt1, p)) * jnp.exp2(m - m_new)
    return m_new, acc, bad


SCORE_LOOKAHEAD = 6


def _key_step(step_fn, score_fn, vts, carry, shifts, masks=None):
    n_chain = len(carry)
    sts, new = {}, [None] * n_chain
    for t in range(n_chain + SCORE_LOOKAHEAD):
        if t < n_chain:
            sts[t] = score_fn(t)
        c = t - SCORE_LOOKAHEAD
        if c >= 0:
            st = sts.pop(c)
            if masks is not None:
                st = jnp.where(masks[c], st, NEG_INF)
            new[c] = step_fn(st, vts[c], carry[c], shifts[c])
    return tuple(new)


def _causal_key_steps(scores_fn, vt_fn, step_slopes, i, step):
    nsub = step // QUERY_SUB
    n_chain = len(step_slopes)

    def rows_of(j):
        return pl.ds(pl.multiple_of(j * step, step), step)

    def shifts_of(j):
        dj = (j - i).astype(F32)
        return [s * dj for s in step_slopes]

    key = lax.broadcasted_iota(jnp.int32, (step, QUERY_SUB), 0)
    qry = lax.broadcasted_iota(jnp.int32, (step, QUERY_SUB), 1)
    masks = [qry + (c % nsub) * QUERY_SUB >= key for c in range(n_chain)]
    init = tuple((jnp.full((1, QUERY_SUB), NEG_INF, F32),
                  jnp.zeros((LANES + SUM_ROWS, QUERY_SUB), F32)) for _ in range(n_chain))
    own = _key_step(_softmax_step, scores_fn(rows_of(i)), vt_fn(rows_of(i)), init,
                    [0.0] * n_chain, masks)

    def fast_body(j, carry):
        return _key_step(_stale_max_step, scores_fn(rows_of(j)), vt_fn(rows_of(j)), carry,
                         shifts_of(j))

    fast = lax.fori_loop(0, i, fast_body,
                         tuple((m, acc, jnp.zeros((1, QUERY_SUB), F32)) for m, acc in own))
    overflowed = functools.reduce(jnp.maximum, [jnp.max(bad) for _, _, bad in fast]) > 0.0

    def exact_body(j, carry):
        return _key_step(_softmax_step, scores_fn(rows_of(j)), vt_fn(rows_of(j)), carry,
                         shifts_of(j))

    accs = lax.cond(overflowed,
                    lambda: tuple(acc for _, acc in lax.fori_loop(0, i, exact_body, own)),
                    lambda: tuple(acc for _, acc, _ in fast))
    return [(acc[LANES:LANES + 1], acc[:LANES]) for acc in accs]


def _with_sum_rows(vt):
    return jnp.concatenate([vt, jnp.ones((SUM_ROWS, vt.shape[1]), vt.dtype)], axis=0)


def _moba_kernel(slope_ref, qa_ref, ka_ref, vt_ref, o_ref, *, step):
    i = pl.program_id(2)
    nsub = step // QUERY_SUB
    n_heads = qa_ref.shape[1] // LANES
    chains = [(h, qs) for h in range(n_heads) for qs in range(nsub)]
    qa = [qa_ref[qs * QUERY_SUB:(qs + 1) * QUERY_SUB, h * LANES:(h + 1) * LANES]
          for h, qs in chains]
    step_slopes = [slope_ref[n_heads * pl.program_id(1) + h] * float(step) for h, _ in chains]

    def scores_fn(rows):
        return lambda c: _dot_nt(ka_ref[rows, chains[c][0] * LANES:(chains[c][0] + 1) * LANES], qa[c])

    def vt_fn(rows):
        per_pair = [_with_sum_rows(vt_ref[pp * LANES:(pp + 1) * LANES, rows])
                    for pp in range(n_heads // 2)]
        return tuple(per_pair[h // 2] for h, _ in chains)

    res = _causal_key_steps(scores_fn, vt_fn, step_slopes, i, step)
    chan = lax.broadcasted_iota(jnp.int32, (LANES, QUERY_SUB), 0)
    for pp in range(n_heads // 2):
        for qs in range(nsub):
            (l0, acc0), (l1, acc1) = res[2 * pp * nsub + qs], res[(2 * pp + 1) * nsub + qs]
            o_ref[qs * QUERY_SUB:(qs + 1) * QUERY_SUB, pp * LANES:(pp + 1) * LANES] = (
                jnp.where(chan < HEAD_DIM, acc0 / l0, acc1 / l1).T)


def _moba_mixer(qkvm, vt, kmean, *, batch, seq, step=512):
    nblk = seq // MOBA_BLOCK
    assert nblk <= _SEL_LANES and step % MOBA_BLOCK == 0 and seq % step == 0
    blocks_per_step = step // MOBA_BLOCK
    npair = MOBA_HEADS // 2
    x = qkvm.reshape(batch, seq, QKVM_WIDTH)
    km = kmean.reshape(batch, nblk, MIX_WIDTH)
    slopes = _log2_slopes(MOBA_HEADS)
    q_const = np.zeros((npair, 8, LANES), np.float32)
    for p in range(npair):
        for hh in range(2):
            a0 = (1 - hh) * HEAD_DIM + _SEL_LANES
            pieces = _bf16_pieces(slopes[2 * p + hh], _ALIBI_PIECES)
            q_const[p, hh, a0:a0 + _ALIBI_PIECES] = pieces
            q_const[p, hh, a0 + _ALIBI_PIECES:a0 + 2 * _ALIBI_PIECES] = [
                v * MOBA_BLOCK for v in pieces]
    tq = min(seq, MOBA_PREP_ROWS)
    kcol = MIX_WIDTH // LANES
    qa, ka = pl.pallas_call(
        functools.partial(_moba_prep_kernel, nblk=nblk, blocks_per_step=blocks_per_step),
        grid=(batch, npair, seq // tq),
        in_specs=[pl.BlockSpec((None, 8, LANES), lambda b, p, i: (p, 0, 0)),
                  pl.BlockSpec((None, tq, LANES), lambda b, p, i: (b, i, p)),
                  pl.BlockSpec((None, tq, LANES), lambda b, p, i: (b, i, kcol + p)),
                  pl.BlockSpec((None, nblk, LANES), lambda b, p, i: (b, 0, p))],
        out_specs=[pl.BlockSpec((None, tq, 2 * LANES), lambda b, p, i: (b, i, p))] * 2,
        out_shape=[jax.ShapeDtypeStruct((batch, seq, MOBA_HEADS * LANES), BF16)] * 2,
        compiler_params=_compiler_params(3),
        name="moba_prep",
    )(jnp.asarray(q_const), x, x, km)
    ppc = ATTN_PAIRS_PER_CALL
    out = pl.pallas_call(
        functools.partial(_moba_kernel, step=step),
        grid=(batch, npair // ppc, seq // step),
        in_specs=[pl.BlockSpec(memory_space=pltpu.SMEM),
                  pl.BlockSpec((None, step, 2 * ppc * LANES), lambda b, p, i: (b, i, p)),
                  pl.BlockSpec((None, seq, 2 * ppc * LANES), lambda b, p, i: (b, 0, p)),
                  pl.BlockSpec((ppc * LANES, seq), lambda b, p, i: (p, b))],
        out_specs=pl.BlockSpec((None, step, ppc * LANES), lambda b, p, i: (b, i, p)),
        out_shape=jax.ShapeDtypeStruct((batch, seq, MIX_WIDTH), F32),
        compiler_params=_compiler_params(3),
        name="moba_attn",
    )(jnp.asarray(slopes), qa, ka, vt)
    return out.reshape(batch * seq, MIX_WIDTH)


def _position_lanes(step):
    tab = np.zeros((step, LANES), np.float32)
    r = np.arange(step)
    tab[:, 0:_ALIBI_PIECES] = (r % MOBA_BLOCK)[:, None]
    tab[:, _ALIBI_PIECES:2 * _ALIBI_PIECES] = (r // MOBA_BLOCK)[:, None]
    return jnp.asarray(tab, dtype=BF16)


def _diff_kernel(slope_ref, qx_ref, kx_ref, lam_ref, g_ref, q_ref, k_ref, vt_ref, o_ref, *,
                 step, lambda_init):
    i = pl.program_id(2)
    nsub = step // QUERY_SUB
    n_heads = q_ref.shape[1] // LANES
    lane = lax.broadcasted_iota(jnp.int32, (QUERY_SUB, LANES), 1)
    chains = [(h, mi, qs) for h in range(n_heads) for mi in range(2) for qs in range(nsub)]
    qa = []
    for h, mi, qs in chains:
        q2 = q_ref[qs * QUERY_SUB:(qs + 1) * QUERY_SUB, h * LANES:(h + 1) * LANES]
        qx = jnp.broadcast_to(qx_ref[h, 0:1, :], (QUERY_SUB, LANES)).astype(BF16)
        own_map = (lane >= mi * DIFF_D) & (lane < (mi + 1) * DIFF_D)
        qa.append(jnp.concatenate([jnp.where(own_map, q2, jnp.zeros_like(q2)), qx], axis=1))
    kx = kx_ref[...]
    step_slopes = [slope_ref[n_heads * pl.program_id(1) + h] * float(step) for h, _, _ in chains]

    def scores_fn(rows):
        ka = [jnp.concatenate([k_ref[rows, h * LANES:(h + 1) * LANES], kx], axis=1)
              for h in range(n_heads)]
        return lambda c: _dot_nt(ka[chains[c][0]], qa[c])

    def vt_fn(rows):
        per_head = [_with_sum_rows(vt_ref[h * LANES:(h + 1) * LANES, rows]) for h in range(n_heads)]
        return tuple(per_head[h] for h, _, _ in chains)

    res = _causal_key_steps(scores_fn, vt_fn, step_slopes, i, step)

    lq1, lk1, lq2, lk2 = (lam_ref[r:r + 1, :] for r in range(4))
    lam = (jnp.exp(jnp.sum(lq1 * lk1, axis=1, keepdims=True))
           - jnp.exp(jnp.sum(lq2 * lk2, axis=1, keepdims=True)) + lambda_init)
    for h in range(n_heads):
        for qs in range(nsub):
            (l1, acc1), (l2, acc2) = res[(2 * h) * nsub + qs], res[(2 * h + 1) * nsub + qs]
            o = acc1 / l1 - lam * (acc2 / l2)
            y = o * lax.rsqrt(jnp.mean(o * o, axis=0, keepdims=True) + SUBLN_EPS)
            o_ref[qs * QUERY_SUB:(qs + 1) * QUERY_SUB, h * LANES:(h + 1) * LANES] = (
                (y * g_ref[...]) * (1.0 - lambda_init)).T


def _diff_mixer(qkvm, vt, lam_vecs, subln_g, lambda_init, *, batch, seq, step=512):
    assert step % MOBA_BLOCK == 0 and seq % step == 0
    x = qkvm.reshape(batch, seq, QKVM_WIDTH)
    slopes = _log2_slopes(DIFF_HEADS)
    q_extra = np.zeros((DIFF_HEADS, 8, LANES), np.float32)
    for h in range(DIFF_HEADS):
        pieces = _bf16_pieces(slopes[h], _ALIBI_PIECES)
        q_extra[h, 0, 0:_ALIBI_PIECES] = pieces
        q_extra[h, 0, _ALIBI_PIECES:2 * _ALIBI_PIECES] = [v * MOBA_BLOCK for v in pieces]
    kcol = MIX_WIDTH // LANES
    hpc = ATTN_PAIRS_PER_CALL
    assert DIFF_HEADS % hpc == 0 and kcol % hpc == 0
    out = pl.pallas_call(
        functools.partial(_diff_kernel, step=step, lambda_init=float(lambda_init)),
        grid=(batch, DIFF_HEADS // hpc, seq // step),
        in_specs=[pl.BlockSpec(memory_space=pltpu.SMEM),
                  pl.BlockSpec((hpc, 8, LANES), lambda b, h, i: (h, 0, 0)),
                  pl.BlockSpec((step, LANES), lambda b, h, i: (0, 0)),
                  pl.BlockSpec((4, DIFF_D), lambda b, h, i: (0, 0)),
                  pl.BlockSpec((2 * DIFF_D, 1), lambda b, h, i: (0, 0)),
                  pl.BlockSpec((None, step, hpc * LANES), lambda b, h, i: (b, i, h)),
                  pl.BlockSpec((None, seq, hpc * LANES), lambda b, h, i: (b, 0, kcol // hpc + h)),
                  pl.BlockSpec((hpc * LANES, seq), lambda b, h, i: (h, b))],
        out_specs=pl.BlockSpec((None, step, hpc * LANES), lambda b, h, i: (b, i, h)),
        out_shape=jax.ShapeDtypeStruct((batch, seq, MIX_WIDTH), F32),
        compiler_params=_compiler_params(3),
        name="diff_attn",
    )(jnp.asarray(slopes), jnp.asarray(q_extra), _position_lanes(step), lam_vecs,
      subln_g.reshape(2 * DIFF_D, 1), x, x, vt)
    return out.reshape(batch * seq, MIX_WIDTH)


def _post_kernel(*refs, n_mix, dilated, final):
    mix_refs = refs[:n_mix]
    x_ref, qm_ref, gate_ref, kv_ref, w_ref = refs[n_mix:n_mix + 5]
    rest = refs[n_mix + 5:]
    fg_ref = rest[0] if final else None
    o_ref = rest[-1]
    tm = x_ref.shape[0]

    if dilated:
        outs = [r[...] for r in mix_refs[:3]]
        lses = [r[...] for r in mix_refs[3:]]
        mx = jnp.maximum(jnp.maximum(lses[0], lses[1]), lses[2])
        es = [jnp.exp(v - mx) for v in lses]
        den = es[0] + es[1] + es[2]
        mix_parts = [o * (e / den) for o, e in zip(outs, es)]
    else:
        mix = mix_refs[0][...]
        mix_parts = [mix[:, g * DSW_GROUP_WIDTH:(g + 1) * DSW_GROUP_WIDTH] for g in range(3)]

    lane = lax.broadcasted_iota(jnp.int32, (tm, LANES), 1)
    mem_parts = []
    for pair in range(MEM_WIDTH // LANES):
        cs = slice(pair * LANES, (pair + 1) * LANES)
        q2 = qm_ref[:, cs].astype(BF16)
        k2 = kv_ref[:, cs]
        v2 = kv_ref[:, MEM_WIDTH + pair * LANES:MEM_WIDTH + (pair + 1) * LANES]
        o_h = []
        for hh in range(2):
            own = (lane >= hh * HEAD_DIM) & (lane < (hh + 1) * HEAD_DIM)
            s = _dot_nt(jnp.where(own, q2, jnp.zeros_like(q2)), k2)
            p = jnp.exp(s - jnp.max(s, axis=1, keepdims=True))
            l = jnp.sum(p, axis=1, keepdims=True)
            o_h.append(_dot(p.astype(BF16), v2) / l)
        mem_parts.append(jnp.where(lane < HEAD_DIM, o_h[0], o_h[1]))

    parts = [(g * DSW_GROUP_WIDTH, mix_parts[g]) for g in range(3)]
    parts += [(MIX_WIDTH + p * LANES, mem_parts[p]) for p in range(len(mem_parts))]
    ys = []
    for c0, val in parts:
        gt = gate_ref[:, c0:c0 + val.shape[1]]
        ys.append((val * (gt * (1.0 / (1.0 + jnp.exp(-gt))))).astype(BF16))
    acc = x_ref[...] + _dot(jnp.concatenate(ys, axis=1), w_ref[...])
    if final:
        yn = acc * lax.rsqrt(jnp.mean(acc * acc, axis=-1, keepdims=True) + RMS_EPS)
        acc = yn * fg_ref[...]
    o_ref[...] = acc


def _post(mix_list, x, qkvm, gate, mem_kv, w_out, final_g, *, dilated, batch, seq, tm=512):
    T = x.shape[0]
    tiles_per_batch = seq // tm
    n_mix = len(mix_list)
    final = final_g is not None
    qm_block = (3 * MIX_WIDTH) // MEM_WIDTH
    in_specs = [pl.BlockSpec((tm, m.shape[1]), lambda i: (i, 0)) for m in mix_list]
    in_specs += [pl.BlockSpec((tm, D_MODEL), lambda i: (i, 0)),
                 pl.BlockSpec((tm, MEM_WIDTH), lambda i: (i, qm_block)),
                 pl.BlockSpec((tm, BRANCH), lambda i: (i, 0)),
                 pl.BlockSpec((None, N_MEM, 2 * MEM_WIDTH), lambda i: (i // tiles_per_batch, 0, 0)),
                 pl.BlockSpec((BRANCH, D_MODEL), lambda i: (0, 0))]
    args = list(mix_list) + [x, qkvm, gate, mem_kv, w_out]
    if final:
        in_specs.append(pl.BlockSpec((1, D_MODEL), lambda i: (0, 0)))
        args.append(final_g)
    return pl.pallas_call(
        functools.partial(_post_kernel, n_mix=n_mix, dilated=dilated, final=final),
        grid=(T // tm,),
        in_specs=in_specs,
        out_specs=pl.BlockSpec((tm, D_MODEL), lambda i: (i, 0)),
        out_shape=jax.ShapeDtypeStruct((T, D_MODEL), F32),
        compiler_params=_compiler_params(1),
        name="post",
    )(*args)


def kernel(x, mem, norm_g, w_in, w_out, mem_norm_g, w_mem_kv, diff_lambda_q1, diff_lambda_k1,
           diff_lambda_q2, diff_lambda_k2, diff_subln_g, final_norm_g):
    batch, seq, d = x.shape
    depth = w_in.shape[0]
    T = batch * seq
    xf = x.reshape(T, d)
    memf = mem.reshape(batch * mem.shape[1], d)

    w_in_b = w_in.astype(BF16)
    w_out_b = w_out.astype(BF16)
    w_kv_b = w_mem_kv.astype(BF16)
    scale = HEAD_DIM ** -0.5
    v_cols = (2 * MIX_WIDTH, 3 * MIX_WIDTH)

    for i in range(depth):
        kind = i % N_MIXERS
        q_scale = scale if kind == 0 else scale * LOG2E
        scales = [(0, MIX_WIDTH, q_scale), (3 * MIX_WIDTH, QKVM_WIDTH, scale)]
        kmean_cols = (MIX_WIDTH, 2 * MIX_WIDTH) if kind == 1 else None
        proj = _rms_proj(xf, norm_g[i].reshape(1, d), w_in_b[i],
                         [(0, QKVM_WIDTH), (QKVM_WIDTH, IN_WIDTH)],
                         [F32 if kind == 0 else BF16, F32], tm=512, scales=scales,
                         vt_cols=None if kind == 0 else v_cols, kmean_cols=kmean_cols)
        qkvm, gate = proj[0], proj[1]
        (mem_kv,) = _rms_proj(memf, mem_norm_g[i].reshape(1, d), w_kv_b[i],
                              [(0, 2 * MEM_WIDTH)], [BF16], tm=N_MEM)
        mem_kv = mem_kv.reshape(batch, N_MEM, 2 * MEM_WIDTH)
        if kind == 0:
            res = [_dilated_group(qkvm, g, batch=batch, seq=seq) for g in range(len(DSW_GROUPS))]
            mix_list = [r[0] for r in res] + [r[1] for r in res]
        elif kind == 1:
            mix_list = [_moba_mixer(qkvm, proj[2], proj[3], batch=batch, seq=seq)]
        else:
            c = i // N_MIXERS
            lambda_init = 0.8 - 0.6 * math.exp(-0.3 * i)
            lam_vecs = jnp.stack([diff_lambda_q1[c], diff_lambda_k1[c],
                                  diff_lambda_q2[c], diff_lambda_k2[c]]).astype(F32)
            mix_list = [_diff_mixer(qkvm, proj[2], lam_vecs, diff_subln_g[c], lambda_init,
                                    batch=batch, seq=seq)]
        fg = final_norm_g.reshape(1, d) if i == depth - 1 else None
        xf = _post(mix_list, xf, qkvm, gate, mem_kv, w_out_b[i], fg,
                   dilated=(kind == 0), batch=batch, seq=seq)
    return xf.reshape(batch, seq, d)
```

```python
import functools
import math

import numpy as np
import jax
import jax.numpy as jnp
from jax import lax
from jax.experimental import pallas as pl
from jax.experimental.pallas import tpu as pltpu

D_MODEL = 1024
DEPTH = 4
N_MIXERS = 3
HEAD_DIM = 64
MIX_WIDTH = 768
N_MEM_HEADS = 4
MEM_WIDTH = N_MEM_HEADS * HEAD_DIM
N_MEM = 256
BRANCH = MIX_WIDTH + MEM_WIDTH
IN_WIDTH = 3 * MIX_WIDTH + MEM_WIDTH + BRANCH
QKVM_WIDTH = 3 * MIX_WIDTH + MEM_WIDTH
RMS_EPS = 1e-6
SUBLN_EPS = 1e-5
NEG_INF = -1e30
LOG2E = math.log2(math.e)

DSW_GROUPS = ((128, 1), (512, 4), (2048, 16))
DSW_HEADS = 12
DSW_GROUP_WIDTH = 256
BAND_BLOCK = 128

MOBA_HEADS = 12
MOBA_BLOCK = 256
MOBA_TOPK = 3

DIFF_HEADS = 6
DIFF_D = 64

LANES = 128
QUERY_SUB = 256
MOBA_PREP_ROWS = 2048
DILATED_BLOCK_ROWS = 2048
MOBA_PAIRS_PER_CALL = 3
DIFF_HEADS_PER_CALL = 3
VMEM_LIMIT_BYTES = 56 * 1024 * 1024

BF16 = jnp.bfloat16
F32 = jnp.float32


def _alibi_slopes(n):
    return np.asarray(2.0 ** (-8.0 * np.arange(1, n + 1) / n), dtype=np.float32)


def _log2_slopes(n):
    return np.asarray(_alibi_slopes(n).astype(np.float64) * LOG2E, dtype=np.float32)


def _dot(a, b):
    return jnp.dot(a, b, preferred_element_type=F32)


def _dot_nt(a, b):
    return lax.dot_general(a, b, (((1,), (1,)), ((), ())), preferred_element_type=F32)


def _dot_tn(a, b):
    return lax.dot_general(a, b, (((0,), (0,)), ((), ())), preferred_element_type=F32)


def _compiler_params(n_axes):
    return pltpu.CompilerParams(dimension_semantics=("parallel",) * n_axes,
                                vmem_limit_bytes=VMEM_LIMIT_BYTES)


def _rms_proj_kernel(x_ref, g_ref, w_ref, *out_refs, splits, n_chunk, scales, vt_cols, kmean_cols):
    x = x_ref[...].astype(F32)
    y = x * lax.rsqrt(jnp.mean(x * x, axis=-1, keepdims=True) + RMS_EPS)
    h = (y * g_ref[...]).astype(BF16)
    tm = x.shape[0]
    for (lo, hi), o_ref in zip(splits, out_refs):
        for c in range(lo, hi, n_chunk):
            acc = _dot(h, w_ref[:, c:c + n_chunk])
            for s_lo, s_hi, factor in scales:
                if s_lo <= c < s_hi:
                    acc = acc * factor
            o_ref[:, c - lo:c - lo + n_chunk] = acc.astype(o_ref.dtype)
            if vt_cols is not None and vt_cols[0] <= c < vt_cols[1]:
                vt_ref = out_refs[len(splits)]
                vt_ref[c - vt_cols[0]:c - vt_cols[0] + n_chunk, :] = acc.T.astype(BF16)
            if kmean_cols is not None and kmean_cols[0] <= c < kmean_cols[1]:
                km_ref = out_refs[-1]
                for r in range(tm // MOBA_BLOCK):
                    blk = acc[r * MOBA_BLOCK:(r + 1) * MOBA_BLOCK]
                    km_ref[r, :, c - kmean_cols[0]:c - kmean_cols[0] + n_chunk] = (
                        jnp.mean(blk, axis=0, keepdims=True))


def _rms_proj(x, g, w, splits, dtypes, *, tm, n_chunk=256, scales=(), vt_cols=None,
              kmean_cols=None):
    T, D = x.shape
    N = w.shape[1]
    out_shape = [jax.ShapeDtypeStruct((T, hi - lo), dt) for (lo, hi), dt in zip(splits, dtypes)]
    out_specs = [pl.BlockSpec((tm, hi - lo), lambda i: (i, 0)) for (lo, hi) in splits]
    if vt_cols is not None:
        vw = vt_cols[1] - vt_cols[0]
        out_shape.append(jax.ShapeDtypeStruct((vw, T), BF16))
        out_specs.append(pl.BlockSpec((vw, tm), lambda i: (0, i)))
    if kmean_cols is not None:
        kw = kmean_cols[1] - kmean_cols[0]
        out_shape.append(jax.ShapeDtypeStruct((T // MOBA_BLOCK, 1, kw), F32))
        out_specs.append(pl.BlockSpec((tm // MOBA_BLOCK, 1, kw), lambda i: (i, 0, 0)))
    kern = functools.partial(_rms_proj_kernel, splits=tuple(splits), n_chunk=n_chunk,
                             scales=tuple(scales), vt_cols=vt_cols, kmean_cols=kmean_cols)
    return pl.pallas_call(
        kern,
        grid=(T // tm,),
        in_specs=[pl.BlockSpec((tm, D), lambda i: (i, 0)),
                  pl.BlockSpec((1, D), lambda i: (0, 0)),
                  pl.BlockSpec((D, N), lambda i: (0, 0))],
        out_specs=out_specs,
        out_shape=out_shape,
        compiler_params=_compiler_params(1),
        name="rms_proj",
    )(x, g, w)


def _dilated_kernel(q_ref, kc_ref, kp_ref, vc_ref, vp_ref, o_ref, lse_ref, *, tu, dil, slopes):
    ui = pl.program_id(1)
    pair = pl.program_id(2)
    bb = BAND_BLOCK
    key = lax.broadcasted_iota(jnp.int32, (bb, bb), 0)
    qry = lax.broadcasted_iota(jnp.int32, (bb, bb), 1)
    d_cur = (qry - key).astype(F32)
    d_prev = d_cur + float(bb)
    valid_cur = qry >= key
    valid_prev_static = key >= qry
    lane = lax.broadcasted_iota(jnp.int32, (bb, LANES), 1)
    chan = lax.broadcasted_iota(jnp.int32, (LANES, bb), 0)
    head_slopes = [jnp.where(pair == 0, slopes[hh], slopes[2 + hh]) for hh in range(2)]

    def rows_of(blk, r):
        return pl.ds(blk * bb * dil + r, bb, stride=dil)

    def steps(ref, blk, r):
        return ref[rows_of(blk, r), :].astype(BF16)

    work = []
    for r in range(dil):
        for s in range(tu // bb):
            qs = steps(q_ref, s, r)
            v_cur = steps(vc_ref, s, r)
            if s == 0:
                k_prev, v_prev = steps(kp_ref, 0, r), steps(vp_ref, 0, r)
                valid_prev = jnp.logical_and(valid_prev_static, ui > 0)
            else:
                k_prev, v_prev = steps(kc_ref, s - 1, r), steps(vc_ref, s - 1, r)
                valid_prev = valid_prev_static
            k_cur = steps(kc_ref, s, r)
            scores = []
            for hh in range(2):
                own = (lane >= hh * HEAD_DIM) & (lane < (hh + 1) * HEAD_DIM)
                qh = jnp.where(own, qs, jnp.zeros_like(qs))
                scores.append((_dot_nt(k_cur, qh), _dot_nt(k_prev, qh)))
            work.append((r, s, v_cur, v_prev, valid_prev, scores))

    for r, s, v_cur, v_prev, valid_prev, scores in work:
        o_h, lse_h = [], []
        for hh in range(2):
            s_cur = jnp.where(valid_cur, scores[hh][0] - head_slopes[hh] * d_cur, NEG_INF)
            s_prev = jnp.where(valid_prev, scores[hh][1] - head_slopes[hh] * d_prev, NEG_INF)
            m = jnp.maximum(jnp.max(s_cur, axis=0, keepdims=True),
                            jnp.max(s_prev, axis=0, keepdims=True))
            p_cur = jnp.exp(s_cur - m)
            p_prev = jnp.exp(s_prev - m)
            l = jnp.sum(p_cur, axis=0, keepdims=True) + jnp.sum(p_prev, axis=0, keepdims=True)
            acc = _dot_tn(v_cur, p_cur.astype(BF16)) + _dot_tn(v_prev, p_prev.astype(BF16))
            o_h.append(acc / l)
            lse_h.append(jnp.broadcast_to(m + jnp.log(l), (LANES, bb)))
        first = chan < HEAD_DIM
        o_ref[rows_of(s, r), :] = jnp.where(first, o_h[0], o_h[1]).T
        lse_ref[rows_of(s, r), :] = jnp.where(first, lse_h[0], lse_h[1]).T


def _dilated_group(qkvm, g, *, batch, seq):
    window, dil = DSW_GROUPS[g]
    assert window // dil == BAND_BLOCK
    U = seq // dil
    tu = min(U, DILATED_BLOCK_ROWS // dil)
    gw = DSW_GROUP_WIDTH
    tiles = gw // LANES
    kcol = MIX_WIDTH // LANES
    x = qkvm.reshape(batch, seq, QKVM_WIDTH)
    sub = tu // BAND_BLOCK
    slopes = tuple(float(v) for v in _alibi_slopes(DSW_HEADS)[g * 4:(g + 1) * 4] * np.float32(dil))

    def cur(off):
        return pl.BlockSpec((None, tu * dil, LANES), lambda b, u, p: (b, u, off + g * tiles + p))

    def prev(off):
        return pl.BlockSpec((None, BAND_BLOCK * dil, LANES),
                            lambda b, u, p: (b, jnp.maximum(u * sub - 1, 0), off + g * tiles + p))

    out_spec = pl.BlockSpec((None, tu * dil, LANES), lambda b, u, p: (b, u, p))
    o, lse = pl.pallas_call(
        functools.partial(_dilated_kernel, tu=tu, dil=dil, slopes=slopes),
        grid=(batch, U // tu, tiles),
        in_specs=[cur(0), cur(kcol), prev(kcol), cur(2 * kcol), prev(2 * kcol)],
        out_specs=[out_spec, out_spec],
        out_shape=[jax.ShapeDtypeStruct((batch, seq, gw), F32)] * 2,
        compiler_params=_compiler_params(3),
        name=f"dilated_g{g}",
    )(x, x, x, x, x)
    return o.reshape(batch * seq, gw), lse.reshape(batch * seq, gw)


_SEL_LANES = 32
_ALIBI_PIECES = 3


def _bf16_pieces(v, n):
    out, rest = [], np.float32(v)
    for _ in range(n):
        piece = np.asarray(rest, np.float32).astype(jnp.bfloat16).astype(np.float32)
        out.append(float(piece))
        rest = np.float32(rest - piece)
    assert rest == 0.0, "slope does not split exactly into bf16 pieces"
    return out


def _moba_prep_kernel(const_ref, q_ref, k_ref, km_ref, qa_ref, ka_ref, *, nblk, blocks_per_step):
    tq = MOBA_BLOCK
    nsb = q_ref.shape[0] // tq
    km = km_ref[...]
    lane = lax.broadcasted_iota(jnp.int32, (tq, LANES), 1)
    rowi = lax.broadcasted_iota(jnp.int32, (tq, LANES), 0)
    km_lane = lax.broadcasted_iota(jnp.int32, (nblk, LANES), 1)
    km_row = lax.broadcasted_iota(jnp.int32, (nblk, LANES), 0)
    cand_blk = lax.broadcasted_iota(jnp.int32, (nblk, tq), 0)
    cand_blk_f = cand_blk.astype(F32)
    chains = [(sb, hh) for sb in range(nsb) for hh in range(2)]
    rows = [slice(sb * tq, (sb + 1) * tq) for sb in range(nsb)]
    own_blk = [pl.program_id(2) * nsb + sb for sb in range(nsb)]
    own = [(lane >= hh * HEAD_DIM) & (lane < (hh + 1) * HEAD_DIM) for hh in range(2)]
    off = [(1 - hh) * HEAD_DIM for hh in range(2)]
    km_h = [jnp.where((km_lane >= hh * HEAD_DIM) & (km_lane < (hh + 1) * HEAD_DIM), km, 0.0)
            .astype(BF16) for hh in range(2)]

    gates = []
    for sb, hh in chains:
        q2 = q_ref[rows[sb], :]
        gates.append(_dot_nt(km_h[hh], jnp.where(own[hh], q2, jnp.zeros_like(q2))))

    picked = []
    for c, (sb, hh) in enumerate(chains):
        g = jnp.where(cand_blk < own_blk[sb], gates[c], -jnp.inf)
        sel = jnp.zeros((nblk, tq), jnp.bool_)
        for _ in range(MOBA_TOPK):
            mx = jnp.max(g, axis=0, keepdims=True)
            hit = (g == mx) & (mx > -jnp.inf)
            idx = jnp.min(jnp.where(hit, cand_blk_f, float(nblk)), axis=0, keepdims=True)
            pick = cand_blk_f == idx
            sel = sel | pick
            g = jnp.where(pick, -jnp.inf, g)
        picked.append(jnp.where(sel, 1.0, 0.0).astype(BF16))

    place = [jnp.where(km_lane == km_row + off[hh], 1.0, 0.0).astype(BF16) for hh in range(2)]
    placed = [_dot_tn(picked[c], place[hh]) for c, (sb, hh) in enumerate(chains)]

    for c, (sb, hh) in enumerate(chains):
        n = own_blk[sb]
        blk = lane - off[hh]
        is_sel_lane = (blk >= 0) & (blk < _SEL_LANES)
        sel_val = jnp.where((placed[c] > 0.5) | (blk == n), 0.0, NEG_INF)
        q_extra = jnp.where(is_sel_lane, sel_val, const_ref[hh:hh + 1, :])
        qa = jnp.where(own[hh], q_ref[rows[sb], :].astype(F32), q_extra)
        a0 = off[hh] + _SEL_LANES
        is_fine = (lane >= a0) & (lane < a0 + _ALIBI_PIECES)
        is_coarse = (lane >= a0 + _ALIBI_PIECES) & (lane < a0 + 2 * _ALIBI_PIECES)
        coarse = (n % blocks_per_step).astype(F32)
        k_extra = jnp.where(blk == n, 1.0,
                            jnp.where(is_fine, rowi.astype(F32), jnp.where(is_coarse, coarse, 0.0)))
        ka = jnp.where(own[hh], k_ref[rows[sb], :].astype(F32), k_extra)
        qa_ref[rows[sb], hh * LANES:(hh + 1) * LANES] = qa.astype(BF16)
        ka_ref[rows[sb], hh * LANES:(hh + 1) * LANES] = ka.astype(BF16)


SUM_ROWS = 16


def _softmax_step(st, vt1, carry, shift):
    m, acc = carry
    m_new = jnp.maximum(m, jnp.max(st, axis=0, keepdims=True) + shift)
    p = jnp.exp2((st - (m_new - shift)).astype(BF16))
    acc = jnp.exp2(m - m_new) * acc + _dot(vt1, p)
    return m_new, acc


STALE_MAX_CAP = 64.0


def _stale_max_step(st, vt1, carry, shift):
    m, acc, bad = carry
    p = jnp.exp2((st - (m - shift)).astype(BF16))
    cmax = jnp.max(st, axis=0, keepdims=True) + shift
    m_new = jnp.maximum(m, cmax)
    bad = jnp.maximum(bad, jnp.where(cmax - m > STALE_MAX_CAP, 1.0, 0.0))
    acc = (acc + _dot(vt1, p)) * jnp.exp2(m - m_new)
    return m_new, acc, bad


SCORE_LOOKAHEAD = {_softmax_step: 6, _stale_max_step: 3}


def _key_step(step_fn, score_fn, vts, carry, shifts, masks=None):
    n_chain = len(carry)
    lookahead = SCORE_LOOKAHEAD[step_fn]
    sts, new = {}, [None] * n_chain
    for t in range(n_chain + lookahead):
        if t < n_chain:
            sts[t] = score_fn(t)
        c = t - lookahead
        if c >= 0:
            st = sts.pop(c)
            if masks is not None:
                st = jnp.where(masks[c], st, NEG_INF)
            new[c] = step_fn(st, vts[c], carry[c], shifts[c])
    return tuple(new)


def _causal_key_steps(scores_fn, vt_fn, step_slopes, i, step):
    nsub = step // QUERY_SUB
    n_chain = len(step_slopes)

    def rows_of(j):
        return pl.ds(pl.multiple_of(j * step, step), step)

    def shifts_of(j):
        dj = (j - i).astype(F32)
        return [s * dj for s in step_slopes]

    key = lax.broadcasted_iota(jnp.int32, (step, QUERY_SUB), 0)
    qry = lax.broadcasted_iota(jnp.int32, (step, QUERY_SUB), 1)
    masks = [qry + (c % nsub) * QUERY_SUB >= key for c in range(n_chain)]
    init = tuple((jnp.full((1, QUERY_SUB), NEG_INF, F32),
                  jnp.zeros((LANES + SUM_ROWS, QUERY_SUB), F32)) for _ in range(n_chain))
    own = _key_step(_softmax_step, scores_fn(rows_of(i)), vt_fn(rows_of(i)), init,
                    [0.0] * n_chain, masks)

    def fast_body(j, carry):
        return _key_step(_stale_max_step, scores_fn(rows_of(j)), vt_fn(rows_of(j)), carry,
                         shifts_of(j))

    fast = lax.fori_loop(0, i, fast_body,
                         tuple((m, acc, jnp.zeros((1, QUERY_SUB), F32)) for m, acc in own))
    overflowed = functools.reduce(jnp.maximum, [jnp.max(bad) for _, _, bad in fast]) > 0.0

    def exact_body(j, carry):
        return _key_step(_softmax_step, scores_fn(rows_of(j)), vt_fn(rows_of(j)), carry,
                         shifts_of(j))

    accs = lax.cond(overflowed,
                    lambda: tuple(acc for _, acc in lax.fori_loop(0, i, exact_body, own)),
                    lambda: tuple(acc for _, acc, _ in fast))
    return [(acc[LANES:LANES + 1], acc[:LANES]) for acc in accs]


def _with_sum_rows(vt):
    return jnp.concatenate([vt, jnp.ones((SUM_ROWS, vt.shape[1]), vt.dtype)], axis=0)


def _moba_kernel(slope_ref, qa_ref, ka_ref, vt_ref, o_ref, *, step):
    i = pl.program_id(2)
    nsub = step // QUERY_SUB
    n_heads = qa_ref.shape[1] // LANES
    chains = [(h, qs) for h in range(n_heads) for qs in range(nsub)]
    qa = [qa_ref[qs * QUERY_SUB:(qs + 1) * QUERY_SUB, h * LANES:(h + 1) * LANES]
          for h, qs in chains]
    step_slopes = [slope_ref[n_heads * pl.program_id(1) + h] * float(step) for h, _ in chains]

    def scores_fn(rows):
        return lambda c: _dot_nt(ka_ref[rows, chains[c][0] * LANES:(chains[c][0] + 1) * LANES], qa[c])

    def vt_fn(rows):
        per_pair = [_with_sum_rows(vt_ref[pp * LANES:(pp + 1) * LANES, rows])
                    for pp in range(n_heads // 2)]
        return tuple(per_pair[h // 2] for h, _ in chains)

    res = _causal_key_steps(scores_fn, vt_fn, step_slopes, i, step)
    chan = lax.broadcasted_iota(jnp.int32, (LANES, QUERY_SUB), 0)
    for pp in range(n_heads // 2):
        for qs in range(nsub):
            (l0, acc0), (l1, acc1) = res[2 * pp * nsub + qs], res[(2 * pp + 1) * nsub + qs]
            o_ref[qs * QUERY_SUB:(qs + 1) * QUERY_SUB, pp * LANES:(pp + 1) * LANES] = (
                jnp.where(chan < HEAD_DIM, acc0 / l0, acc1 / l1).T)


def _moba_mixer(qkvm, vt, kmean, *, batch, seq, step=512):
    nblk = seq // MOBA_BLOCK
    assert nblk <= _SEL_LANES and step % MOBA_BLOCK == 0 and seq % step == 0
    blocks_per_step = step // MOBA_BLOCK
    npair = MOBA_HEADS // 2
    x = qkvm.reshape(batch, seq, QKVM_WIDTH)
    km = kmean.reshape(batch, nblk, MIX_WIDTH)
    slopes = _log2_slopes(MOBA_HEADS)
    q_const = np.zeros((npair, 8, LANES), np.float32)
    for p in range(npair):
        for hh in range(2):
            a0 = (1 - hh) * HEAD_DIM + _SEL_LANES
            pieces = _bf16_pieces(slopes[2 * p + hh], _ALIBI_PIECES)
            q_const[p, hh, a0:a0 + _ALIBI_PIECES] = pieces
            q_const[p, hh, a0 + _ALIBI_PIECES:a0 + 2 * _ALIBI_PIECES] = [
                v * MOBA_BLOCK for v in pieces]
    tq = min(seq, MOBA_PREP_ROWS)
    kcol = MIX_WIDTH // LANES
    qa, ka = pl.pallas_call(
        functools.partial(_moba_prep_kernel, nblk=nblk, blocks_per_step=blocks_per_step),
        grid=(batch, npair, seq // tq),
        in_specs=[pl.BlockSpec((None, 8, LANES), lambda b, p, i: (p, 0, 0)),
                  pl.BlockSpec((None, tq, LANES), lambda b, p, i: (b, i, p)),
                  pl.BlockSpec((None, tq, LANES), lambda b, p, i: (b, i, kcol + p)),
                  pl.BlockSpec((None, nblk, LANES), lambda b, p, i: (b, 0, p))],
        out_specs=[pl.BlockSpec((None, tq, 2 * LANES), lambda b, p, i: (b, i, p))] * 2,
        out_shape=[jax.ShapeDtypeStruct((batch, seq, MOBA_HEADS * LANES), BF16)] * 2,
        compiler_params=_compiler_params(3),
        name="moba_prep",
    )(jnp.asarray(q_const), x, x, km)
    ppc = MOBA_PAIRS_PER_CALL
    out = pl.pallas_call(
        functools.partial(_moba_kernel, step=step),
        grid=(batch, npair // ppc, seq // step),
        in_specs=[pl.BlockSpec(memory_space=pltpu.SMEM),
                  pl.BlockSpec((None, step, 2 * ppc * LANES), lambda b, p, i: (b, i, p)),
                  pl.BlockSpec((None, seq, 2 * ppc * LANES), lambda b, p, i: (b, 0, p)),
                  pl.BlockSpec((ppc * LANES, seq), lambda b, p, i: (p, b))],
        out_specs=pl.BlockSpec((None, step, ppc * LANES), lambda b, p, i: (b, i, p)),
        out_shape=jax.ShapeDtypeStruct((batch, seq, MIX_WIDTH), F32),
        compiler_params=_compiler_params(3),
        name="moba_attn",
    )(jnp.asarray(slopes), qa, ka, vt)
    return out.reshape(batch * seq, MIX_WIDTH)


def _extras_lane(mi):
    return (1 - mi) * DIFF_D


def _position_lanes(step):
    tab = np.zeros((step, 2 * LANES), np.float32)
    r = np.arange(step)
    for mi in range(2):
        a0 = mi * LANES + _extras_lane(mi)
        tab[:, a0:a0 + _ALIBI_PIECES] = (r % MOBA_BLOCK)[:, None]
        tab[:, a0 + _ALIBI_PIECES:a0 + 2 * _ALIBI_PIECES] = (r // MOBA_BLOCK)[:, None]
    return jnp.asarray(tab, dtype=BF16)


def _diff_kernel(slope_ref, qx_ref, kx_ref, lam_ref, g_ref, q_ref, k_ref, vt_ref, o_ref, *,
                 step, lambda_init):
    i = pl.program_id(2)
    nsub = step // QUERY_SUB
    n_heads = q_ref.shape[1] // LANES
    lane_q = lax.broadcasted_iota(jnp.int32, (QUERY_SUB, LANES), 1)
    lane_k = lax.broadcasted_iota(jnp.int32, (step, LANES), 1)
    chains = [(h, mi, qs) for h in range(n_heads) for mi in range(2) for qs in range(nsub)]
    qa = []
    for h, mi, qs in chains:
        q2 = q_ref[qs * QUERY_SUB:(qs + 1) * QUERY_SUB, h * LANES:(h + 1) * LANES]
        qx = jnp.broadcast_to(qx_ref[h, mi:mi + 1, :], (QUERY_SUB, LANES)).astype(BF16)
        own_map = (lane_q >= mi * DIFF_D) & (lane_q < (mi + 1) * DIFF_D)
        qa.append(jnp.where(own_map, q2, qx))
    kx = [kx_ref[:, mi * LANES:(mi + 1) * LANES] for mi in range(2)]
    own_k = [(lane_k >= mi * DIFF_D) & (lane_k < (mi + 1) * DIFF_D) for mi in range(2)]
    step_slopes = [slope_ref[n_heads * pl.program_id(1) + h] * float(step) for h, _, _ in chains]

    def scores_fn(rows):
        ka = {}
        for h in range(n_heads):
            kb = k_ref[rows, h * LANES:(h + 1) * LANES]
            for mi in range(2):
                ka[h, mi] = jnp.where(own_k[mi], kb, kx[mi])
        return lambda c: _dot_nt(ka[chains[c][0], chains[c][1]], qa[c])

    def vt_fn(rows):
        per_head = [_with_sum_rows(vt_ref[h * LANES:(h + 1) * LANES, rows]) for h in range(n_heads)]
        return tuple(per_head[h] for h, _, _ in chains)

    res = _causal_key_steps(scores_fn, vt_fn, step_slopes, i, step)

    lq1, lk1, lq2, lk2 = (lam_ref[r:r + 1, :] for r in range(4))
    lam = (jnp.exp(jnp.sum(lq1 * lk1, axis=1, keepdims=True))
           - jnp.exp(jnp.sum(lq2 * lk2, axis=1, keepdims=True)) + lambda_init)
    for h in range(n_heads):
        for qs in range(nsub):
            (l1, acc1), (l2, acc2) = res[(2 * h) * nsub + qs], res[(2 * h + 1) * nsub + qs]
            o = acc1 / l1 - lam * (acc2 / l2)
            y = o * lax.rsqrt(jnp.mean(o * o, axis=0, keepdims=True) + SUBLN_EPS)
            o_ref[qs * QUERY_SUB:(qs + 1) * QUERY_SUB, h * LANES:(h + 1) * LANES] = (
                (y * g_ref[...]) * (1.0 - lambda_init)).T


def _diff_mixer(qkvm, vt, lam_vecs, subln_g, lambda_init, *, batch, seq, step=512):
    assert step % MOBA_BLOCK == 0 and seq % step == 0
    x = qkvm.reshape(batch, seq, QKVM_WIDTH)
    slopes = _log2_slopes(DIFF_HEADS)
    q_extra = np.zeros((DIFF_HEADS, 8, LANES), np.float32)
    for h in range(DIFF_HEADS):
        pieces = _bf16_pieces(slopes[h], _ALIBI_PIECES)
        for mi in range(2):
            a0 = _extras_lane(mi)
            q_extra[h, mi, a0:a0 + _ALIBI_PIECES] = pieces
            q_extra[h, mi, a0 + _ALIBI_PIECES:a0 + 2 * _ALIBI_PIECES] = [
                v * MOBA_BLOCK for v in pieces]
    kcol = MIX_WIDTH // LANES
    hpc = DIFF_HEADS_PER_CALL
    assert DIFF_HEADS % hpc == 0 and kcol % hpc == 0
    out = pl.pallas_call(
        functools.partial(_diff_kernel, step=step, lambda_init=float(lambda_init)),
        grid=(batch, DIFF_HEADS // hpc, seq // step),
        in_specs=[pl.BlockSpec(memory_space=pltpu.SMEM),
                  pl.BlockSpec((hpc, 8, LANES), lambda b, h, i: (h, 0, 0)),
                  pl.BlockSpec((step, 2 * LANES), lambda b, h, i: (0, 0)),
                  pl.BlockSpec((4, DIFF_D), lambda b, h, i: (0, 0)),
                  pl.BlockSpec((2 * DIFF_D, 1), lambda b, h, i: (0, 0)),
                  pl.BlockSpec((None, step, hpc * LANES), lambda b, h, i: (b, i, h)),
                  pl.BlockSpec((None, seq, hpc * LANES), lambda b, h, i: (b, 0, kcol // hpc + h)),
                  pl.BlockSpec((hpc * LANES, seq), lambda b, h, i: (h, b))],
        out_specs=pl.BlockSpec((None, step, hpc * LANES), lambda b, h, i: (b, i, h)),
        out_shape=jax.ShapeDtypeStruct((batch, seq, MIX_WIDTH), F32),
        compiler_params=_compiler_params(3),
        name="diff_attn",
    )(jnp.asarray(slopes), jnp.asarray(q_extra), _position_lanes(step), lam_vecs,
      subln_g.reshape(2 * DIFF_D, 1), x, x, vt)
    return out.reshape(batch * seq, MIX_WIDTH)


def _post_kernel(*refs, n_mix, dilated, final):
    mix_refs = refs[:n_mix]
    x_ref, qm_ref, gate_ref, kv_ref, vmt_ref, w_ref = refs[n_mix:n_mix + 6]
    rest = refs[n_mix + 6:]
    fg_ref = rest[0] if final else None
    o_ref = rest[-1]
    tm = x_ref.shape[0]

    if dilated:
        outs = [r[...] for r in mix_refs[:3]]
        lses = [r[...] for r in mix_refs[3:]]
        mx = jnp.maximum(jnp.maximum(lses[0], lses[1]), lses[2])
        es = [jnp.exp(v - mx) for v in lses]
        den = es[0] + es[1] + es[2]
        mix_parts = [o * (e / den) for o, e in zip(outs, es)]
    else:
        mix = mix_refs[0][...]
        mix_parts = [mix[:, g * DSW_GROUP_WIDTH:(g + 1) * DSW_GROUP_WIDTH] for g in range(3)]

    lane = lax.broadcasted_iota(jnp.int32, (tm, LANES), 1)
    chan = lax.broadcasted_iota(jnp.int32, (LANES, tm), 0)
    n_pair = MEM_WIDTH // LANES
    scores = []
    for pair in range(n_pair):
        cs = slice(pair * LANES, (pair + 1) * LANES)
        q2 = qm_ref[:, cs].astype(BF16)
        for hh in range(2):
            own = (lane >= hh * HEAD_DIM) & (lane < (hh + 1) * HEAD_DIM)
            scores.append(_dot_nt(kv_ref[:, cs], jnp.where(own, q2, jnp.zeros_like(q2))))
    mem_parts = []
    for pair in range(n_pair):
        vt1 = _with_sum_rows(vmt_ref[pair * LANES:(pair + 1) * LANES, :])
        o_h = []
        for hh in range(2):
            st = scores[2 * pair + hh]
            p = jnp.exp2((st - jnp.max(st, axis=0, keepdims=True)).astype(BF16))
            acc = _dot(vt1, p)
            o_h.append(acc[:LANES] / acc[LANES:LANES + 1])
        mem_parts.append(jnp.where(chan < HEAD_DIM, o_h[0], o_h[1]).T)

    parts = [(g * DSW_GROUP_WIDTH, mix_parts[g]) for g in range(3)]
    parts += [(MIX_WIDTH + p * LANES, mem_parts[p]) for p in range(len(mem_parts))]
    ys = []
    for c0, val in parts:
        gt = gate_ref[:, c0:c0 + val.shape[1]]
        ys.append((val * (gt * (1.0 / (1.0 + jnp.exp(-gt))))).astype(BF16))
    acc = x_ref[...] + _dot(jnp.concatenate(ys, axis=1), w_ref[...])
    if final:
        yn = acc * lax.rsqrt(jnp.mean(acc * acc, axis=-1, keepdims=True) + RMS_EPS)
        acc = yn * fg_ref[...]
    o_ref[...] = acc


def _post(mix_list, x, qkvm, gate, mem_kv, mem_vt, w_out, final_g, *, dilated, batch, seq, tm=512):
    T = x.shape[0]
    tiles_per_batch = seq // tm
    n_mix = len(mix_list)
    final = final_g is not None
    qm_block = (3 * MIX_WIDTH) // MEM_WIDTH
    in_specs = [pl.BlockSpec((tm, m.shape[1]), lambda i: (i, 0)) for m in mix_list]
    in_specs += [pl.BlockSpec((tm, D_MODEL), lambda i: (i, 0)),
                 pl.BlockSpec((tm, MEM_WIDTH), lambda i: (i, qm_block)),
                 pl.BlockSpec((tm, BRANCH), lambda i: (i, 0)),
                 pl.BlockSpec((None, N_MEM, 2 * MEM_WIDTH), lambda i: (i // tiles_per_batch, 0, 0)),
                 pl.BlockSpec((MEM_WIDTH, N_MEM), lambda i: (0, i // tiles_per_batch)),
                 pl.BlockSpec((BRANCH, D_MODEL), lambda i: (0, 0))]
    args = list(mix_list) + [x, qkvm, gate, mem_kv, mem_vt, w_out]
    if final:
        in_specs.append(pl.BlockSpec((1, D_MODEL), lambda i: (0, 0)))
        args.append(final_g)
    return pl.pallas_call(
        functools.partial(_post_kernel, n_mix=n_mix, dilated=dilated, final=final),
        grid=(T // tm,),
        in_specs=in_specs,
        out_specs=pl.BlockSpec((tm, D_MODEL), lambda i: (i, 0)),
        out_shape=jax.ShapeDtypeStruct((T, D_MODEL), F32),
        compiler_params=_compiler_params(1),
        name="post",
    )(*args)


def kernel(x, mem, norm_g, w_in, w_out, mem_norm_g, w_mem_kv, diff_lambda_q1, diff_lambda_k1,
           diff_lambda_q2, diff_lambda_k2, diff_subln_g, final_norm_g):
    batch, seq, d = x.shape
    depth = w_in.shape[0]
    T = batch * seq
    xf = x.reshape(T, d)
    memf = mem.reshape(batch * mem.shape[1], d)

    w_in_b = w_in.astype(BF16)
    w_out_b = w_out.astype(BF16)
    w_kv_b = w_mem_kv.astype(BF16)
    scale = HEAD_DIM ** -0.5
    v_cols = (2 * MIX_WIDTH, 3 * MIX_WIDTH)

    for i in range(depth):
        kind = i % N_MIXERS
        q_scale = scale if kind == 0 else scale * LOG2E
        scales = [(0, MIX_WIDTH, q_scale), (3 * MIX_WIDTH, QKVM_WIDTH, scale * LOG2E)]
        kmean_cols = (MIX_WIDTH, 2 * MIX_WIDTH) if kind == 1 else None
        proj = _rms_proj(xf, norm_g[i].reshape(1, d), w_in_b[i],
                         [(0, QKVM_WIDTH), (QKVM_WIDTH, IN_WIDTH)],
                         [F32 if kind == 0 else BF16, F32], tm=512, scales=scales,
                         vt_cols=None if kind == 0 else v_cols, kmean_cols=kmean_cols)
        qkvm, gate = proj[0], proj[1]
        mem_kv, mem_vt = _rms_proj(memf, mem_norm_g[i].reshape(1, d), w_kv_b[i],
                                   [(0, 2 * MEM_WIDTH)], [BF16], tm=N_MEM,
                                   vt_cols=(MEM_WIDTH, 2 * MEM_WIDTH))
        mem_kv = mem_kv.reshape(batch, N_MEM, 2 * MEM_WIDTH)
        if kind == 0:
            res = [_dilated_group(qkvm, g, batch=batch, seq=seq) for g in range(len(DSW_GROUPS))]
            mix_list = [r[0] for r in res] + [r[1] for r in res]
        elif kind == 1:
            mix_list = [_moba_mixer(qkvm, proj[2], proj[3], batch=batch, seq=seq)]
        else:
            c = i // N_MIXERS
            lambda_init = 0.8 - 0.6 * math.exp(-0.3 * i)
            lam_vecs = jnp.stack([diff_lambda_q1[c], diff_lambda_k1[c],
                                  diff_lambda_q2[c], diff_lambda_k2[c]]).astype(F32)
            mix_list = [_diff_mixer(qkvm, proj[2], lam_vecs, diff_subln_g[c], lambda_init,
                                    batch=batch, seq=seq)]
        fg = final_norm_g.reshape(1, d) if i == depth - 1 else None
        xf = _post(mix_list, xf, qkvm, gate, mem_kv, mem_vt, w_out_b[i], fg,
                   dilated=(kind == 0), batch=batch, seq=seq)
    return xf.reshape(batch, seq, d)
```

```python
import functools
import math

import numpy as np
import jax
import jax.numpy as jnp
from jax import lax
from jax.experimental import pallas as pl
from jax.experimental.pallas import tpu as pltpu

D_MODEL = 1024
DEPTH = 4
N_MIXERS = 3
HEAD_DIM = 64
MIX_WIDTH = 768
N_MEM_HEADS = 4
MEM_WIDTH = N_MEM_HEADS * HEAD_DIM
N_MEM = 256
BRANCH = MIX_WIDTH + MEM_WIDTH
IN_WIDTH = 3 * MIX_WIDTH + MEM_WIDTH + BRANCH
QKVM_WIDTH = 3 * MIX_WIDTH + MEM_WIDTH
RMS_EPS = 1e-6
SUBLN_EPS = 1e-5
NEG_INF = -1e30
LOG2E = math.log2(math.e)

DSW_GROUPS = ((128, 1), (512, 4), (2048, 16))
DSW_HEADS = 12
DSW_GROUP_WIDTH = 256
BAND_BLOCK = 128

MOBA_HEADS = 12
MOBA_BLOCK = 256
MOBA_TOPK = 3

DIFF_HEADS = 6
DIFF_D = 64

LANES = 128
QUERY_SUB = 256
MOBA_PREP_ROWS = 2048
DILATED_BLOCK_ROWS = 2048
MOBA_PAIRS_PER_CALL = 3
DIFF_HEADS_PER_CALL = 3
VMEM_LIMIT_BYTES = 56 * 1024 * 1024

BF16 = jnp.bfloat16
F32 = jnp.float32


def _alibi_slopes(n):
    return np.asarray(2.0 ** (-8.0 * np.arange(1, n + 1) / n), dtype=np.float32)


def _log2_slopes(n):
    return np.asarray(_alibi_slopes(n).astype(np.float64) * LOG2E, dtype=np.float32)


def _dot(a, b):
    return jnp.dot(a, b, preferred_element_type=F32)


def _dot_nt(a, b):
    return lax.dot_general(a, b, (((1,), (1,)), ((), ())), preferred_element_type=F32)


def _dot_tn(a, b):
    return lax.dot_general(a, b, (((0,), (0,)), ((), ())), preferred_element_type=F32)


def _compiler_params(n_axes):
    return pltpu.CompilerParams(dimension_semantics=("parallel",) * n_axes,
                                vmem_limit_bytes=VMEM_LIMIT_BYTES)


def _rms_proj_kernel(x_ref, g_ref, w_ref, *out_refs, splits, n_chunk, scales, vt_cols, kmean_cols):
    x = x_ref[...].astype(F32)
    y = x * lax.rsqrt(jnp.mean(x * x, axis=-1, keepdims=True) + RMS_EPS)
    h = (y * g_ref[...]).astype(BF16)
    tm = x.shape[0]
    for (lo, hi), o_ref in zip(splits, out_refs):
        for c in range(lo, hi, n_chunk):
            acc = _dot(h, w_ref[:, c:c + n_chunk])
            for s_lo, s_hi, factor in scales:
                if s_lo <= c < s_hi:
                    acc = acc * factor
            o_ref[:, c - lo:c - lo + n_chunk] = acc.astype(o_ref.dtype)
            if vt_cols is not None and vt_cols[0] <= c < vt_cols[1]:
                vt_ref = out_refs[len(splits)]
                vt_ref[c - vt_cols[0]:c - vt_cols[0] + n_chunk, :] = acc.T.astype(BF16)
            if kmean_cols is not None and kmean_cols[0] <= c < kmean_cols[1]:
                km_ref = out_refs[-1]
                for r in range(tm // MOBA_BLOCK):
                    blk = acc[r * MOBA_BLOCK:(r + 1) * MOBA_BLOCK]
                    km_ref[r, :, c - kmean_cols[0]:c - kmean_cols[0] + n_chunk] = (
                        jnp.mean(blk, axis=0, keepdims=True))


def _rms_proj(x, g, w, layer, splits, dtypes, *, tm, n_chunk=256, scales=(), vt_cols=None,
              kmean_cols=None):
    T, D = x.shape
    N = w.shape[2]
    out_shape = [jax.ShapeDtypeStruct((T, hi - lo), dt) for (lo, hi), dt in zip(splits, dtypes)]
    out_specs = [pl.BlockSpec((tm, hi - lo), lambda i: (i, 0)) for (lo, hi) in splits]
    if vt_cols is not None:
        vw = vt_cols[1] - vt_cols[0]
        out_shape.append(jax.ShapeDtypeStruct((vw, T), BF16))
        out_specs.append(pl.BlockSpec((vw, tm), lambda i: (0, i)))
    if kmean_cols is not None:
        kw = kmean_cols[1] - kmean_cols[0]
        out_shape.append(jax.ShapeDtypeStruct((T // MOBA_BLOCK, 1, kw), F32))
        out_specs.append(pl.BlockSpec((tm // MOBA_BLOCK, 1, kw), lambda i: (i, 0, 0)))
    kern = functools.partial(_rms_proj_kernel, splits=tuple(splits), n_chunk=n_chunk,
                             scales=tuple(scales), vt_cols=vt_cols, kmean_cols=kmean_cols)
    return pl.pallas_call(
        kern,
        grid=(T // tm,),
        in_specs=[pl.BlockSpec((tm, D), lambda i: (i, 0)),
                  pl.BlockSpec((None, 1, D), lambda i: (layer, 0, 0)),
                  pl.BlockSpec((None, D, N), lambda i: (layer, 0, 0))],
        out_specs=out_specs,
        out_shape=out_shape,
        compiler_params=_compiler_params(1),
        name="rms_proj",
    )(x, g, w)


def _dilated_kernel(q_ref, kc_ref, kp_ref, vc_ref, vp_ref, o_ref, lse_ref, *, tu, dil, slopes):
    ui = pl.program_id(1)
    pair = pl.program_id(2)
    bb = BAND_BLOCK
    key = lax.broadcasted_iota(jnp.int32, (bb, bb), 0)
    qry = lax.broadcasted_iota(jnp.int32, (bb, bb), 1)
    d_cur = (qry - key).astype(F32)
    d_prev = d_cur + float(bb)
    valid_cur = qry >= key
    valid_prev_static = key >= qry
    lane = lax.broadcasted_iota(jnp.int32, (bb, LANES), 1)
    chan = lax.broadcasted_iota(jnp.int32, (LANES, bb), 0)
    head_slopes = [jnp.where(pair == 0, slopes[hh], slopes[2 + hh]) for hh in range(2)]

    def rows_of(blk, r):
        return pl.ds(blk * bb * dil + r, bb, stride=dil)

    def steps(ref, blk, r):
        return ref[rows_of(blk, r), :].astype(BF16)

    work = []
    for r in range(dil):
        for s in range(tu // bb):
            qs = steps(q_ref, s, r)
            v_cur = steps(vc_ref, s, r)
            if s == 0:
                k_prev, v_prev = steps(kp_ref, 0, r), steps(vp_ref, 0, r)
                valid_prev = jnp.logical_and(valid_prev_static, ui > 0)
            else:
                k_prev, v_prev = steps(kc_ref, s - 1, r), steps(vc_ref, s - 1, r)
                valid_prev = valid_prev_static
            k_cur = steps(kc_ref, s, r)
            scores = []
            for hh in range(2):
                own = (lane >= hh * HEAD_DIM) & (lane < (hh + 1) * HEAD_DIM)
                qh = jnp.where(own, qs, jnp.zeros_like(qs))
                scores.append((_dot_nt(k_cur, qh), _dot_nt(k_prev, qh)))
            work.append((r, s, v_cur, v_prev, valid_prev, scores))

    for r, s, v_cur, v_prev, valid_prev, scores in work:
        o_h, lse_h = [], []
        for hh in range(2):
            s_cur = jnp.where(valid_cur, scores[hh][0] - head_slopes[hh] * d_cur, NEG_INF)
            s_prev = jnp.where(valid_prev, scores[hh][1] - head_slopes[hh] * d_prev, NEG_INF)
            m = jnp.maximum(jnp.max(s_cur, axis=0, keepdims=True),
                            jnp.max(s_prev, axis=0, keepdims=True))
            p_cur = jnp.exp(s_cur - m)
            p_prev = jnp.exp(s_prev - m)
            l = jnp.sum(p_cur, axis=0, keepdims=True) + jnp.sum(p_prev, axis=0, keepdims=True)
            acc = _dot_tn(v_cur, p_cur.astype(BF16)) + _dot_tn(v_prev, p_prev.astype(BF16))
            o_h.append(acc / l)
            lse_h.append(jnp.broadcast_to(m + jnp.log(l), (LANES, bb)))
        first = chan < HEAD_DIM
        o_ref[rows_of(s, r), :] = jnp.where(first, o_h[0], o_h[1]).T
        lse_ref[rows_of(s, r), :] = jnp.where(first, lse_h[0], lse_h[1]).T


def _dilated_group(qkvm, g, *, batch, seq):
    window, dil = DSW_GROUPS[g]
    assert window // dil == BAND_BLOCK
    U = seq // dil
    tu = min(U, DILATED_BLOCK_ROWS // dil)
    gw = DSW_GROUP_WIDTH
    tiles = gw // LANES
    kcol = MIX_WIDTH // LANES
    x = qkvm.reshape(batch, seq, QKVM_WIDTH)
    sub = tu // BAND_BLOCK
    slopes = tuple(float(v) for v in _alibi_slopes(DSW_HEADS)[g * 4:(g + 1) * 4] * np.float32(dil))

    def cur(off):
        return pl.BlockSpec((None, tu * dil, LANES), lambda b, u, p: (b, u, off + g * tiles + p))

    def prev(off):
        return pl.BlockSpec((None, BAND_BLOCK * dil, LANES),
                            lambda b, u, p: (b, jnp.maximum(u * sub - 1, 0), off + g * tiles + p))

    out_spec = pl.BlockSpec((None, tu * dil, LANES), lambda b, u, p: (b, u, p))
    o, lse = pl.pallas_call(
        functools.partial(_dilated_kernel, tu=tu, dil=dil, slopes=slopes),
        grid=(batch, U // tu, tiles),
        in_specs=[cur(0), cur(kcol), prev(kcol), cur(2 * kcol), prev(2 * kcol)],
        out_specs=[out_spec, out_spec],
        out_shape=[jax.ShapeDtypeStruct((batch, seq, gw), F32)] * 2,
        compiler_params=_compiler_params(3),
        name=f"dilated_g{g}",
    )(x, x, x, x, x)
    return o.reshape(batch * seq, gw), lse.reshape(batch * seq, gw)


_SEL_LANES = 32
_ALIBI_PIECES = 3


def _bf16_pieces(v, n):
    out, rest = [], np.float32(v)
    for _ in range(n):
        piece = np.asarray(rest, np.float32).astype(jnp.bfloat16).astype(np.float32)
        out.append(float(piece))
        rest = np.float32(rest - piece)
    assert rest == 0.0, "slope does not split exactly into bf16 pieces"
    return out


def _moba_prep_kernel(const_ref, q_ref, k_ref, km_ref, qa_ref, ka_ref, *, nblk, blocks_per_step):
    tq = MOBA_BLOCK
    nsb = q_ref.shape[0] // tq
    km = km_ref[...]
    lane = lax.broadcasted_iota(jnp.int32, (tq, LANES), 1)
    rowi = lax.broadcasted_iota(jnp.int32, (tq, LANES), 0)
    km_lane = lax.broadcasted_iota(jnp.int32, (nblk, LANES), 1)
    km_row = lax.broadcasted_iota(jnp.int32, (nblk, LANES), 0)
    cand_blk = lax.broadcasted_iota(jnp.int32, (nblk, tq), 0)
    cand_blk_f = cand_blk.astype(F32)
    chains = [(sb, hh) for sb in range(nsb) for hh in range(2)]
    rows = [slice(sb * tq, (sb + 1) * tq) for sb in range(nsb)]
    own_blk = [pl.program_id(2) * nsb + sb for sb in range(nsb)]
    own = [(lane >= hh * HEAD_DIM) & (lane < (hh + 1) * HEAD_DIM) for hh in range(2)]
    off = [(1 - hh) * HEAD_DIM for hh in range(2)]
    km_h = [jnp.where((km_lane >= hh * HEAD_DIM) & (km_lane < (hh + 1) * HEAD_DIM), km, 0.0)
            .astype(BF16) for hh in range(2)]

    gates = []
    for sb, hh in chains:
        q2 = q_ref[rows[sb], :]
        gates.append(_dot_nt(km_h[hh], jnp.where(own[hh], q2, jnp.zeros_like(q2))))

    picked = []
    for c, (sb, hh) in enumerate(chains):
        g = jnp.where(cand_blk < own_blk[sb], gates[c], -jnp.inf)
        sel = jnp.zeros((nblk, tq), jnp.bool_)
        for _ in range(MOBA_TOPK):
            mx = jnp.max(g, axis=0, keepdims=True)
            hit = (g == mx) & (mx > -jnp.inf)
            idx = jnp.min(jnp.where(hit, cand_blk_f, float(nblk)), axis=0, keepdims=True)
            pick = cand_blk_f == idx
            sel = sel | pick
            g = jnp.where(pick, -jnp.inf, g)
        picked.append(jnp.where(sel, 1.0, 0.0).astype(BF16))

    place = [jnp.where(km_lane == km_row + off[hh], 1.0, 0.0).astype(BF16) for hh in range(2)]
    placed = [_dot_tn(picked[c], place[hh]) for c, (sb, hh) in enumerate(chains)]

    for c, (sb, hh) in enumerate(chains):
        n = own_blk[sb]
        blk = lane - off[hh]
        is_sel_lane = (blk >= 0) & (blk < _SEL_LANES)
        sel_val = jnp.where((placed[c] > 0.5) | (blk == n), 0.0, NEG_INF)
        q_extra = jnp.where(is_sel_lane, sel_val, const_ref[hh:hh + 1, :])
        qa = jnp.where(own[hh], q_ref[rows[sb], :].astype(F32), q_extra)
        a0 = off[hh] + _SEL_LANES
        is_fine = (lane >= a0) & (lane < a0 + _ALIBI_PIECES)
        is_coarse = (lane >= a0 + _ALIBI_PIECES) & (lane < a0 + 2 * _ALIBI_PIECES)
        coarse = (n % blocks_per_step).astype(F32)
        k_extra = jnp.where(blk == n, 1.0,
                            jnp.where(is_fine, rowi.astype(F32), jnp.where(is_coarse, coarse, 0.0)))
        ka = jnp.where(own[hh], k_ref[rows[sb], :].astype(F32), k_extra)
        qa_ref[rows[sb], hh * LANES:(hh + 1) * LANES] = qa.astype(BF16)
        ka_ref[rows[sb], hh * LANES:(hh + 1) * LANES] = ka.astype(BF16)


SUM_ROWS = 16


def _softmax_step(st, vt1, carry, shift):
    m, acc = carry
    m_new = jnp.maximum(m, jnp.max(st, axis=0, keepdims=True) + shift)
    p = jnp.exp2((st - (m_new - shift)).astype(BF16))
    acc = jnp.exp2(m - m_new) * acc + _dot(vt1, p)
    return m_new, acc


STALE_MAX_CAP = 64.0


def _stale_max_step(st, vt1, carry, shift):
    m, acc, bad = carry
    p = jnp.exp2((st - (m - shift)).astype(BF16))
    cmax = jnp.max(st, axis=0, keepdims=True) + shift
    m_new = jnp.maximum(m, cmax)
    bad = jnp.maximum(bad, jnp.where(cmax - m > STALE_MAX_CAP, 1.0, 0.0))
    acc = (acc + _dot(vt1, p)) * jnp.exp2(m - m_new)
    return m_new, acc, bad


SCORE_LOOKAHEAD = {_softmax_step: 6, _stale_max_step: 3}


def _key_step(step_fn, score_fn, vts, carry, shifts, masks=None):
    n_chain = len(carry)
    lookahead = SCORE_LOOKAHEAD[step_fn]
    sts, new = {}, [None] * n_chain
    for t in range(n_chain + lookahead):
        if t < n_chain:
            sts[t] = score_fn(t)
        c = t - lookahead
        if c >= 0:
            st = sts.pop(c)
            if masks is not None:
                st = jnp.where(masks[c], st, NEG_INF)
            new[c] = step_fn(st, vts[c], carry[c], shifts[c])
    return tuple(new)


def _causal_key_steps(scores_fn, vt_fn, step_slopes, i, step):
    nsub = step // QUERY_SUB
    n_chain = len(step_slopes)

    def rows_of(j):
        return pl.ds(pl.multiple_of(j * step, step), step)

    def shifts_of(j):
        dj = (j - i).astype(F32)
        return [s * dj for s in step_slopes]

    key = lax.broadcasted_iota(jnp.int32, (step, QUERY_SUB), 0)
    qry = lax.broadcasted_iota(jnp.int32, (step, QUERY_SUB), 1)
    masks = [qry + (c % nsub) * QUERY_SUB >= key for c in range(n_chain)]
    init = tuple((jnp.full((1, QUERY_SUB), NEG_INF, F32),
                  jnp.zeros((LANES + SUM_ROWS, QUERY_SUB), F32)) for _ in range(n_chain))
    own = _key_step(_softmax_step, scores_fn(rows_of(i)), vt_fn(rows_of(i)), init,
                    [0.0] * n_chain, masks)

    def fast_body(j, carry):
        return _key_step(_stale_max_step, scores_fn(rows_of(j)), vt_fn(rows_of(j)), carry,
                         shifts_of(j))

    fast = lax.fori_loop(0, i, fast_body,
                         tuple((m, acc, jnp.zeros((1, QUERY_SUB), F32)) for m, acc in own))
    overflowed = functools.reduce(jnp.maximum, [jnp.max(bad) for _, _, bad in fast]) > 0.0

    def exact_body(j, carry):
        return _key_step(_softmax_step, scores_fn(rows_of(j)), vt_fn(rows_of(j)), carry,
                         shifts_of(j))

    accs = lax.cond(overflowed,
                    lambda: tuple(acc for _, acc in lax.fori_loop(0, i, exact_body, own)),
                    lambda: tuple(acc for _, acc, _ in fast))
    return [(acc[LANES:LANES + 1], acc[:LANES]) for acc in accs]


def _with_sum_rows(vt):
    return jnp.concatenate([vt, jnp.ones((SUM_ROWS, vt.shape[1]), vt.dtype)], axis=0)


def _moba_kernel(slope_ref, qa_ref, ka_ref, vt_ref, o_ref, *, step):
    i = pl.program_id(2)
    nsub = step // QUERY_SUB
    n_heads = qa_ref.shape[1] // LANES
    chains = [(h, qs) for h in range(n_heads) for qs in range(nsub)]
    qa = [qa_ref[qs * QUERY_SUB:(qs + 1) * QUERY_SUB, h * LANES:(h + 1) * LANES]
          for h, qs in chains]
    step_slopes = [slope_ref[n_heads * pl.program_id(1) + h] * float(step) for h, _ in chains]

    def scores_fn(rows):
        return lambda c: _dot_nt(ka_ref[rows, chains[c][0] * LANES:(chains[c][0] + 1) * LANES], qa[c])

    def vt_fn(rows):
        per_pair = [_with_sum_rows(vt_ref[pp * LANES:(pp + 1) * LANES, rows])
                    for pp in range(n_heads // 2)]
        return tuple(per_pair[h // 2] for h, _ in chains)

    res = _causal_key_steps(scores_fn, vt_fn, step_slopes, i, step)
    chan = lax.broadcasted_iota(jnp.int32, (LANES, QUERY_SUB), 0)
    for pp in range(n_heads // 2):
        for qs in range(nsub):
            (l0, acc0), (l1, acc1) = res[2 * pp * nsub + qs], res[(2 * pp + 1) * nsub + qs]
            o_ref[qs * QUERY_SUB:(qs + 1) * QUERY_SUB, pp * LANES:(pp + 1) * LANES] = (
                jnp.where(chan < HEAD_DIM, acc0 / l0, acc1 / l1).T.astype(o_ref.dtype))


def _moba_mixer(qkvm, vt, kmean, *, batch, seq, step=512):
    nblk = seq // MOBA_BLOCK
    assert nblk <= _SEL_LANES and step % MOBA_BLOCK == 0 and seq % step == 0
    blocks_per_step = step // MOBA_BLOCK
    npair = MOBA_HEADS // 2
    x = qkvm.reshape(batch, seq, QKVM_WIDTH)
    km = kmean.reshape(batch, nblk, MIX_WIDTH)
    slopes = _log2_slopes(MOBA_HEADS)
    q_const = np.zeros((npair, 8, LANES), np.float32)
    for p in range(npair):
        for hh in range(2):
            a0 = (1 - hh) * HEAD_DIM + _SEL_LANES
            pieces = _bf16_pieces(slopes[2 * p + hh], _ALIBI_PIECES)
            q_const[p, hh, a0:a0 + _ALIBI_PIECES] = pieces
            q_const[p, hh, a0 + _ALIBI_PIECES:a0 + 2 * _ALIBI_PIECES] = [
                v * MOBA_BLOCK for v in pieces]
    tq = min(seq, MOBA_PREP_ROWS)
    kcol = MIX_WIDTH // LANES
    qa, ka = pl.pallas_call(
        functools.partial(_moba_prep_kernel, nblk=nblk, blocks_per_step=blocks_per_step),
        grid=(batch, npair, seq // tq),
        in_specs=[pl.BlockSpec((None, 8, LANES), lambda b, p, i: (p, 0, 0)),
                  pl.BlockSpec((None, tq, LANES), lambda b, p, i: (b, i, p)),
                  pl.BlockSpec((None, tq, LANES), lambda b, p, i: (b, i, kcol + p)),
                  pl.BlockSpec((None, nblk, LANES), lambda b, p, i: (b, 0, p))],
        out_specs=[pl.BlockSpec((None, tq, 2 * LANES), lambda b, p, i: (b, i, p))] * 2,
        out_shape=[jax.ShapeDtypeStruct((batch, seq, MOBA_HEADS * LANES), BF16)] * 2,
        compiler_params=_compiler_params(3),
        name="moba_prep",
    )(jnp.asarray(q_const), x, x, km)
    ppc = MOBA_PAIRS_PER_CALL
    out = pl.pallas_call(
        functools.partial(_moba_kernel, step=step),
        grid=(batch, npair // ppc, seq // step),
        in_specs=[pl.BlockSpec(memory_space=pltpu.SMEM),
                  pl.BlockSpec((None, step, 2 * ppc * LANES), lambda b, p, i: (b, i, p)),
                  pl.BlockSpec((None, seq, 2 * ppc * LANES), lambda b, p, i: (b, 0, p)),
                  pl.BlockSpec((ppc * LANES, seq), lambda b, p, i: (p, b))],
        out_specs=pl.BlockSpec((None, step, ppc * LANES), lambda b, p, i: (b, i, p)),
        out_shape=jax.ShapeDtypeStruct((batch, seq, MIX_WIDTH), BF16),
        compiler_params=_compiler_params(3),
        name="moba_attn",
    )(jnp.asarray(slopes), qa, ka, vt)
    return out.reshape(batch * seq, MIX_WIDTH)


def _extras_lane(mi):
    return (1 - mi) * DIFF_D


def _position_lanes(step):
    tab = np.zeros((step, 2 * LANES), np.float32)
    r = np.arange(step)
    for mi in range(2):
        a0 = mi * LANES + _extras_lane(mi)
        tab[:, a0:a0 + _ALIBI_PIECES] = (r % MOBA_BLOCK)[:, None]
        tab[:, a0 + _ALIBI_PIECES:a0 + 2 * _ALIBI_PIECES] = (r // MOBA_BLOCK)[:, None]
    return jnp.asarray(tab, dtype=BF16)


def _diff_kernel(slope_ref, qx_ref, kx_ref, lam_ref, g_ref, q_ref, k_ref, vt_ref, o_ref, *,
                 step, lambda_init):
    i = pl.program_id(2)
    nsub = step // QUERY_SUB
    n_heads = q_ref.shape[1] // LANES
    lane_q = lax.broadcasted_iota(jnp.int32, (QUERY_SUB, LANES), 1)
    lane_k = lax.broadcasted_iota(jnp.int32, (step, LANES), 1)
    chains = [(h, mi, qs) for h in range(n_heads) for mi in range(2) for qs in range(nsub)]
    qa = []
    for h, mi, qs in chains:
        q2 = q_ref[qs * QUERY_SUB:(qs + 1) * QUERY_SUB, h * LANES:(h + 1) * LANES]
        qx = jnp.broadcast_to(qx_ref[h, mi:mi + 1, :], (QUERY_SUB, LANES)).astype(BF16)
        own_map = (lane_q >= mi * DIFF_D) & (lane_q < (mi + 1) * DIFF_D)
        qa.append(jnp.where(own_map, q2, qx))
    kx = [kx_ref[:, mi * LANES:(mi + 1) * LANES] for mi in range(2)]
    own_k = [(lane_k >= mi * DIFF_D) & (lane_k < (mi + 1) * DIFF_D) for mi in range(2)]
    step_slopes = [slope_ref[n_heads * pl.program_id(1) + h] * float(step) for h, _, _ in chains]

    def scores_fn(rows):
        ka = {}
        for h in range(n_heads):
            kb = k_ref[rows, h * LANES:(h + 1) * LANES]
            for mi in range(2):
                ka[h, mi] = jnp.where(own_k[mi], kb, kx[mi])
        return lambda c: _dot_nt(ka[chains[c][0], chains[c][1]], qa[c])

    def vt_fn(rows):
        per_head = [_with_sum_rows(vt_ref[h * LANES:(h + 1) * LANES, rows]) for h in range(n_heads)]
        return tuple(per_head[h] for h, _, _ in chains)

    res = _causal_key_steps(scores_fn, vt_fn, step_slopes, i, step)

    lq1, lk1, lq2, lk2 = (lam_ref[r:r + 1, :] for r in range(4))
    lam = (jnp.exp(jnp.sum(lq1 * lk1, axis=1, keepdims=True))
           - jnp.exp(jnp.sum(lq2 * lk2, axis=1, keepdims=True)) + lambda_init)
    for h in range(n_heads):
        for qs in range(nsub):
            (l1, acc1), (l2, acc2) = res[(2 * h) * nsub + qs], res[(2 * h + 1) * nsub + qs]
            o = acc1 / l1 - lam * (acc2 / l2)
            y = o * lax.rsqrt(jnp.mean(o * o, axis=0, keepdims=True) + SUBLN_EPS)
            o_ref[qs * QUERY_SUB:(qs + 1) * QUERY_SUB, h * LANES:(h + 1) * LANES] = (
                (y * g_ref[...]) * (1.0 - lambda_init)).T.astype(o_ref.dtype)


def _diff_mixer(qkvm, vt, lam_vecs, subln_g, lambda_init, *, batch, seq, step=512):
    assert step % MOBA_BLOCK == 0 and seq % step == 0
    x = qkvm.reshape(batch, seq, QKVM_WIDTH)
    slopes = _log2_slopes(DIFF_HEADS)
    q_extra = np.zeros((DIFF_HEADS, 8, LANES), np.float32)
    for h in range(DIFF_HEADS):
        pieces = _bf16_pieces(slopes[h], _ALIBI_PIECES)
        for mi in range(2):
            a0 = _extras_lane(mi)
            q_extra[h, mi, a0:a0 + _ALIBI_PIECES] = pieces
            q_extra[h, mi, a0 + _ALIBI_PIECES:a0 + 2 * _ALIBI_PIECES] = [
                v * MOBA_BLOCK for v in pieces]
    kcol = MIX_WIDTH // LANES
    hpc = DIFF_HEADS_PER_CALL
    assert DIFF_HEADS % hpc == 0 and kcol % hpc == 0
    out = pl.pallas_call(
        functools.partial(_diff_kernel, step=step, lambda_init=float(lambda_init)),
        grid=(batch, DIFF_HEADS // hpc, seq // step),
        in_specs=[pl.BlockSpec(memory_space=pltpu.SMEM),
                  pl.BlockSpec((hpc, 8, LANES), lambda b, h, i: (h, 0, 0)),
                  pl.BlockSpec((step, 2 * LANES), lambda b, h, i: (0, 0)),
                  pl.BlockSpec((4, DIFF_D), lambda b, h, i: (0, 0)),
                  pl.BlockSpec((2 * DIFF_D, 1), lambda b, h, i: (0, 0)),
                  pl.BlockSpec((None, step, hpc * LANES), lambda b, h, i: (b, i, h)),
                  pl.BlockSpec((None, seq, hpc * LANES), lambda b, h, i: (b, 0, kcol // hpc + h)),
                  pl.BlockSpec((hpc * LANES, seq), lambda b, h, i: (h, b))],
        out_specs=pl.BlockSpec((None, step, hpc * LANES), lambda b, h, i: (b, i, h)),
        out_shape=jax.ShapeDtypeStruct((batch, seq, MIX_WIDTH), BF16),
        compiler_params=_compiler_params(3),
        name="diff_attn",
    )(jnp.asarray(slopes), jnp.asarray(q_extra), _position_lanes(step), lam_vecs,
      subln_g.reshape(2 * DIFF_D, 1), x, x, vt)
    return out.reshape(batch * seq, MIX_WIDTH)


def _post_kernel(*refs, n_mix, dilated, final):
    mix_refs = refs[:n_mix]
    x_ref, qm_ref, gate_ref, kv_ref, vmt_ref, w_ref = refs[n_mix:n_mix + 6]
    rest = refs[n_mix + 6:]
    fg_ref = rest[0] if final else None
    o_ref = rest[-1]
    tm = x_ref.shape[0]

    if dilated:
        outs = [r[...] for r in mix_refs[:3]]
        lses = [r[...] for r in mix_refs[3:]]
        mx = jnp.maximum(jnp.maximum(lses[0], lses[1]), lses[2])
        es = [jnp.exp(v - mx) for v in lses]
        den = es[0] + es[1] + es[2]
        mix_parts = [o * (e / den) for o, e in zip(outs, es)]
    else:
        mix = mix_refs[0][...].astype(F32)
        mix_parts = [mix[:, g * DSW_GROUP_WIDTH:(g + 1) * DSW_GROUP_WIDTH] for g in range(3)]

    lane = lax.broadcasted_iota(jnp.int32, (tm, LANES), 1)
    chan = lax.broadcasted_iota(jnp.int32, (LANES, tm), 0)
    n_pair = MEM_WIDTH // LANES
    scores = []
    for pair in range(n_pair):
        cs = slice(pair * LANES, (pair + 1) * LANES)
        q2 = qm_ref[:, cs].astype(BF16)
        for hh in range(2):
            own = (lane >= hh * HEAD_DIM) & (lane < (hh + 1) * HEAD_DIM)
            scores.append(_dot_nt(kv_ref[:, cs], jnp.where(own, q2, jnp.zeros_like(q2))))
    mem_parts = []
    for pair in range(n_pair):
        vt1 = _with_sum_rows(vmt_ref[pair * LANES:(pair + 1) * LANES, :])
        o_h = []
        for hh in range(2):
            st = scores[2 * pair + hh]
            p = jnp.exp2((st - jnp.max(st, axis=0, keepdims=True)).astype(BF16))
            acc = _dot(vt1, p)
            o_h.append(acc[:LANES] / acc[LANES:LANES + 1])
        mem_parts.append(jnp.where(chan < HEAD_DIM, o_h[0], o_h[1]).T)

    parts = [(g * DSW_GROUP_WIDTH, mix_parts[g]) for g in range(3)]
    parts += [(MIX_WIDTH + p * LANES, mem_parts[p]) for p in range(len(mem_parts))]
    ys = []
    for c0, val in parts:
        gt = gate_ref[:, c0:c0 + val.shape[1]].astype(F32)
        ys.append((val * (gt * (1.0 / (1.0 + jnp.exp(-gt))))).astype(BF16))
    acc = x_ref[...] + _dot(jnp.concatenate(ys, axis=1), w_ref[...])
    if final:
        yn = acc * lax.rsqrt(jnp.mean(acc * acc, axis=-1, keepdims=True) + RMS_EPS)
        acc = yn * fg_ref[...]
    o_ref[...] = acc


def _post(mix_list, x, qkvm, gate, mem_kv, mem_vt, w_out, layer, final_g, *, dilated, batch, seq,
          tm=512):
    T = x.shape[0]
    tiles_per_batch = seq // tm
    n_mix = len(mix_list)
    final = final_g is not None
    qm_block = (3 * MIX_WIDTH) // MEM_WIDTH
    in_specs = [pl.BlockSpec((tm, m.shape[1]), lambda i: (i, 0)) for m in mix_list]
    in_specs += [pl.BlockSpec((tm, D_MODEL), lambda i: (i, 0)),
                 pl.BlockSpec((tm, MEM_WIDTH), lambda i: (i, qm_block)),
                 pl.BlockSpec((tm, BRANCH), lambda i: (i, 0)),
                 pl.BlockSpec((None, N_MEM, 2 * MEM_WIDTH), lambda i: (i // tiles_per_batch, 0, 0)),
                 pl.BlockSpec((MEM_WIDTH, N_MEM), lambda i: (0, i // tiles_per_batch)),
                 pl.BlockSpec((None, BRANCH, D_MODEL), lambda i: (layer, 0, 0))]
    args = list(mix_list) + [x, qkvm, gate, mem_kv, mem_vt, w_out]
    if final:
        in_specs.append(pl.BlockSpec((1, D_MODEL), lambda i: (0, 0)))
        args.append(final_g)
    return pl.pallas_call(
        functools.partial(_post_kernel, n_mix=n_mix, dilated=dilated, final=final),
        grid=(T // tm,),
        in_specs=in_specs,
        out_specs=pl.BlockSpec((tm, D_MODEL), lambda i: (i, 0)),
        out_shape=jax.ShapeDtypeStruct((T, D_MODEL), F32),
        compiler_params=_compiler_params(1),
        name="post",
    )(*args)


def kernel(x, mem, norm_g, w_in, w_out, mem_norm_g, w_mem_kv, diff_lambda_q1, diff_lambda_k1,
           diff_lambda_q2, diff_lambda_k2, diff_subln_g, final_norm_g):
    batch, seq, d = x.shape
    depth = w_in.shape[0]
    T = batch * seq
    xf = x.reshape(T, d)
    memf = mem.reshape(batch * mem.shape[1], d)

    w_in_b = w_in.astype(BF16)
    w_out_b = w_out.astype(BF16)
    w_kv_b = w_mem_kv.astype(BF16)
    norm_g3 = norm_g.reshape(depth, 1, d)
    mem_norm_g3 = mem_norm_g.reshape(depth, 1, d)
    scale = HEAD_DIM ** -0.5
    v_cols = (2 * MIX_WIDTH, 3 * MIX_WIDTH)

    for i in range(depth):
        kind = i % N_MIXERS
        q_scale = scale if kind == 0 else scale * LOG2E
        scales = [(0, MIX_WIDTH, q_scale), (3 * MIX_WIDTH, QKVM_WIDTH, scale * LOG2E)]
        kmean_cols = (MIX_WIDTH, 2 * MIX_WIDTH) if kind == 1 else None
        proj = _rms_proj(xf, norm_g3, w_in_b, i, [(0, QKVM_WIDTH), (QKVM_WIDTH, IN_WIDTH)],
                         [F32 if kind == 0 else BF16, BF16], tm=512, scales=scales,
                         vt_cols=None if kind == 0 else v_cols, kmean_cols=kmean_cols)
        qkvm, gate = proj[0], proj[1]
        mem_kv, mem_vt = _rms_proj(memf, mem_norm_g3, w_kv_b, i, [(0, 2 * MEM_WIDTH)], [BF16], tm=N_MEM,
                                   vt_cols=(MEM_WIDTH, 2 * MEM_WIDTH))
        mem_kv = mem_kv.reshape(batch, N_MEM, 2 * MEM_WIDTH)
        if kind == 0:
            res = [_dilated_group(qkvm, g, batch=batch, seq=seq) for g in range(len(DSW_GROUPS))]
            mix_list = [r[0] for r in res] + [r[1] for r in res]
        elif kind == 1:
            mix_list = [_moba_mixer(qkvm, proj[2], proj[3], batch=batch, seq=seq)]
        else:
            c = i // N_MIXERS
            lambda_init = 0.8 - 0.6 * math.exp(-0.3 * i)
            lam_vecs = jnp.stack([diff_lambda_q1[c], diff_lambda_k1[c],
                                  diff_lambda_q2[c], diff_lambda_k2[c]]).astype(F32)
            mix_list = [_diff_mixer(qkvm, proj[2], lam_vecs, diff_subln_g[c], lambda_init,
                                    batch=batch, seq=seq)]
        fg = final_norm_g.reshape(1, d) if i == depth - 1 else None
        xf = _post(mix_list, xf, qkvm, gate, mem_kv, mem_vt, w_out_b, i, fg,
                   dilated=(kind == 0), batch=batch, seq=seq)
    return xf.reshape(batch, seq, d)
```

```python
import functools
import math

import numpy as np
import jax
import jax.numpy as jnp
from jax import lax
from jax.experimental import pallas as pl
from jax.experimental.pallas import tpu as pltpu

D_MODEL = 1024
DEPTH = 4
N_MIXERS = 3
HEAD_DIM = 64
MIX_WIDTH = 768
N_MEM_HEADS = 4
MEM_WIDTH = N_MEM_HEADS * HEAD_DIM
N_MEM = 256
BRANCH = MIX_WIDTH + MEM_WIDTH
IN_WIDTH = 3 * MIX_WIDTH + MEM_WIDTH + BRANCH
QKVM_WIDTH = 3 * MIX_WIDTH + MEM_WIDTH
RMS_EPS = 1e-6
SUBLN_EPS = 1e-5
NEG_INF = -1e30
LOG2E = math.log2(math.e)

DSW_GROUPS = ((128, 1), (512, 4), (2048, 16))
DSW_HEADS = 12
DSW_GROUP_WIDTH = 256
BAND_BLOCK = 128

MOBA_HEADS = 12
MOBA_BLOCK = 256
MOBA_TOPK = 3

DIFF_HEADS = 6
DIFF_D = 64

LANES = 128
QUERY_SUB = 256
MOBA_PREP_ROWS = 2048
DILATED_BLOCK_ROWS = 2048
MOBA_PAIRS_PER_CALL = 3
DIFF_HEADS_PER_CALL = 3
VMEM_LIMIT_BYTES = 56 * 1024 * 1024

BF16 = jnp.bfloat16
F32 = jnp.float32


def _alibi_slopes(n):
    return np.asarray(2.0 ** (-8.0 * np.arange(1, n + 1) / n), dtype=np.float32)


def _log2_slopes(n):
    return np.asarray(_alibi_slopes(n).astype(np.float64) * LOG2E, dtype=np.float32)


def _dot(a, b):
    return jnp.dot(a, b, preferred_element_type=F32)


def _dot_nt(a, b):
    return lax.dot_general(a, b, (((1,), (1,)), ((), ())), preferred_element_type=F32)


def _dot_tn(a, b):
    return lax.dot_general(a, b, (((0,), (0,)), ((), ())), preferred_element_type=F32)


def _compiler_params(n_axes):
    return pltpu.CompilerParams(dimension_semantics=("parallel",) * n_axes,
                                vmem_limit_bytes=VMEM_LIMIT_BYTES)


def _rms_proj_kernel(x_ref, g_ref, w_ref, *out_refs, splits, n_chunk, scales, vt_cols, kmean_cols):
    x = x_ref[...].astype(F32)
    y = x * lax.rsqrt(jnp.mean(x * x, axis=-1, keepdims=True) + RMS_EPS)
    h = (y * g_ref[...]).astype(BF16)
    tm = x.shape[0]
    for (lo, hi), o_ref in zip(splits, out_refs):
        for c in range(lo, hi, n_chunk):
            acc = _dot(h, w_ref[:, c:c + n_chunk].astype(BF16))
            for s_lo, s_hi, factor in scales:
                if s_lo <= c < s_hi:
                    acc = acc * factor
            o_ref[:, c - lo:c - lo + n_chunk] = acc.astype(o_ref.dtype)
            if vt_cols is not None and vt_cols[0] <= c < vt_cols[1]:
                vt_ref = out_refs[len(splits)]
                vt_ref[c - vt_cols[0]:c - vt_cols[0] + n_chunk, :] = acc.T.astype(BF16)
            if kmean_cols is not None and kmean_cols[0] <= c < kmean_cols[1]:
                km_ref = out_refs[-1]
                for r in range(tm // MOBA_BLOCK):
                    blk = acc[r * MOBA_BLOCK:(r + 1) * MOBA_BLOCK]
                    km_ref[r, :, c - kmean_cols[0]:c - kmean_cols[0] + n_chunk] = (
                        jnp.mean(blk, axis=0, keepdims=True))


def _rms_proj(x, g, w, layer, splits, dtypes, *, tm, n_chunk=256, scales=(), vt_cols=None,
              kmean_cols=None):
    T, D = x.shape
    N = w.shape[2]
    out_shape = [jax.ShapeDtypeStruct((T, hi - lo), dt) for (lo, hi), dt in zip(splits, dtypes)]
    out_specs = [pl.BlockSpec((tm, hi - lo), lambda i: (i, 0)) for (lo, hi) in splits]
    if vt_cols is not None:
        vw = vt_cols[1] - vt_cols[0]
        out_shape.append(jax.ShapeDtypeStruct((vw, T), BF16))
        out_specs.append(pl.BlockSpec((vw, tm), lambda i: (0, i)))
    if kmean_cols is not None:
        kw = kmean_cols[1] - kmean_cols[0]
        out_shape.append(jax.ShapeDtypeStruct((T // MOBA_BLOCK, 1, kw), F32))
        out_specs.append(pl.BlockSpec((tm // MOBA_BLOCK, 1, kw), lambda i: (i, 0, 0)))
    kern = functools.partial(_rms_proj_kernel, splits=tuple(splits), n_chunk=n_chunk,
                             scales=tuple(scales), vt_cols=vt_cols, kmean_cols=kmean_cols)
    return pl.pallas_call(
        kern,
        grid=(T // tm,),
        in_specs=[pl.BlockSpec((tm, D), lambda i: (i, 0)),
                  pl.BlockSpec((None, 1, D), lambda i: (layer, 0, 0)),
                  pl.BlockSpec((None, D, N), lambda i: (layer, 0, 0))],
        out_specs=out_specs,
        out_shape=out_shape,
        compiler_params=_compiler_params(1),
        name="rms_proj",
    )(x, g, w)


def _dilated_kernel(q_ref, kc_ref, kp_ref, vc_ref, vp_ref, o_ref, lse_ref, *, tu, dil, slopes):
    ui = pl.program_id(1)
    pair = pl.program_id(2)
    bb = BAND_BLOCK
    key = lax.broadcasted_iota(jnp.int32, (bb, bb), 0)
    qry = lax.broadcasted_iota(jnp.int32, (bb, bb), 1)
    d_cur = (qry - key).astype(F32)
    d_prev = d_cur + float(bb)
    valid_cur = qry >= key
    valid_prev_static = key >= qry
    lane = lax.broadcasted_iota(jnp.int32, (bb, LANES), 1)
    chan = lax.broadcasted_iota(jnp.int32, (LANES, bb), 0)
    head_slopes = [jnp.where(pair == 0, slopes[hh], slopes[2 + hh]) for hh in range(2)]

    def rows_of(blk, r):
        return pl.ds(blk * bb * dil + r, bb, stride=dil)

    def steps(ref, blk, r):
        return ref[rows_of(blk, r), :].astype(BF16)

    work = []
    for r in range(dil):
        for s in range(tu // bb):
            qs = steps(q_ref, s, r)
            v_cur = steps(vc_ref, s, r)
            if s == 0:
                k_prev, v_prev = steps(kp_ref, 0, r), steps(vp_ref, 0, r)
                valid_prev = jnp.logical_and(valid_prev_static, ui > 0)
            else:
                k_prev, v_prev = steps(kc_ref, s - 1, r), steps(vc_ref, s - 1, r)
                valid_prev = valid_prev_static
            k_cur = steps(kc_ref, s, r)
            scores = []
            for hh in range(2):
                own = (lane >= hh * HEAD_DIM) & (lane < (hh + 1) * HEAD_DIM)
                qh = jnp.where(own, qs, jnp.zeros_like(qs))
                scores.append((_dot_nt(k_cur, qh), _dot_nt(k_prev, qh)))
            work.append((r, s, v_cur, v_prev, valid_prev, scores))

    for r, s, v_cur, v_prev, valid_prev, scores in work:
        o_h, lse_h = [], []
        for hh in range(2):
            s_cur = jnp.where(valid_cur, scores[hh][0] - head_slopes[hh] * d_cur, NEG_INF)
            s_prev = jnp.where(valid_prev, scores[hh][1] - head_slopes[hh] * d_prev, NEG_INF)
            m = jnp.maximum(jnp.max(s_cur, axis=0, keepdims=True),
                            jnp.max(s_prev, axis=0, keepdims=True))
            p_cur = jnp.exp(s_cur - m)
            p_prev = jnp.exp(s_prev - m)
            l = jnp.sum(p_cur, axis=0, keepdims=True) + jnp.sum(p_prev, axis=0, keepdims=True)
            acc = _dot_tn(v_cur, p_cur.astype(BF16)) + _dot_tn(v_prev, p_prev.astype(BF16))
            o_h.append(acc / l)
            lse_h.append(jnp.broadcast_to(m + jnp.log(l), (LANES, bb)))
        first = chan < HEAD_DIM
        o_ref[rows_of(s, r), :] = jnp.where(first, o_h[0], o_h[1]).T
        lse_ref[rows_of(s, r), :] = jnp.where(first, lse_h[0], lse_h[1]).T


def _dilated_group(qkvm, g, *, batch, seq):
    window, dil = DSW_GROUPS[g]
    assert window // dil == BAND_BLOCK
    U = seq // dil
    tu = min(U, DILATED_BLOCK_ROWS // dil)
    gw = DSW_GROUP_WIDTH
    tiles = gw // LANES
    kcol = MIX_WIDTH // LANES
    x = qkvm.reshape(batch, seq, QKVM_WIDTH)
    sub = tu // BAND_BLOCK
    slopes = tuple(float(v) for v in _alibi_slopes(DSW_HEADS)[g * 4:(g + 1) * 4] * np.float32(dil))

    def cur(off):
        return pl.BlockSpec((None, tu * dil, LANES), lambda b, u, p: (b, u, off + g * tiles + p))

    def prev(off):
        return pl.BlockSpec((None, BAND_BLOCK * dil, LANES),
                            lambda b, u, p: (b, jnp.maximum(u * sub - 1, 0), off + g * tiles + p))

    out_spec = pl.BlockSpec((None, tu * dil, LANES), lambda b, u, p: (b, u, p))
    o, lse = pl.pallas_call(
        functools.partial(_dilated_kernel, tu=tu, dil=dil, slopes=slopes),
        grid=(batch, U // tu, tiles),
        in_specs=[cur(0), cur(kcol), prev(kcol), cur(2 * kcol), prev(2 * kcol)],
        out_specs=[out_spec, out_spec],
        out_shape=[jax.ShapeDtypeStruct((batch, seq, gw), F32)] * 2,
        compiler_params=_compiler_params(3),
        name=f"dilated_g{g}",
    )(x, x, x, x, x)
    return o.reshape(batch * seq, gw), lse.reshape(batch * seq, gw)


_SEL_LANES = 32
_ALIBI_PIECES = 3


def _bf16_pieces(v, n):
    out, rest = [], np.float32(v)
    for _ in range(n):
        piece = np.asarray(rest, np.float32).astype(jnp.bfloat16).astype(np.float32)
        out.append(float(piece))
        rest = np.float32(rest - piece)
    assert rest == 0.0, "slope does not split exactly into bf16 pieces"
    return out


def _moba_prep_kernel(const_ref, q_ref, k_ref, km_ref, qa_ref, ka_ref, *, nblk, blocks_per_step):
    tq = MOBA_BLOCK
    nsb = q_ref.shape[0] // tq
    km = km_ref[...]
    lane = lax.broadcasted_iota(jnp.int32, (tq, LANES), 1)
    rowi = lax.broadcasted_iota(jnp.int32, (tq, LANES), 0)
    km_lane = lax.broadcasted_iota(jnp.int32, (nblk, LANES), 1)
    km_row = lax.broadcasted_iota(jnp.int32, (nblk, LANES), 0)
    cand_blk = lax.broadcasted_iota(jnp.int32, (nblk, tq), 0)
    cand_blk_f = cand_blk.astype(F32)
    chains = [(sb, hh) for sb in range(nsb) for hh in range(2)]
    rows = [slice(sb * tq, (sb + 1) * tq) for sb in range(nsb)]
    own_blk = [pl.program_id(2) * nsb + sb for sb in range(nsb)]
    own = [(lane >= hh * HEAD_DIM) & (lane < (hh + 1) * HEAD_DIM) for hh in range(2)]
    off = [(1 - hh) * HEAD_DIM for hh in range(2)]
    km_h = [jnp.where((km_lane >= hh * HEAD_DIM) & (km_lane < (hh + 1) * HEAD_DIM), km, 0.0)
            .astype(BF16) for hh in range(2)]

    gates = []
    for sb, hh in chains:
        q2 = q_ref[rows[sb], :]
        gates.append(_dot_nt(km_h[hh], jnp.where(own[hh], q2, jnp.zeros_like(q2))))

    picked = []
    for c, (sb, hh) in enumerate(chains):
        g = jnp.where(cand_blk < own_blk[sb], gates[c], -jnp.inf)
        sel = jnp.zeros((nblk, tq), jnp.bool_)
        for _ in range(MOBA_TOPK):
            mx = jnp.max(g, axis=0, keepdims=True)
            hit = (g == mx) & (mx > -jnp.inf)
            idx = jnp.min(jnp.where(hit, cand_blk_f, float(nblk)), axis=0, keepdims=True)
            pick = cand_blk_f == idx
            sel = sel | pick
            g = jnp.where(pick, -jnp.inf, g)
        picked.append(jnp.where(sel, 1.0, 0.0).astype(BF16))

    place = [jnp.where(km_lane == km_row + off[hh], 1.0, 0.0).astype(BF16) for hh in range(2)]
    placed = [_dot_tn(picked[c], place[hh]) for c, (sb, hh) in enumerate(chains)]

    for c, (sb, hh) in enumerate(chains):
        n = own_blk[sb]
        blk = lane - off[hh]
        is_sel_lane = (blk >= 0) & (blk < _SEL_LANES)
        sel_val = jnp.where((placed[c] > 0.5) | (blk == n), 0.0, NEG_INF)
        q_extra = jnp.where(is_sel_lane, sel_val, const_ref[hh:hh + 1, :])
        qa = jnp.where(own[hh], q_ref[rows[sb], :].astype(F32), q_extra)
        a0 = off[hh] + _SEL_LANES
        is_fine = (lane >= a0) & (lane < a0 + _ALIBI_PIECES)
        is_coarse = (lane >= a0 + _ALIBI_PIECES) & (lane < a0 + 2 * _ALIBI_PIECES)
        coarse = (n % blocks_per_step).astype(F32)
        k_extra = jnp.where(blk == n, 1.0,
                            jnp.where(is_fine, rowi.astype(F32), jnp.where(is_coarse, coarse, 0.0)))
        ka = jnp.where(own[hh], k_ref[rows[sb], :].astype(F32), k_extra)
        qa_ref[rows[sb], hh * LANES:(hh + 1) * LANES] = qa.astype(BF16)
        ka_ref[rows[sb], hh * LANES:(hh + 1) * LANES] = ka.astype(BF16)


SUM_ROWS = 16


def _softmax_step(st, vt1, carry, shift):
    m, acc = carry
    m_new = jnp.maximum(m, jnp.max(st, axis=0, keepdims=True) + shift)
    p = jnp.exp2((st - (m_new - shift)).astype(BF16))
    acc = jnp.exp2(m - m_new) * acc + _dot(vt1, p)
    return m_new, acc


STALE_MAX_CAP = 64.0


def _stale_max_step(st, vt1, carry, shift):
    m, acc, bad = carry
    p = jnp.exp2((st - (m - shift)).astype(BF16))
    cmax = jnp.max(st, axis=0, keepdims=True) + shift
    m_new = jnp.maximum(m, cmax)
    bad = jnp.maximum(bad, jnp.where(cmax - m > STALE_MAX_CAP, 1.0, 0.0))
    acc = (acc + _dot(vt1, p)) * jnp.exp2(m - m_new)
    return m_new, acc, bad


SCORE_LOOKAHEAD = {_softmax_step: 6, _stale_max_step: 3}


def _key_step(step_fn, score_fn, vts, carry, shifts, masks=None):
    n_chain = len(carry)
    lookahead = SCORE_LOOKAHEAD[step_fn]
    sts, new = {}, [None] * n_chain
    for t in range(n_chain + lookahead):
        if t < n_chain:
            sts[t] = score_fn(t)
        c = t - lookahead
        if c >= 0:
            st = sts.pop(c)
            if masks is not None:
                st = jnp.where(masks[c], st, NEG_INF)
            new[c] = step_fn(st, vts[c], carry[c], shifts[c])
    return tuple(new)


def _causal_key_steps(scores_fn, vt_fn, step_slopes, i, step):
    nsub = step // QUERY_SUB
    n_chain = len(step_slopes)

    def rows_of(j):
        return pl.ds(pl.multiple_of(j * step, step), step)

    def shifts_of(j):
        dj = (j - i).astype(F32)
        return [s * dj for s in step_slopes]

    key = lax.broadcasted_iota(jnp.int32, (step, QUERY_SUB), 0)
    qry = lax.broadcasted_iota(jnp.int32, (step, QUERY_SUB), 1)
    masks = [qry + (c % nsub) * QUERY_SUB >= key for c in range(n_chain)]
    init = tuple((jnp.full((1, QUERY_SUB), NEG_INF, F32),
                  jnp.zeros((LANES + SUM_ROWS, QUERY_SUB), F32)) for _ in range(n_chain))
    own = _key_step(_softmax_step, scores_fn(rows_of(i)), vt_fn(rows_of(i)), init,
                    [0.0] * n_chain, masks)

    def fast_body(j, carry):
        return _key_step(_stale_max_step, scores_fn(rows_of(j)), vt_fn(rows_of(j)), carry,
                         shifts_of(j))

    fast = lax.fori_loop(0, i, fast_body,
                         tuple((m, acc, jnp.zeros((1, QUERY_SUB), F32)) for m, acc in own))
    overflowed = functools.reduce(jnp.maximum, [jnp.max(bad) for _, _, bad in fast]) > 0.0

    def exact_body(j, carry):
        return _key_step(_softmax_step, scores_fn(rows_of(j)), vt_fn(rows_of(j)), carry,
                         shifts_of(j))

    accs = lax.cond(overflowed,
                    lambda: tuple(acc for _, acc in lax.fori_loop(0, i, exact_body, own)),
                    lambda: tuple(acc for _, acc, _ in fast))
    return [(acc[LANES:LANES + 1], acc[:LANES]) for acc in accs]


def _with_sum_rows(vt):
    return jnp.concatenate([vt, jnp.ones((SUM_ROWS, vt.shape[1]), vt.dtype)], axis=0)


def _moba_kernel(slope_ref, qa_ref, ka_ref, vt_ref, o_ref, *, step):
    i = pl.program_id(2)
    nsub = step // QUERY_SUB
    n_heads = qa_ref.shape[1] // LANES
    chains = [(h, qs) for h in range(n_heads) for qs in range(nsub)]
    qa = [qa_ref[qs * QUERY_SUB:(qs + 1) * QUERY_SUB, h * LANES:(h + 1) * LANES]
          for h, qs in chains]
    step_slopes = [slope_ref[n_heads * pl.program_id(1) + h] * float(step) for h, _ in chains]

    def scores_fn(rows):
        return lambda c: _dot_nt(ka_ref[rows, chains[c][0] * LANES:(chains[c][0] + 1) * LANES], qa[c])

    def vt_fn(rows):
        per_pair = [_with_sum_rows(vt_ref[pp * LANES:(pp + 1) * LANES, rows])
                    for pp in range(n_heads // 2)]
        return tuple(per_pair[h // 2] for h, _ in chains)

    res = _causal_key_steps(scores_fn, vt_fn, step_slopes, i, step)
    chan = lax.broadcasted_iota(jnp.int32, (LANES, QUERY_SUB), 0)
    for pp in range(n_heads // 2):
        for qs in range(nsub):
            (l0, acc0), (l1, acc1) = res[2 * pp * nsub + qs], res[(2 * pp + 1) * nsub + qs]
            o_ref[qs * QUERY_SUB:(qs + 1) * QUERY_SUB, pp * LANES:(pp + 1) * LANES] = (
                jnp.where(chan < HEAD_DIM, acc0 / l0, acc1 / l1).T.astype(o_ref.dtype))


def _moba_mixer(qkvm, vt, kmean, *, batch, seq, step=512):
    nblk = seq // MOBA_BLOCK
    assert nblk <= _SEL_LANES and step % MOBA_BLOCK == 0 and seq % step == 0
    blocks_per_step = step // MOBA_BLOCK
    npair = MOBA_HEADS // 2
    x = qkvm.reshape(batch, seq, QKVM_WIDTH)
    km = kmean.reshape(batch, nblk, MIX_WIDTH)
    slopes = _log2_slopes(MOBA_HEADS)
    q_const = np.zeros((npair, 8, LANES), np.float32)
    for p in range(npair):
        for hh in range(2):
            a0 = (1 - hh) * HEAD_DIM + _SEL_LANES
            pieces = _bf16_pieces(slopes[2 * p + hh], _ALIBI_PIECES)
            q_const[p, hh, a0:a0 + _ALIBI_PIECES] = pieces
            q_const[p, hh, a0 + _ALIBI_PIECES:a0 + 2 * _ALIBI_PIECES] = [
                v * MOBA_BLOCK for v in pieces]
    tq = min(seq, MOBA_PREP_ROWS)
    kcol = MIX_WIDTH // LANES
    qa, ka = pl.pallas_call(
        functools.partial(_moba_prep_kernel, nblk=nblk, blocks_per_step=blocks_per_step),
        grid=(batch, npair, seq // tq),
        in_specs=[pl.BlockSpec((None, 8, LANES), lambda b, p, i: (p, 0, 0)),
                  pl.BlockSpec((None, tq, LANES), lambda b, p, i: (b, i, p)),
                  pl.BlockSpec((None, tq, LANES), lambda b, p, i: (b, i, kcol + p)),
                  pl.BlockSpec((None, nblk, LANES), lambda b, p, i: (b, 0, p))],
        out_specs=[pl.BlockSpec((None, tq, 2 * LANES), lambda b, p, i: (b, i, p))] * 2,
        out_shape=[jax.ShapeDtypeStruct((batch, seq, MOBA_HEADS * LANES), BF16)] * 2,
        compiler_params=_compiler_params(3),
        name="moba_prep",
    )(jnp.asarray(q_const), x, x, km)
    ppc = MOBA_PAIRS_PER_CALL
    out = pl.pallas_call(
        functools.partial(_moba_kernel, step=step),
        grid=(batch, npair // ppc, seq // step),
        in_specs=[pl.BlockSpec(memory_space=pltpu.SMEM),
                  pl.BlockSpec((None, step, 2 * ppc * LANES), lambda b, p, i: (b, i, p)),
                  pl.BlockSpec((None, seq, 2 * ppc * LANES), lambda b, p, i: (b, 0, p)),
                  pl.BlockSpec((ppc * LANES, seq), lambda b, p, i: (p, b))],
        out_specs=pl.BlockSpec((None, step, ppc * LANES), lambda b, p, i: (b, i, p)),
        out_shape=jax.ShapeDtypeStruct((batch, seq, MIX_WIDTH), BF16),
        compiler_params=_compiler_params(3),
        name="moba_attn",
    )(jnp.asarray(slopes), qa, ka, vt)
    return out.reshape(batch * seq, MIX_WIDTH)


def _extras_lane(mi):
    return (1 - mi) * DIFF_D


def _position_lanes(step):
    tab = np.zeros((step, 2 * LANES), np.float32)
    r = np.arange(step)
    for mi in range(2):
        a0 = mi * LANES + _extras_lane(mi)
        tab[:, a0:a0 + _ALIBI_PIECES] = (r % MOBA_BLOCK)[:, None]
        tab[:, a0 + _ALIBI_PIECES:a0 + 2 * _ALIBI_PIECES] = (r // MOBA_BLOCK)[:, None]
    return jnp.asarray(tab, dtype=BF16)


def _diff_kernel(slope_ref, qx_ref, kx_ref, lam_ref, g_ref, q_ref, k_ref, vt_ref, o_ref, *,
                 step, lambda_init):
    i = pl.program_id(2)
    nsub = step // QUERY_SUB
    n_heads = q_ref.shape[1] // LANES
    lane_q = lax.broadcasted_iota(jnp.int32, (QUERY_SUB, LANES), 1)
    lane_k = lax.broadcasted_iota(jnp.int32, (step, LANES), 1)
    chains = [(h, mi, qs) for h in range(n_heads) for mi in range(2) for qs in range(nsub)]
    qa = []
    for h, mi, qs in chains:
        q2 = q_ref[qs * QUERY_SUB:(qs + 1) * QUERY_SUB, h * LANES:(h + 1) * LANES]
        qx = jnp.broadcast_to(qx_ref[h, mi:mi + 1, :], (QUERY_SUB, LANES)).astype(BF16)
        own_map = (lane_q >= mi * DIFF_D) & (lane_q < (mi + 1) * DIFF_D)
        qa.append(jnp.where(own_map, q2, qx))
    kx = [kx_ref[:, mi * LANES:(mi + 1) * LANES] for mi in range(2)]
    own_k = [(lane_k >= mi * DIFF_D) & (lane_k < (mi + 1) * DIFF_D) for mi in range(2)]
    step_slopes = [slope_ref[n_heads * pl.program_id(1) + h] * float(step) for h, _, _ in chains]

    def scores_fn(rows):
        ka = {}
        for h in range(n_heads):
            kb = k_ref[rows, h * LANES:(h + 1) * LANES]
            for mi in range(2):
                ka[h, mi] = jnp.where(own_k[mi], kb, kx[mi])
        return lambda c: _dot_nt(ka[chains[c][0], chains[c][1]], qa[c])

    def vt_fn(rows):
        per_head = [_with_sum_rows(vt_ref[h * LANES:(h + 1) * LANES, rows]) for h in range(n_heads)]
        return tuple(per_head[h] for h, _, _ in chains)

    res = _causal_key_steps(scores_fn, vt_fn, step_slopes, i, step)

    lq1, lk1, lq2, lk2 = (lam_ref[r:r + 1, :] for r in range(4))
    lam = (jnp.exp(jnp.sum(lq1 * lk1, axis=1, keepdims=True))
           - jnp.exp(jnp.sum(lq2 * lk2, axis=1, keepdims=True)) + lambda_init)
    for h in range(n_heads):
        for qs in range(nsub):
            (l1, acc1), (l2, acc2) = res[(2 * h) * nsub + qs], res[(2 * h + 1) * nsub + qs]
            o = acc1 / l1 - lam * (acc2 / l2)
            y = o * lax.rsqrt(jnp.mean(o * o, axis=0, keepdims=True) + SUBLN_EPS)
            o_ref[qs * QUERY_SUB:(qs + 1) * QUERY_SUB, h * LANES:(h + 1) * LANES] = (
                (y * g_ref[...]) * (1.0 - lambda_init)).T.astype(o_ref.dtype)


def _diff_mixer(qkvm, vt, lam_vecs, subln_g, lambda_init, *, batch, seq, step=512):
    assert step % MOBA_BLOCK == 0 and seq % step == 0
    x = qkvm.reshape(batch, seq, QKVM_WIDTH)
    slopes = _log2_slopes(DIFF_HEADS)
    q_extra = np.zeros((DIFF_HEADS, 8, LANES), np.float32)
    for h in range(DIFF_HEADS):
        pieces = _bf16_pieces(slopes[h], _ALIBI_PIECES)
        for mi in range(2):
            a0 = _extras_lane(mi)
            q_extra[h, mi, a0:a0 + _ALIBI_PIECES] = pieces
            q_extra[h, mi, a0 + _ALIBI_PIECES:a0 + 2 * _ALIBI_PIECES] = [
                v * MOBA_BLOCK for v in pieces]
    kcol = MIX_WIDTH // LANES
    hpc = DIFF_HEADS_PER_CALL
    assert DIFF_HEADS % hpc == 0 and kcol % hpc == 0
    out = pl.pallas_call(
        functools.partial(_diff_kernel, step=step, lambda_init=float(lambda_init)),
        grid=(batch, DIFF_HEADS // hpc, seq // step),
        in_specs=[pl.BlockSpec(memory_space=pltpu.SMEM),
                  pl.BlockSpec((hpc, 8, LANES), lambda b, h, i: (h, 0, 0)),
                  pl.BlockSpec((step, 2 * LANES), lambda b, h, i: (0, 0)),
                  pl.BlockSpec((4, DIFF_D), lambda b, h, i: (0, 0)),
                  pl.BlockSpec((2 * DIFF_D, 1), lambda b, h, i: (0, 0)),
                  pl.BlockSpec((None, step, hpc * LANES), lambda b, h, i: (b, i, h)),
                  pl.BlockSpec((None, seq, hpc * LANES), lambda b, h, i: (b, 0, kcol // hpc + h)),
                  pl.BlockSpec((hpc * LANES, seq), lambda b, h, i: (h, b))],
        out_specs=pl.BlockSpec((None, step, hpc * LANES), lambda b, h, i: (b, i, h)),
        out_shape=jax.ShapeDtypeStruct((batch, seq, MIX_WIDTH), BF16),
        compiler_params=_compiler_params(3),
        name="diff_attn",
    )(jnp.asarray(slopes), jnp.asarray(q_extra), _position_lanes(step), lam_vecs,
      subln_g.reshape(2 * DIFF_D, 1), x, x, vt)
    return out.reshape(batch * seq, MIX_WIDTH)


def _post_kernel(*refs, n_mix, dilated, final):
    mix_refs = refs[:n_mix]
    x_ref, qm_ref, gate_ref, kv_ref, vmt_ref, w_ref = refs[n_mix:n_mix + 6]
    rest = refs[n_mix + 6:]
    fg_ref = rest[0] if final else None
    o_ref = rest[-1]
    tm = x_ref.shape[0]

    if dilated:
        outs = [r[...] for r in mix_refs[:3]]
        lses = [r[...] for r in mix_refs[3:]]
        mx = jnp.maximum(jnp.maximum(lses[0], lses[1]), lses[2])
        es = [jnp.exp(v - mx) for v in lses]
        den = es[0] + es[1] + es[2]
        mix_parts = [o * (e / den) for o, e in zip(outs, es)]
    else:
        mix = mix_refs[0][...].astype(F32)
        mix_parts = [mix[:, g * DSW_GROUP_WIDTH:(g + 1) * DSW_GROUP_WIDTH] for g in range(3)]

    lane = lax.broadcasted_iota(jnp.int32, (tm, LANES), 1)
    chan = lax.broadcasted_iota(jnp.int32, (LANES, tm), 0)
    n_pair = MEM_WIDTH // LANES
    scores = []
    for pair in range(n_pair):
        cs = slice(pair * LANES, (pair + 1) * LANES)
        q2 = qm_ref[:, cs].astype(BF16)
        for hh in range(2):
            own = (lane >= hh * HEAD_DIM) & (lane < (hh + 1) * HEAD_DIM)
            scores.append(_dot_nt(kv_ref[:, cs], jnp.where(own, q2, jnp.zeros_like(q2))))
    mem_parts = []
    for pair in range(n_pair):
        vt1 = _with_sum_rows(vmt_ref[pair * LANES:(pair + 1) * LANES, :])
        o_h = []
        for hh in range(2):
            st = scores[2 * pair + hh]
            p = jnp.exp2((st - jnp.max(st, axis=0, keepdims=True)).astype(BF16))
            acc = _dot(vt1, p)
            o_h.append(acc[:LANES] / acc[LANES:LANES + 1])
        mem_parts.append(jnp.where(chan < HEAD_DIM, o_h[0], o_h[1]).T)

    parts = [(g * DSW_GROUP_WIDTH, mix_parts[g]) for g in range(3)]
    parts += [(MIX_WIDTH + p * LANES, mem_parts[p]) for p in range(len(mem_parts))]
    ys = []
    for c0, val in parts:
        gt = gate_ref[:, c0:c0 + val.shape[1]].astype(F32)
        ys.append((val * (gt * (1.0 / (1.0 + jnp.exp(-gt))))).astype(BF16))
    acc = x_ref[...] + _dot(jnp.concatenate(ys, axis=1), w_ref[...].astype(BF16))
    if final:
        yn = acc * lax.rsqrt(jnp.mean(acc * acc, axis=-1, keepdims=True) + RMS_EPS)
        acc = yn * fg_ref[...]
    o_ref[...] = acc


def _post(mix_list, x, qkvm, gate, mem_kv, mem_vt, w_out, layer, final_g, *, dilated, batch, seq,
          tm=1024):
    T = x.shape[0]
    tiles_per_batch = seq // tm
    n_mix = len(mix_list)
    final = final_g is not None
    qm_block = (3 * MIX_WIDTH) // MEM_WIDTH
    in_specs = [pl.BlockSpec((tm, m.shape[1]), lambda i: (i, 0)) for m in mix_list]
    in_specs += [pl.BlockSpec((tm, D_MODEL), lambda i: (i, 0)),
                 pl.BlockSpec((tm, MEM_WIDTH), lambda i: (i, qm_block)),
                 pl.BlockSpec((tm, BRANCH), lambda i: (i, 0)),
                 pl.BlockSpec((None, N_MEM, 2 * MEM_WIDTH), lambda i: (i // tiles_per_batch, 0, 0)),
                 pl.BlockSpec((MEM_WIDTH, N_MEM), lambda i: (0, i // tiles_per_batch)),
                 pl.BlockSpec((None, BRANCH, D_MODEL), lambda i: (layer, 0, 0))]
    args = list(mix_list) + [x, qkvm, gate, mem_kv, mem_vt, w_out]
    if final:
        in_specs.append(pl.BlockSpec((1, D_MODEL), lambda i: (0, 0)))
        args.append(final_g)
    return pl.pallas_call(
        functools.partial(_post_kernel, n_mix=n_mix, dilated=dilated, final=final),
        grid=(T // tm,),
        in_specs=in_specs,
        out_specs=pl.BlockSpec((tm, D_MODEL), lambda i: (i, 0)),
        out_shape=jax.ShapeDtypeStruct((T, D_MODEL), F32),
        compiler_params=_compiler_params(1),
        name="post",
    )(*args)


def kernel(x, mem, norm_g, w_in, w_out, mem_norm_g, w_mem_kv, diff_lambda_q1, diff_lambda_k1,
           diff_lambda_q2, diff_lambda_k2, diff_subln_g, final_norm_g):
    batch, seq, d = x.shape
    depth = w_in.shape[0]
    T = batch * seq
    xf = x.reshape(T, d)
    memf = mem.reshape(batch * mem.shape[1], d)

    norm_g3 = norm_g.reshape(depth, 1, d)
    mem_norm_g3 = mem_norm_g.reshape(depth, 1, d)
    scale = HEAD_DIM ** -0.5
    v_cols = (2 * MIX_WIDTH, 3 * MIX_WIDTH)

    for i in range(depth):
        kind = i % N_MIXERS
        q_scale = scale if kind == 0 else scale * LOG2E
        scales = [(0, MIX_WIDTH, q_scale), (3 * MIX_WIDTH, QKVM_WIDTH, scale * LOG2E)]
        kmean_cols = (MIX_WIDTH, 2 * MIX_WIDTH) if kind == 1 else None
        proj = _rms_proj(xf, norm_g3, w_in, i, [(0, QKVM_WIDTH), (QKVM_WIDTH, IN_WIDTH)],
                         [F32 if kind == 0 else BF16, BF16], tm=512, scales=scales,
                         vt_cols=None if kind == 0 else v_cols, kmean_cols=kmean_cols)
        qkvm, gate = proj[0], proj[1]
        mem_kv, mem_vt = _rms_proj(memf, mem_norm_g3, w_mem_kv, i, [(0, 2 * MEM_WIDTH)], [BF16],
                                   tm=N_MEM,
                                   vt_cols=(MEM_WIDTH, 2 * MEM_WIDTH))
        mem_kv = mem_kv.reshape(batch, N_MEM, 2 * MEM_WIDTH)
        if kind == 0:
            res = [_dilated_group(qkvm, g, batch=batch, seq=seq) for g in range(len(DSW_GROUPS))]
            mix_list = [r[0] for r in res] + [r[1] for r in res]
        elif kind == 1:
            mix_list = [_moba_mixer(qkvm, proj[2], proj[3], batch=batch, seq=seq)]
        else:
            c = i // N_MIXERS
            lambda_init = 0.8 - 0.6 * math.exp(-0.3 * i)
            lam_vecs = jnp.stack([diff_lambda_q1[c], diff_lambda_k1[c],
                                  diff_lambda_q2[c], diff_lambda_k2[c]]).astype(F32)
            mix_list = [_diff_mixer(qkvm, proj[2], lam_vecs, diff_subln_g[c], lambda_init,
                                    batch=batch, seq=seq)]
        fg = final_norm_g.reshape(1, d) if i == depth - 1 else None
        xf = _post(mix_list, xf, qkvm, gate, mem_kv, mem_vt, w_out, i, fg,
                   dilated=(kind == 0), batch=batch, seq=seq)
    return xf.reshape(batch, seq, d)
```

```python
import functools
import math

import numpy as np
import jax
import jax.numpy as jnp
from jax import lax
from jax.experimental import pallas as pl
from jax.experimental.pallas import tpu as pltpu

D_MODEL = 1024
DEPTH = 4
N_MIXERS = 3
HEAD_DIM = 64
MIX_WIDTH = 768
N_MEM_HEADS = 4
MEM_WIDTH = N_MEM_HEADS * HEAD_DIM
N_MEM = 256
BRANCH = MIX_WIDTH + MEM_WIDTH
IN_WIDTH = 3 * MIX_WIDTH + MEM_WIDTH + BRANCH
QKVM_WIDTH = 3 * MIX_WIDTH + MEM_WIDTH
RMS_EPS = 1e-6
SUBLN_EPS = 1e-5
NEG_INF = -1e30
LOG2E = math.log2(math.e)

DSW_GROUPS = ((128, 1), (512, 4), (2048, 16))
DSW_HEADS = 12
DSW_GROUP_WIDTH = 256
BAND_BLOCK = 128

MOBA_HEADS = 12
MOBA_BLOCK = 256
MOBA_TOPK = 3

DIFF_HEADS = 6
DIFF_D = 64

LANES = 128
QUERY_SUB = 256
MOBA_PREP_ROWS = 2048
DILATED_BLOCK_ROWS = 2048
MOBA_PAIRS_PER_CALL = 3
DIFF_HEADS_PER_CALL = 3
VMEM_LIMIT_BYTES = 56 * 1024 * 1024

BF16 = jnp.bfloat16
F32 = jnp.float32


def _alibi_slopes(n):
    return np.asarray(2.0 ** (-8.0 * np.arange(1, n + 1) / n), dtype=np.float32)


def _log2_slopes(n):
    return np.asarray(_alibi_slopes(n).astype(np.float64) * LOG2E, dtype=np.float32)


def _dot(a, b):
    return jnp.dot(a, b, preferred_element_type=F32)


def _dot_nt(a, b):
    return lax.dot_general(a, b, (((1,), (1,)), ((), ())), preferred_element_type=F32)


def _dot_tn(a, b):
    return lax.dot_general(a, b, (((0,), (0,)), ((), ())), preferred_element_type=F32)


def _compiler_params(n_axes):
    return pltpu.CompilerParams(dimension_semantics=("parallel",) * n_axes,
                                vmem_limit_bytes=VMEM_LIMIT_BYTES)


def _rms_proj_kernel(x_ref, g_ref, w_ref, *out_refs, splits, n_chunk, scales, vt_cols, kmean_cols):
    x = x_ref[...].astype(F32)
    y = x * lax.rsqrt(jnp.mean(x * x, axis=-1, keepdims=True) + RMS_EPS)
    h = (y * g_ref[...]).astype(BF16)
    tm = x.shape[0]
    for (lo, hi), o_ref in zip(splits, out_refs):
        for c in range(lo, hi, n_chunk):
            acc = _dot(h, w_ref[:, c:c + n_chunk].astype(BF16))
            for s_lo, s_hi, factor in scales:
                if s_lo <= c < s_hi:
                    acc = acc * factor
            o_ref[:, c - lo:c - lo + n_chunk] = acc.astype(o_ref.dtype)
            if vt_cols is not None and vt_cols[0] <= c < vt_cols[1]:
                vt_ref = out_refs[len(splits)]
                vt_ref[c - vt_cols[0]:c - vt_cols[0] + n_chunk, :] = acc.T.astype(BF16)
            if kmean_cols is not None and kmean_cols[0] <= c < kmean_cols[1]:
                km_ref = out_refs[-1]
                for r in range(tm // MOBA_BLOCK):
                    blk = acc[r * MOBA_BLOCK:(r + 1) * MOBA_BLOCK]
                    km_ref[r, :, c - kmean_cols[0]:c - kmean_cols[0] + n_chunk] = (
                        jnp.mean(blk, axis=0, keepdims=True))


def _rms_proj(x, g, w, layer, splits, dtypes, *, tm, n_chunk=256, scales=(), vt_cols=None,
              kmean_cols=None):
    T, D = x.shape
    N = w.shape[2]
    out_shape = [jax.ShapeDtypeStruct((T, hi - lo), dt) for (lo, hi), dt in zip(splits, dtypes)]
    out_specs = [pl.BlockSpec((tm, hi - lo), lambda i: (i, 0)) for (lo, hi) in splits]
    if vt_cols is not None:
        vw = vt_cols[1] - vt_cols[0]
        out_shape.append(jax.ShapeDtypeStruct((vw, T), BF16))
        out_specs.append(pl.BlockSpec((vw, tm), lambda i: (0, i)))
    if kmean_cols is not None:
        kw = kmean_cols[1] - kmean_cols[0]
        out_shape.append(jax.ShapeDtypeStruct((T // MOBA_BLOCK, 1, kw), F32))
        out_specs.append(pl.BlockSpec((tm // MOBA_BLOCK, 1, kw), lambda i: (i, 0, 0)))
    kern = functools.partial(_rms_proj_kernel, splits=tuple(splits), n_chunk=n_chunk,
                             scales=tuple(scales), vt_cols=vt_cols, kmean_cols=kmean_cols)
    return pl.pallas_call(
        kern,
        grid=(T // tm,),
        in_specs=[pl.BlockSpec((tm, D), lambda i: (i, 0)),
                  pl.BlockSpec((None, 1, D), lambda i: (layer, 0, 0)),
                  pl.BlockSpec((None, D, N), lambda i: (layer, 0, 0))],
        out_specs=out_specs,
        out_shape=out_shape,
        compiler_params=_compiler_params(1),
        name="rms_proj",
    )(x, g, w)


def _dilated_kernel(q_ref, kc_ref, kp_ref, vc_ref, vp_ref, o_ref, lse_ref, *, tu, dil, slopes):
    ui = pl.program_id(1)
    pair = pl.program_id(2)
    bb = BAND_BLOCK
    key = lax.broadcasted_iota(jnp.int32, (bb, bb), 0)
    qry = lax.broadcasted_iota(jnp.int32, (bb, bb), 1)
    d_cur = (qry - key).astype(F32)
    d_prev = d_cur + float(bb)
    valid_cur = qry >= key
    valid_prev_static = key >= qry
    lane = lax.broadcasted_iota(jnp.int32, (bb, LANES), 1)
    chan = lax.broadcasted_iota(jnp.int32, (LANES, bb), 0)
    head_slopes = [jnp.where(pair == 0, slopes[hh], slopes[2 + hh]) for hh in range(2)]

    def rows_of(blk, r):
        return pl.ds(blk * bb * dil + r, bb, stride=dil)

    def steps(ref, blk, r):
        return ref[rows_of(blk, r), :].astype(BF16)

    work = []
    for r in range(dil):
        for s in range(tu // bb):
            qs = steps(q_ref, s, r)
            v_cur = steps(vc_ref, s, r)
            if s == 0:
                k_prev, v_prev = steps(kp_ref, 0, r), steps(vp_ref, 0, r)
                valid_prev = jnp.logical_and(valid_prev_static, ui > 0)
            else:
                k_prev, v_prev = steps(kc_ref, s - 1, r), steps(vc_ref, s - 1, r)
                valid_prev = valid_prev_static
            k_cur = steps(kc_ref, s, r)
            scores = []
            for hh in range(2):
                own = (lane >= hh * HEAD_DIM) & (lane < (hh + 1) * HEAD_DIM)
                qh = jnp.where(own, qs, jnp.zeros_like(qs))
                scores.append((_dot_nt(k_cur, qh), _dot_nt(k_prev, qh)))
            work.append((r, s, v_cur, v_prev, valid_prev, scores))

    for r, s, v_cur, v_prev, valid_prev, scores in work:
        o_h, lse_h = [], []
        for hh in range(2):
            s_cur = jnp.where(valid_cur, scores[hh][0] - head_slopes[hh] * d_cur, NEG_INF)
            s_prev = jnp.where(valid_prev, scores[hh][1] - head_slopes[hh] * d_prev, NEG_INF)
            m = jnp.maximum(jnp.max(s_cur, axis=0, keepdims=True),
                            jnp.max(s_prev, axis=0, keepdims=True))
            p_cur = jnp.exp(s_cur - m)
            p_prev = jnp.exp(s_prev - m)
            l = jnp.sum(p_cur, axis=0, keepdims=True) + jnp.sum(p_prev, axis=0, keepdims=True)
            acc = _dot_tn(v_cur, p_cur.astype(BF16)) + _dot_tn(v_prev, p_prev.astype(BF16))
            o_h.append(acc / l)
            lse_h.append(jnp.broadcast_to(m + jnp.log(l), (LANES, bb)))
        first = chan < HEAD_DIM
        o_ref[rows_of(s, r), :] = jnp.where(first, o_h[0], o_h[1]).T
        lse_ref[rows_of(s, r), :] = jnp.where(first, lse_h[0], lse_h[1]).T


def _dilated_group(qkvm, g, *, batch, seq):
    window, dil = DSW_GROUPS[g]
    assert window // dil == BAND_BLOCK
    U = seq // dil
    tu = min(U, DILATED_BLOCK_ROWS // dil)
    gw = DSW_GROUP_WIDTH
    tiles = gw // LANES
    kcol = MIX_WIDTH // LANES
    x = qkvm.reshape(batch, seq, QKVM_WIDTH)
    sub = tu // BAND_BLOCK
    slopes = tuple(float(v) for v in _alibi_slopes(DSW_HEADS)[g * 4:(g + 1) * 4] * np.float32(dil))

    def cur(off):
        return pl.BlockSpec((None, tu * dil, LANES), lambda b, u, p: (b, u, off + g * tiles + p))

    def prev(off):
        return pl.BlockSpec((None, BAND_BLOCK * dil, LANES),
                            lambda b, u, p: (b, jnp.maximum(u * sub - 1, 0), off + g * tiles + p))

    out_spec = pl.BlockSpec((None, tu * dil, LANES), lambda b, u, p: (b, u, p))
    o, lse = pl.pallas_call(
        functools.partial(_dilated_kernel, tu=tu, dil=dil, slopes=slopes),
        grid=(batch, U // tu, tiles),
        in_specs=[cur(0), cur(kcol), prev(kcol), cur(2 * kcol), prev(2 * kcol)],
        out_specs=[out_spec, out_spec],
        out_shape=[jax.ShapeDtypeStruct((batch, seq, gw), F32)] * 2,
        compiler_params=_compiler_params(3),
        name=f"dilated_g{g}",
    )(x, x, x, x, x)
    return o.reshape(batch * seq, gw), lse.reshape(batch * seq, gw)


_SEL_LANES = 32
_ALIBI_PIECES = 3


def _bf16_pieces(v, n):
    out, rest = [], np.float32(v)
    for _ in range(n):
        piece = np.asarray(rest, np.float32).astype(jnp.bfloat16).astype(np.float32)
        out.append(float(piece))
        rest = np.float32(rest - piece)
    assert rest == 0.0, "slope does not split exactly into bf16 pieces"
    return out


def _moba_prep_kernel(const_ref, q_ref, k_ref, km_ref, qa_ref, ka_ref, *, nblk, blocks_per_step):
    tq = MOBA_BLOCK
    nsb = q_ref.shape[0] // tq
    km = km_ref[...]
    lane = lax.broadcasted_iota(jnp.int32, (tq, LANES), 1)
    rowi = lax.broadcasted_iota(jnp.int32, (tq, LANES), 0)
    km_lane = lax.broadcasted_iota(jnp.int32, (nblk, LANES), 1)
    km_row = lax.broadcasted_iota(jnp.int32, (nblk, LANES), 0)
    cand_blk = lax.broadcasted_iota(jnp.int32, (nblk, tq), 0)
    cand_blk_f = cand_blk.astype(F32)
    chains = [(sb, hh) for sb in range(nsb) for hh in range(2)]
    rows = [slice(sb * tq, (sb + 1) * tq) for sb in range(nsb)]
    own_blk = [pl.program_id(2) * nsb + sb for sb in range(nsb)]
    own = [(lane >= hh * HEAD_DIM) & (lane < (hh + 1) * HEAD_DIM) for hh in range(2)]
    off = [(1 - hh) * HEAD_DIM for hh in range(2)]
    km_h = [jnp.where((km_lane >= hh * HEAD_DIM) & (km_lane < (hh + 1) * HEAD_DIM), km, 0.0)
            .astype(BF16) for hh in range(2)]

    gates = []
    for sb, hh in chains:
        q2 = q_ref[rows[sb], :]
        gates.append(_dot_nt(km_h[hh], jnp.where(own[hh], q2, jnp.zeros_like(q2))))

    picked = []
    for c, (sb, hh) in enumerate(chains):
        g = jnp.where(cand_blk < own_blk[sb], gates[c], -jnp.inf)
        sel = jnp.zeros((nblk, tq), jnp.bool_)
        for _ in range(MOBA_TOPK):
            mx = jnp.max(g, axis=0, keepdims=True)
            hit = (g == mx) & (mx > -jnp.inf)
            idx = jnp.min(jnp.where(hit, cand_blk_f, float(nblk)), axis=0, keepdims=True)
            pick = cand_blk_f == idx
            sel = sel | pick
            g = jnp.where(pick, -jnp.inf, g)
        picked.append(jnp.where(sel, 1.0, 0.0).astype(BF16))

    place = [jnp.where(km_lane == km_row + off[hh], 1.0, 0.0).astype(BF16) for hh in range(2)]
    placed = [_dot_tn(picked[c], place[hh]) for c, (sb, hh) in enumerate(chains)]

    for c, (sb, hh) in enumerate(chains):
        n = own_blk[sb]
        blk = lane - off[hh]
        is_sel_lane = (blk >= 0) & (blk < _SEL_LANES)
        sel_val = jnp.where((placed[c] > 0.5) | (blk == n), 0.0, NEG_INF)
        q_extra = jnp.where(is_sel_lane, sel_val, const_ref[hh:hh + 1, :])
        qa = jnp.where(own[hh], q_ref[rows[sb], :].astype(F32), q_extra)
        a0 = off[hh] + _SEL_LANES
        is_fine = (lane >= a0) & (lane < a0 + _ALIBI_PIECES)
        is_coarse = (lane >= a0 + _ALIBI_PIECES) & (lane < a0 + 2 * _ALIBI_PIECES)
        coarse = (n % blocks_per_step).astype(F32)
        k_extra = jnp.where(blk == n, 1.0,
                            jnp.where(is_fine, rowi.astype(F32), jnp.where(is_coarse, coarse, 0.0)))
        ka = jnp.where(own[hh], k_ref[rows[sb], :].astype(F32), k_extra)
        qa_ref[rows[sb], hh * LANES:(hh + 1) * LANES] = qa.astype(BF16)
        ka_ref[rows[sb], hh * LANES:(hh + 1) * LANES] = ka.astype(BF16)


SUM_ROWS = 16


def _softmax_step(st, vt1, carry, shift):
    m, acc = carry
    m_new = jnp.maximum(m, jnp.max(st, axis=0, keepdims=True) + shift)
    p = jnp.exp2((st - (m_new - shift)).astype(BF16))
    acc = jnp.exp2(m - m_new) * acc + _dot(vt1, p)
    return m_new, acc


STALE_MAX_CAP = 64.0


def _stale_max_step(st, vt1, carry, shift):
    m, acc, bad = carry
    p = jnp.exp2((st - (m - shift)).astype(BF16))
    cmax = jnp.max(st, axis=0, keepdims=True) + shift
    m_new = jnp.maximum(m, cmax)
    bad = jnp.maximum(bad, jnp.where(cmax - m > STALE_MAX_CAP, 1.0, 0.0))
    acc = (acc + _dot(vt1, p)) * jnp.exp2(m - m_new)
    return m_new, acc, bad


SCORE_LOOKAHEAD = {_softmax_step: 6, _stale_max_step: 3}


def _key_step(step_fn, score_fn, vts, carry, shifts, masks=None):
    n_chain = len(carry)
    lookahead = SCORE_LOOKAHEAD[step_fn]
    sts, new = {}, [None] * n_chain
    for t in range(n_chain + lookahead):
        if t < n_chain:
            sts[t] = score_fn(t)
        c = t - lookahead
        if c >= 0:
            st = sts.pop(c)
            if masks is not None:
                st = jnp.where(masks[c], st, NEG_INF)
            new[c] = step_fn(st, vts[c], carry[c], shifts[c])
    return tuple(new)


def _causal_key_steps(scores_fn, vt_fn, finish, step_slopes, i, step):
    nsub = step // QUERY_SUB
    n_chain = len(step_slopes)

    def rows_of(j):
        return pl.ds(pl.multiple_of(j * step, step), step)

    def shifts_of(j):
        dj = (j - i).astype(F32)
        return [s * dj for s in step_slopes]

    own_rows = [pl.ds(pl.multiple_of(i * step, step), (qs + 1) * QUERY_SUB) for qs in range(nsub)]
    own_scores = [scores_fn(r) for r in own_rows]
    own_vts = [vt_fn(r) for r in own_rows]
    masks = []
    for qs in range(nsub):
        key = lax.broadcasted_iota(jnp.int32, ((qs + 1) * QUERY_SUB, QUERY_SUB), 0)
        qry = lax.broadcasted_iota(jnp.int32, ((qs + 1) * QUERY_SUB, QUERY_SUB), 1)
        masks.append(qry + qs * QUERY_SUB >= key)
    init = tuple((jnp.full((1, QUERY_SUB), NEG_INF, F32),
                  jnp.zeros((LANES + SUM_ROWS, QUERY_SUB), F32)) for _ in range(n_chain))
    own = _key_step(_softmax_step, lambda c: own_scores[c % nsub](c),
                    [own_vts[c % nsub][c] for c in range(n_chain)], init, [0.0] * n_chain,
                    [masks[c % nsub] for c in range(n_chain)])

    def fast_body(j, carry):
        return _key_step(_stale_max_step, scores_fn(rows_of(j)), vt_fn(rows_of(j)), carry,
                         shifts_of(j))

    fast = lax.fori_loop(0, i, fast_body,
                         tuple((m, acc, jnp.zeros((1, QUERY_SUB), F32)) for m, acc in own))
    overflowed = functools.reduce(jnp.maximum, [jnp.max(bad) for _, _, bad in fast]) > 0.0

    def exact_body(j, carry):
        return _key_step(_softmax_step, scores_fn(rows_of(j)), vt_fn(rows_of(j)), carry,
                         shifts_of(j))

    def split(acc):
        return acc[LANES:LANES + 1], acc[:LANES]

    finish([split(acc) for _, acc, _ in fast])

    @pl.when(overflowed)
    def _():
        finish([split(acc) for _, acc in lax.fori_loop(0, i, exact_body, own)])


def _with_sum_rows(vt):
    return jnp.concatenate([vt, jnp.ones((SUM_ROWS, vt.shape[1]), vt.dtype)], axis=0)


def _moba_kernel(slope_ref, qa_ref, ka_ref, vt_ref, o_ref, *, step):
    i = pl.program_id(2)
    nsub = step // QUERY_SUB
    n_heads = qa_ref.shape[1] // LANES
    chains = [(h, qs) for h in range(n_heads) for qs in range(nsub)]
    qa = [qa_ref[qs * QUERY_SUB:(qs + 1) * QUERY_SUB, h * LANES:(h + 1) * LANES]
          for h, qs in chains]
    step_slopes = [slope_ref[n_heads * pl.program_id(1) + h] * float(step) for h, _ in chains]

    def scores_fn(rows):
        return lambda c: _dot_nt(ka_ref[rows, chains[c][0] * LANES:(chains[c][0] + 1) * LANES], qa[c])

    def vt_fn(rows):
        per_pair = [_with_sum_rows(vt_ref[pp * LANES:(pp + 1) * LANES, rows])
                    for pp in range(n_heads // 2)]
        return tuple(per_pair[h // 2] for h, _ in chains)

    chan = lax.broadcasted_iota(jnp.int32, (LANES, QUERY_SUB), 0)

    def finish(res):
        for pp in range(n_heads // 2):
            for qs in range(nsub):
                (l0, acc0), (l1, acc1) = res[2 * pp * nsub + qs], res[(2 * pp + 1) * nsub + qs]
                o_ref[qs * QUERY_SUB:(qs + 1) * QUERY_SUB, pp * LANES:(pp + 1) * LANES] = (
                    jnp.where(chan < HEAD_DIM, acc0 / l0, acc1 / l1).T.astype(o_ref.dtype))

    _causal_key_steps(scores_fn, vt_fn, finish, step_slopes, i, step)


def _moba_mixer(qkvm, vt, kmean, *, batch, seq, step=512):
    nblk = seq // MOBA_BLOCK
    assert nblk <= _SEL_LANES and step % MOBA_BLOCK == 0 and seq % step == 0
    blocks_per_step = step // MOBA_BLOCK
    npair = MOBA_HEADS // 2
    x = qkvm.reshape(batch, seq, QKVM_WIDTH)
    km = kmean.reshape(batch, nblk, MIX_WIDTH)
    slopes = _log2_slopes(MOBA_HEADS)
    q_const = np.zeros((npair, 8, LANES), np.float32)
    for p in range(npair):
        for hh in range(2):
            a0 = (1 - hh) * HEAD_DIM + _SEL_LANES
            pieces = _bf16_pieces(slopes[2 * p + hh], _ALIBI_PIECES)
            q_const[p, hh, a0:a0 + _ALIBI_PIECES] = pieces
            q_const[p, hh, a0 + _ALIBI_PIECES:a0 + 2 * _ALIBI_PIECES] = [
                v * MOBA_BLOCK for v in pieces]
    tq = min(seq, MOBA_PREP_ROWS)
    kcol = MIX_WIDTH // LANES
    qa, ka = pl.pallas_call(
        functools.partial(_moba_prep_kernel, nblk=nblk, blocks_per_step=blocks_per_step),
        grid=(batch, npair, seq // tq),
        in_specs=[pl.BlockSpec((None, 8, LANES), lambda b, p, i: (p, 0, 0)),
                  pl.BlockSpec((None, tq, LANES), lambda b, p, i: (b, i, p)),
                  pl.BlockSpec((None, tq, LANES), lambda b, p, i: (b, i, kcol + p)),
                  pl.BlockSpec((None, nblk, LANES), lambda b, p, i: (b, 0, p))],
        out_specs=[pl.BlockSpec((None, tq, 2 * LANES), lambda b, p, i: (b, i, p))] * 2,
        out_shape=[jax.ShapeDtypeStruct((batch, seq, MOBA_HEADS * LANES), BF16)] * 2,
        compiler_params=_compiler_params(3),
        name="moba_prep",
    )(jnp.asarray(q_const), x, x, km)
    ppc = MOBA_PAIRS_PER_CALL
    out = pl.pallas_call(
        functools.partial(_moba_kernel, step=step),
        grid=(batch, npair // ppc, seq // step),
        in_specs=[pl.BlockSpec(memory_space=pltpu.SMEM),
                  pl.BlockSpec((None, step, 2 * ppc * LANES), lambda b, p, i: (b, i, p)),
                  pl.BlockSpec((None, seq, 2 * ppc * LANES), lambda b, p, i: (b, 0, p)),
                  pl.BlockSpec((ppc * LANES, seq), lambda b, p, i: (p, b))],
        out_specs=pl.BlockSpec((None, step, ppc * LANES), lambda b, p, i: (b, i, p)),
        out_shape=jax.ShapeDtypeStruct((batch, seq, MIX_WIDTH), BF16),
        compiler_params=_compiler_params(3),
        name="moba_attn",
    )(jnp.asarray(slopes), qa, ka, vt)
    return out.reshape(batch * seq, MIX_WIDTH)


def _extras_lane(mi):
    return (1 - mi) * DIFF_D


def _position_lanes(step):
    tab = np.zeros((step, 2 * LANES), np.float32)
    r = np.arange(step)
    for mi in range(2):
        a0 = mi * LANES + _extras_lane(mi)
        tab[:, a0:a0 + _ALIBI_PIECES] = (r % MOBA_BLOCK)[:, None]
        tab[:, a0 + _ALIBI_PIECES:a0 + 2 * _ALIBI_PIECES] = (r // MOBA_BLOCK)[:, None]
    return jnp.asarray(tab, dtype=BF16)


def _diff_kernel(slope_ref, qx_ref, kx_ref, lam_ref, g_ref, q_ref, k_ref, vt_ref, o_ref, *,
                 step, lambda_init):
    i = pl.program_id(2)
    nsub = step // QUERY_SUB
    n_heads = q_ref.shape[1] // LANES
    lane_q = lax.broadcasted_iota(jnp.int32, (QUERY_SUB, LANES), 1)
    chains = [(h, mi, qs) for h in range(n_heads) for mi in range(2) for qs in range(nsub)]
    qa = []
    for h, mi, qs in chains:
        q2 = q_ref[qs * QUERY_SUB:(qs + 1) * QUERY_SUB, h * LANES:(h + 1) * LANES]
        qx = jnp.broadcast_to(qx_ref[h, mi:mi + 1, :], (QUERY_SUB, LANES)).astype(BF16)
        own_map = (lane_q >= mi * DIFF_D) & (lane_q < (mi + 1) * DIFF_D)
        qa.append(jnp.where(own_map, q2, qx))
    step_slopes = [slope_ref[n_heads * pl.program_id(1) + h] * float(step) for h, _, _ in chains]

    def scores_fn(rows):
        ka, n = {}, rows.size
        lane_k = lax.broadcasted_iota(jnp.int32, (n, LANES), 1)
        for h in range(n_heads):
            kb = k_ref[rows, h * LANES:(h + 1) * LANES]
            for mi in range(2):
                own_k = (lane_k >= mi * DIFF_D) & (lane_k < (mi + 1) * DIFF_D)
                ka[h, mi] = jnp.where(own_k, kb, kx_ref[0:n, mi * LANES:(mi + 1) * LANES])
        return lambda c: _dot_nt(ka[chains[c][0], chains[c][1]], qa[c])

    def vt_fn(rows):
        per_head = [_with_sum_rows(vt_ref[h * LANES:(h + 1) * LANES, rows]) for h in range(n_heads)]
        return tuple(per_head[h] for h, _, _ in chains)

    def finish(res):
        lq1, lk1, lq2, lk2 = (lam_ref[r:r + 1, :] for r in range(4))
        lam = (jnp.exp(jnp.sum(lq1 * lk1, axis=1, keepdims=True))
               - jnp.exp(jnp.sum(lq2 * lk2, axis=1, keepdims=True)) + lambda_init)
        for h in range(n_heads):
            for qs in range(nsub):
                (l1, acc1), (l2, acc2) = res[(2 * h) * nsub + qs], res[(2 * h + 1) * nsub + qs]
                o = acc1 / l1 - lam * (acc2 / l2)
                y = o * lax.rsqrt(jnp.mean(o * o, axis=0, keepdims=True) + SUBLN_EPS)
                o_ref[qs * QUERY_SUB:(qs + 1) * QUERY_SUB, h * LANES:(h + 1) * LANES] = (
                    (y * g_ref[...]) * (1.0 - lambda_init)).T.astype(o_ref.dtype)

    _causal_key_steps(scores_fn, vt_fn, finish, step_slopes, i, step)


def _diff_mixer(qkvm, vt, lam_vecs, subln_g, lambda_init, *, batch, seq, step=512):
    assert step % MOBA_BLOCK == 0 and seq % step == 0
    x = qkvm.reshape(batch, seq, QKVM_WIDTH)
    slopes = _log2_slopes(DIFF_HEADS)
    q_extra = np.zeros((DIFF_HEADS, 8, LANES), np.float32)
    for h in range(DIFF_HEADS):
        pieces = _bf16_pieces(slopes[h], _ALIBI_PIECES)
        for mi in range(2):
            a0 = _extras_lane(mi)
            q_extra[h, mi, a0:a0 + _ALIBI_PIECES] = pieces
            q_extra[h, mi, a0 + _ALIBI_PIECES:a0 + 2 * _ALIBI_PIECES] = [
                v * MOBA_BLOCK for v in pieces]
    kcol = MIX_WIDTH // LANES
    hpc = DIFF_HEADS_PER_CALL
    assert DIFF_HEADS % hpc == 0 and kcol % hpc == 0
    out = pl.pallas_call(
        functools.partial(_diff_kernel, step=step, lambda_init=float(lambda_init)),
        grid=(batch, DIFF_HEADS // hpc, seq // step),
        in_specs=[pl.BlockSpec(memory_space=pltpu.SMEM),
                  pl.BlockSpec((hpc, 8, LANES), lambda b, h, i: (h, 0, 0)),
                  pl.BlockSpec((step, 2 * LANES), lambda b, h, i: (0, 0)),
                  pl.BlockSpec((4, DIFF_D), lambda b, h, i: (0, 0)),
                  pl.BlockSpec((2 * DIFF_D, 1), lambda b, h, i: (0, 0)),
                  pl.BlockSpec((None, step, hpc * LANES), lambda b, h, i: (b, i, h)),
                  pl.BlockSpec((None, seq, hpc * LANES), lambda b, h, i: (b, 0, kcol // hpc + h)),
                  pl.BlockSpec((hpc * LANES, seq), lambda b, h, i: (h, b))],
        out_specs=pl.BlockSpec((None, step, hpc * LANES), lambda b, h, i: (b, i, h)),
        out_shape=jax.ShapeDtypeStruct((batch, seq, MIX_WIDTH), BF16),
        compiler_params=_compiler_params(3),
        name="diff_attn",
    )(jnp.asarray(slopes), jnp.asarray(q_extra), _position_lanes(step), lam_vecs,
      subln_g.reshape(2 * DIFF_D, 1), x, x, vt)
    return out.reshape(batch * seq, MIX_WIDTH)


def _post_kernel(*refs, n_mix, dilated, final):
    mix_refs = refs[:n_mix]
    x_ref, qm_ref, gate_ref, kv_ref, vmt_ref, w_ref = refs[n_mix:n_mix + 6]
    rest = refs[n_mix + 6:]
    fg_ref = rest[0] if final else None
    o_ref = rest[-1]
    tm = x_ref.shape[0]

    if dilated:
        outs = [r[...] for r in mix_refs[:3]]
        lses = [r[...] for r in mix_refs[3:]]
        mx = jnp.maximum(jnp.maximum(lses[0], lses[1]), lses[2])
        es = [jnp.exp(v - mx) for v in lses]
        den = es[0] + es[1] + es[2]
        mix_parts = [o * (e / den) for o, e in zip(outs, es)]
    else:
        mix = mix_refs[0][...].astype(F32)
        mix_parts = [mix[:, g * DSW_GROUP_WIDTH:(g + 1) * DSW_GROUP_WIDTH] for g in range(3)]

    lane = lax.broadcasted_iota(jnp.int32, (tm, LANES), 1)
    chan = lax.broadcasted_iota(jnp.int32, (LANES, tm), 0)
    n_pair = MEM_WIDTH // LANES
    scores = []
    for pair in range(n_pair):
        cs = slice(pair * LANES, (pair + 1) * LANES)
        q2 = qm_ref[:, cs].astype(BF16)
        for hh in range(2):
            own = (lane >= hh * HEAD_DIM) & (lane < (hh + 1) * HEAD_DIM)
            scores.append(_dot_nt(kv_ref[:, cs], jnp.where(own, q2, jnp.zeros_like(q2))))
    mem_parts = []
    for pair in range(n_pair):
        vt1 = _with_sum_rows(vmt_ref[pair * LANES:(pair + 1) * LANES, :])
        o_h = []
        for hh in range(2):
            st = scores[2 * pair + hh]
            p = jnp.exp2((st - jnp.max(st, axis=0, keepdims=True)).astype(BF16))
            acc = _dot(vt1, p)
            o_h.append(acc[:LANES] / acc[LANES:LANES + 1])
        mem_parts.append(jnp.where(chan < HEAD_DIM, o_h[0], o_h[1]).T)

    parts = [(g * DSW_GROUP_WIDTH, mix_parts[g]) for g in range(3)]
    parts += [(MIX_WIDTH + p * LANES, mem_parts[p]) for p in range(len(mem_parts))]
    ys = []
    for c0, val in parts:
        gt = gate_ref[:, c0:c0 + val.shape[1]].astype(F32)
        ys.append((val * (gt * (1.0 / (1.0 + jnp.exp(-gt))))).astype(BF16))
    acc = x_ref[...] + _dot(jnp.concatenate(ys, axis=1), w_ref[...].astype(BF16))
    if final:
        yn = acc * lax.rsqrt(jnp.mean(acc * acc, axis=-1, keepdims=True) + RMS_EPS)
        acc = yn * fg_ref[...]
    o_ref[...] = acc


def _post(mix_list, x, qkvm, gate, mem_kv, mem_vt, w_out, layer, final_g, *, dilated, batch, seq,
          tm=1024):
    T = x.shape[0]
    tiles_per_batch = seq // tm
    n_mix = len(mix_list)
    final = final_g is not None
    qm_block = (3 * MIX_WIDTH) // MEM_WIDTH
    in_specs = [pl.BlockSpec((tm, m.shape[1]), lambda i: (i, 0)) for m in mix_list]
    in_specs += [pl.BlockSpec((tm, D_MODEL), lambda i: (i, 0)),
                 pl.BlockSpec((tm, MEM_WIDTH), lambda i: (i, qm_block)),
                 pl.BlockSpec((tm, BRANCH), lambda i: (i, 0)),
                 pl.BlockSpec((None, N_MEM, 2 * MEM_WIDTH), lambda i: (i // tiles_per_batch, 0, 0)),
                 pl.BlockSpec((MEM_WIDTH, N_MEM), lambda i: (0, i // tiles_per_batch)),
                 pl.BlockSpec((None, BRANCH, D_MODEL), lambda i: (layer, 0, 0))]
    args = list(mix_list) + [x, qkvm, gate, mem_kv, mem_vt, w_out]
    if final:
        in_specs.append(pl.BlockSpec((1, D_MODEL), lambda i: (0, 0)))
        args.append(final_g)
    return pl.pallas_call(
        functools.partial(_post_kernel, n_mix=n_mix, dilated=dilated, final=final),
        grid=(T // tm,),
        in_specs=in_specs,
        out_specs=pl.BlockSpec((tm, D_MODEL), lambda i: (i, 0)),
        out_shape=jax.ShapeDtypeStruct((T, D_MODEL), F32),
        compiler_params=_compiler_params(1),
        name="post",
    )(*args)


def kernel(x, mem, norm_g, w_in, w_out, mem_norm_g, w_mem_kv, diff_lambda_q1, diff_lambda_k1,
           diff_lambda_q2, diff_lambda_k2, diff_subln_g, final_norm_g):
    batch, seq, d = x.shape
    depth = w_in.shape[0]
    T = batch * seq
    xf = x.reshape(T, d)
    memf = mem.reshape(batch * mem.shape[1], d)

    norm_g3 = norm_g.reshape(depth, 1, d)
    mem_norm_g3 = mem_norm_g.reshape(depth, 1, d)
    scale = HEAD_DIM ** -0.5
    v_cols = (2 * MIX_WIDTH, 3 * MIX_WIDTH)

    for i in range(depth):
        kind = i % N_MIXERS
        q_scale = scale if kind == 0 else scale * LOG2E
        scales = [(0, MIX_WIDTH, q_scale), (3 * MIX_WIDTH, QKVM_WIDTH, scale * LOG2E)]
        kmean_cols = (MIX_WIDTH, 2 * MIX_WIDTH) if kind == 1 else None
        proj = _rms_proj(xf, norm_g3, w_in, i, [(0, QKVM_WIDTH), (QKVM_WIDTH, IN_WIDTH)],
                         [F32 if kind == 0 else BF16, BF16], tm=512, scales=scales,
                         vt_cols=None if kind == 0 else v_cols, kmean_cols=kmean_cols)
        qkvm, gate = proj[0], proj[1]
        mem_kv, mem_vt = _rms_proj(memf, mem_norm_g3, w_mem_kv, i, [(0, 2 * MEM_WIDTH)], [BF16],
                                   tm=N_MEM,
                                   vt_cols=(MEM_WIDTH, 2 * MEM_WIDTH))
        mem_kv = mem_kv.reshape(batch, N_MEM, 2 * MEM_WIDTH)
        if kind == 0:
            res = [_dilated_group(qkvm, g, batch=batch, seq=seq) for g in range(len(DSW_GROUPS))]
            mix_list = [r[0] for r in res] + [r[1] for r in res]
        elif kind == 1:
            mix_list = [_moba_mixer(qkvm, proj[2], proj[3], batch=batch, seq=seq)]
        else:
            c = i // N_MIXERS
            lambda_init = 0.8 - 0.6 * math.exp(-0.3 * i)
            lam_vecs = jnp.stack([diff_lambda_q1[c], diff_lambda_k1[c],
                                  diff_lambda_q2[c], diff_lambda_k2[c]]).astype(F32)
            mix_list = [_diff_mixer(qkvm, proj[2], lam_vecs, diff_subln_g[c], lambda_init,
                                    batch=batch, seq=seq)]
        fg = final_norm_g.reshape(1, d) if i == depth - 1 else None
        xf = _post(mix_list, xf, qkvm, gate, mem_kv, mem_vt, w_out, i, fg,
                   dilated=(kind == 0), batch=batch, seq=seq)
    return xf.reshape(batch, seq, d)
```

```python
import functools
import math

import numpy as np
import jax
import jax.numpy as jnp
from jax import lax
from jax.experimental import pallas as pl
from jax.experimental.pallas import tpu as pltpu

D_MODEL = 1024
DEPTH = 4
N_MIXERS = 3
HEAD_DIM = 64
MIX_WIDTH = 768
N_MEM_HEADS = 4
MEM_WIDTH = N_MEM_HEADS * HEAD_DIM
N_MEM = 256
BRANCH = MIX_WIDTH + MEM_WIDTH
IN_WIDTH = 3 * MIX_WIDTH + MEM_WIDTH + BRANCH
QKVM_WIDTH = 3 * MIX_WIDTH + MEM_WIDTH
RMS_EPS = 1e-6
SUBLN_EPS = 1e-5
NEG_INF = -1e30
LOG2E = math.log2(math.e)

DSW_GROUPS = ((128, 1), (512, 4), (2048, 16))
DSW_HEADS = 12
DSW_GROUP_WIDTH = 256
BAND_BLOCK = 128

MOBA_HEADS = 12
MOBA_BLOCK = 256
MOBA_TOPK = 3

DIFF_HEADS = 6
DIFF_D = 64

LANES = 128
QUERY_SUB = 256
MOBA_PREP_ROWS = 4096
DILATED_BLOCK_ROWS = 4096
MOBA_PAIRS_PER_CALL = 3
DIFF_HEADS_PER_CALL = 3
VMEM_LIMIT_BYTES = 56 * 1024 * 1024

BF16 = jnp.bfloat16
F32 = jnp.float32


def _alibi_slopes(n):
    return np.asarray(2.0 ** (-8.0 * np.arange(1, n + 1) / n), dtype=np.float32)


def _log2_slopes(n):
    return np.asarray(_alibi_slopes(n).astype(np.float64) * LOG2E, dtype=np.float32)


def _dot(a, b):
    return jnp.dot(a, b, preferred_element_type=F32)


def _dot_nt(a, b):
    return lax.dot_general(a, b, (((1,), (1,)), ((), ())), preferred_element_type=F32)


def _dot_tn(a, b):
    return lax.dot_general(a, b, (((0,), (0,)), ((), ())), preferred_element_type=F32)


def _compiler_params(n_axes):
    return pltpu.CompilerParams(dimension_semantics=("parallel",) * n_axes,
                                vmem_limit_bytes=VMEM_LIMIT_BYTES)


def _rms_proj_kernel(x_ref, g_ref, w_ref, *out_refs, splits, n_chunk, scales, vt_cols, kmean_cols):
    x = x_ref[...].astype(F32)
    y = x * lax.rsqrt(jnp.mean(x * x, axis=-1, keepdims=True) + RMS_EPS)
    h = (y * g_ref[...]).astype(BF16)
    tm = x.shape[0]
    for (lo, hi), o_ref in zip(splits, out_refs):
        for c in range(lo, hi, n_chunk):
            acc = _dot(h, w_ref[:, c:c + n_chunk].astype(BF16))
            for s_lo, s_hi, factor in scales:
                if s_lo <= c < s_hi:
                    acc = acc * factor
            o_ref[:, c - lo:c - lo + n_chunk] = acc.astype(o_ref.dtype)
            if vt_cols is not None and vt_cols[0] <= c < vt_cols[1]:
                vt_ref = out_refs[len(splits)]
                vt_ref[c - vt_cols[0]:c - vt_cols[0] + n_chunk, :] = acc.T.astype(BF16)
            if kmean_cols is not None and kmean_cols[0] <= c < kmean_cols[1]:
                km_ref = out_refs[-1]
                for r in range(tm // MOBA_BLOCK):
                    blk = acc[r * MOBA_BLOCK:(r + 1) * MOBA_BLOCK]
                    km_ref[r, :, c - kmean_cols[0]:c - kmean_cols[0] + n_chunk] = (
                        jnp.mean(blk, axis=0, keepdims=True))


def _rms_proj(x, g, w, layer, splits, dtypes, *, tm, n_chunk=256, scales=(), vt_cols=None,
              kmean_cols=None):
    T, D = x.shape
    N = w.shape[2]
    out_shape = [jax.ShapeDtypeStruct((T, hi - lo), dt) for (lo, hi), dt in zip(splits, dtypes)]
    out_specs = [pl.BlockSpec((tm, hi - lo), lambda i: (i, 0)) for (lo, hi) in splits]
    if vt_cols is not None:
        vw = vt_cols[1] - vt_cols[0]
        out_shape.append(jax.ShapeDtypeStruct((vw, T), BF16))
        out_specs.append(pl.BlockSpec((vw, tm), lambda i: (0, i)))
    if kmean_cols is not None:
        kw = kmean_cols[1] - kmean_cols[0]
        out_shape.append(jax.ShapeDtypeStruct((T // MOBA_BLOCK, 1, kw), F32))
        out_specs.append(pl.BlockSpec((tm // MOBA_BLOCK, 1, kw), lambda i: (i, 0, 0)))
    kern = functools.partial(_rms_proj_kernel, splits=tuple(splits), n_chunk=n_chunk,
                             scales=tuple(scales), vt_cols=vt_cols, kmean_cols=kmean_cols)
    return pl.pallas_call(
        kern,
        grid=(T // tm,),
        in_specs=[pl.BlockSpec((tm, D), lambda i: (i, 0)),
                  pl.BlockSpec((None, 1, D), lambda i: (layer, 0, 0)),
                  pl.BlockSpec((None, D, N), lambda i: (layer, 0, 0))],
        out_specs=out_specs,
        out_shape=out_shape,
        compiler_params=_compiler_params(1),
        name="rms_proj",
    )(x, g, w)


def _dilated_kernel(q_ref, kc_ref, kp_ref, vc_ref, vp_ref, o_ref, lse_ref, *, tu, dil, slopes):
    ui = pl.program_id(1)
    pair = pl.program_id(2)
    bb = BAND_BLOCK
    key = lax.broadcasted_iota(jnp.int32, (bb, bb), 0)
    qry = lax.broadcasted_iota(jnp.int32, (bb, bb), 1)
    d_cur = (qry - key).astype(F32)
    d_prev = d_cur + float(bb)
    valid_cur = qry >= key
    valid_prev_static = key >= qry
    lane = lax.broadcasted_iota(jnp.int32, (bb, LANES), 1)
    chan = lax.broadcasted_iota(jnp.int32, (LANES, bb), 0)
    head_slopes = [jnp.where(pair == 0, slopes[hh], slopes[2 + hh]) for hh in range(2)]

    def rows_of(blk, r):
        return pl.ds(blk * bb * dil + r, bb, stride=dil)

    def steps(ref, blk, r):
        return ref[rows_of(blk, r), :].astype(BF16)

    work = []
    for r in range(dil):
        for s in range(tu // bb):
            qs = steps(q_ref, s, r)
            v_cur = steps(vc_ref, s, r)
            if s == 0:
                k_prev, v_prev = steps(kp_ref, 0, r), steps(vp_ref, 0, r)
                valid_prev = jnp.logical_and(valid_prev_static, ui > 0)
            else:
                k_prev, v_prev = steps(kc_ref, s - 1, r), steps(vc_ref, s - 1, r)
                valid_prev = valid_prev_static
            k_cur = steps(kc_ref, s, r)
            scores = []
            for hh in range(2):
                own = (lane >= hh * HEAD_DIM) & (lane < (hh + 1) * HEAD_DIM)
                qh = jnp.where(own, qs, jnp.zeros_like(qs))
                scores.append((_dot_nt(k_cur, qh), _dot_nt(k_prev, qh)))
            work.append((r, s, v_cur, v_prev, valid_prev, scores))

    for r, s, v_cur, v_prev, valid_prev, scores in work:
        o_h, lse_h = [], []
        for hh in range(2):
            s_cur = jnp.where(valid_cur, scores[hh][0] - head_slopes[hh] * d_cur, NEG_INF)
            s_prev = jnp.where(valid_prev, scores[hh][1] - head_slopes[hh] * d_prev, NEG_INF)
            m = jnp.maximum(jnp.max(s_cur, axis=0, keepdims=True),
                            jnp.max(s_prev, axis=0, keepdims=True))
            p_cur = jnp.exp(s_cur - m)
            p_prev = jnp.exp(s_prev - m)
            l = jnp.sum(p_cur, axis=0, keepdims=True) + jnp.sum(p_prev, axis=0, keepdims=True)
            acc = _dot_tn(v_cur, p_cur.astype(BF16)) + _dot_tn(v_prev, p_prev.astype(BF16))
            o_h.append(acc / l)
            lse_h.append(jnp.broadcast_to(m + jnp.log(l), (LANES, bb)))
        first = chan < HEAD_DIM
        o_ref[rows_of(s, r), :] = jnp.where(first, o_h[0], o_h[1]).T
        lse_ref[rows_of(s, r), :] = jnp.where(first, lse_h[0], lse_h[1]).T


def _dilated_group(qkvm, g, *, batch, seq):
    window, dil = DSW_GROUPS[g]
    assert window // dil == BAND_BLOCK
    U = seq // dil
    tu = min(U, DILATED_BLOCK_ROWS // dil)
    gw = DSW_GROUP_WIDTH
    tiles = gw // LANES
    kcol = MIX_WIDTH // LANES
    x = qkvm.reshape(batch, seq, QKVM_WIDTH)
    sub = tu // BAND_BLOCK
    slopes = tuple(float(v) for v in _alibi_slopes(DSW_HEADS)[g * 4:(g + 1) * 4] * np.float32(dil))

    def cur(off):
        return pl.BlockSpec((None, tu * dil, LANES), lambda b, u, p: (b, u, off + g * tiles + p))

    def prev(off):
        return pl.BlockSpec((None, BAND_BLOCK * dil, LANES),
                            lambda b, u, p: (b, jnp.maximum(u * sub - 1, 0), off + g * tiles + p))

    out_spec = pl.BlockSpec((None, tu * dil, LANES), lambda b, u, p: (b, u, p))
    o, lse = pl.pallas_call(
        functools.partial(_dilated_kernel, tu=tu, dil=dil, slopes=slopes),
        grid=(batch, U // tu, tiles),
        in_specs=[cur(0), cur(kcol), prev(kcol), cur(2 * kcol), prev(2 * kcol)],
        out_specs=[out_spec, out_spec],
        out_shape=[jax.ShapeDtypeStruct((batch, seq, gw), F32)] * 2,
        compiler_params=_compiler_params(3),
        name=f"dilated_g{g}",
    )(x, x, x, x, x)
    return o.reshape(batch * seq, gw), lse.reshape(batch * seq, gw)


_SEL_LANES = 32
_ALIBI_PIECES = 3


def _bf16_pieces(v, n):
    out, rest = [], np.float32(v)
    for _ in range(n):
        piece = np.asarray(rest, np.float32).astype(jnp.bfloat16).astype(np.float32)
        out.append(float(piece))
        rest = np.float32(rest - piece)
    assert rest == 0.0, "slope does not split exactly into bf16 pieces"
    return out


def _moba_prep_kernel(const_ref, q_ref, k_ref, km_ref, qa_ref, ka_ref, *, nblk, blocks_per_step):
    tq = MOBA_BLOCK
    nsb = q_ref.shape[0] // tq
    km = km_ref[...]
    lane = lax.broadcasted_iota(jnp.int32, (tq, LANES), 1)
    rowi = lax.broadcasted_iota(jnp.int32, (tq, LANES), 0)
    km_lane = lax.broadcasted_iota(jnp.int32, (nblk, LANES), 1)
    km_row = lax.broadcasted_iota(jnp.int32, (nblk, LANES), 0)
    cand_blk = lax.broadcasted_iota(jnp.int32, (nblk, tq), 0)
    cand_blk_f = cand_blk.astype(F32)
    chains = [(sb, hh) for sb in range(nsb) for hh in range(2)]
    rows = [slice(sb * tq, (sb + 1) * tq) for sb in range(nsb)]
    own_blk = [pl.program_id(2) * nsb + sb for sb in range(nsb)]
    own = [(lane >= hh * HEAD_DIM) & (lane < (hh + 1) * HEAD_DIM) for hh in range(2)]
    off = [(1 - hh) * HEAD_DIM for hh in range(2)]
    km_h = [jnp.where((km_lane >= hh * HEAD_DIM) & (km_lane < (hh + 1) * HEAD_DIM), km, 0.0)
            .astype(BF16) for hh in range(2)]

    gates = []
    for sb, hh in chains:
        q2 = q_ref[rows[sb], :]
        gates.append(_dot_nt(km_h[hh], jnp.where(own[hh], q2, jnp.zeros_like(q2))))

    picked = []
    for c, (sb, hh) in enumerate(chains):
        g = jnp.where(cand_blk < own_blk[sb], gates[c], -jnp.inf)
        sel = jnp.zeros((nblk, tq), jnp.bool_)
        for _ in range(MOBA_TOPK):
            mx = jnp.max(g, axis=0, keepdims=True)
            hit = (g == mx) & (mx > -jnp.inf)
            idx = jnp.min(jnp.where(hit, cand_blk_f, float(nblk)), axis=0, keepdims=True)
            pick = cand_blk_f == idx
            sel = sel | pick
            g = jnp.where(pick, -jnp.inf, g)
        picked.append(jnp.where(sel, 1.0, 0.0).astype(BF16))

    place = [jnp.where(km_lane == km_row + off[hh], 1.0, 0.0).astype(BF16) for hh in range(2)]
    placed = [_dot_tn(picked[c], place[hh]) for c, (sb, hh) in enumerate(chains)]

    for c, (sb, hh) in enumerate(chains):
        n = own_blk[sb]
        blk = lane - off[hh]
        is_sel_lane = (blk >= 0) & (blk < _SEL_LANES)
        sel_val = jnp.where((placed[c] > 0.5) | (blk == n), 0.0, NEG_INF)
        q_extra = jnp.where(is_sel_lane, sel_val, const_ref[hh:hh + 1, :])
        qa = jnp.where(own[hh], q_ref[rows[sb], :].astype(F32), q_extra)
        a0 = off[hh] + _SEL_LANES
        is_fine = (lane >= a0) & (lane < a0 + _ALIBI_PIECES)
        is_coarse = (lane >= a0 + _ALIBI_PIECES) & (lane < a0 + 2 * _ALIBI_PIECES)
        coarse = (n % blocks_per_step).astype(F32)
        k_extra = jnp.where(blk == n, 1.0,
                            jnp.where(is_fine, rowi.astype(F32), jnp.where(is_coarse, coarse, 0.0)))
        ka = jnp.where(own[hh], k_ref[rows[sb], :].astype(F32), k_extra)
        qa_ref[rows[sb], hh * LANES:(hh + 1) * LANES] = qa.astype(BF16)
        ka_ref[rows[sb], hh * LANES:(hh + 1) * LANES] = ka.astype(BF16)


SUM_ROWS = 16


def _softmax_step(st, vt1, carry, shift):
    m, acc = carry
    m_new = jnp.maximum(m, jnp.max(st, axis=0, keepdims=True) + shift)
    p = jnp.exp2((st - (m_new - shift)).astype(BF16))
    acc = jnp.exp2(m - m_new) * acc + _dot(vt1, p)
    return m_new, acc


STALE_MAX_CAP = 64.0


def _stale_max_step(st, vt1, carry, shift):
    m, acc, bad = carry
    p = jnp.exp2((st - (m - shift)).astype(BF16))
    cmax = jnp.max(st, axis=0, keepdims=True) + shift
    m_new = jnp.maximum(m, cmax)
    bad = jnp.maximum(bad, jnp.where(cmax - m > STALE_MAX_CAP, 1.0, 0.0))
    acc = (acc + _dot(vt1, p)) * jnp.exp2(m - m_new)
    return m_new, acc, bad


SCORE_LOOKAHEAD = {_softmax_step: 6, _stale_max_step: 3}


def _key_step(step_fn, score_fn, vts, carry, shifts, masks=None):
    n_chain = len(carry)
    lookahead = SCORE_LOOKAHEAD[step_fn]
    sts, new = {}, [None] * n_chain
    for t in range(n_chain + lookahead):
        if t < n_chain:
            sts[t] = score_fn(t)
        c = t - lookahead
        if c >= 0:
            st = sts.pop(c)
            if masks is not None:
                st = jnp.where(masks[c], st, NEG_INF)
            new[c] = step_fn(st, vts[c], carry[c], shifts[c])
    return tuple(new)


def _causal_key_steps(scores_fn, vt_fn, finish, step_slopes, i, step):
    nsub = step // QUERY_SUB
    n_chain = len(step_slopes)

    def rows_of(j):
        return pl.ds(pl.multiple_of(j * step, step), step)

    def shifts_of(j):
        dj = (j - i).astype(F32)
        return [s * dj for s in step_slopes]

    own_rows = [pl.ds(pl.multiple_of(i * step, step), (qs + 1) * QUERY_SUB) for qs in range(nsub)]
    own_scores = [scores_fn(r) for r in own_rows]
    own_vts = [vt_fn(r) for r in own_rows]
    masks = []
    for qs in range(nsub):
        key = lax.broadcasted_iota(jnp.int32, ((qs + 1) * QUERY_SUB, QUERY_SUB), 0)
        qry = lax.broadcasted_iota(jnp.int32, ((qs + 1) * QUERY_SUB, QUERY_SUB), 1)
        masks.append(qry + qs * QUERY_SUB >= key)
    init = tuple((jnp.full((1, QUERY_SUB), NEG_INF, F32),
                  jnp.zeros((LANES + SUM_ROWS, QUERY_SUB), F32)) for _ in range(n_chain))
    own = _key_step(_softmax_step, lambda c: own_scores[c % nsub](c),
                    [own_vts[c % nsub][c] for c in range(n_chain)], init, [0.0] * n_chain,
                    [masks[c % nsub] for c in range(n_chain)])

    def fast_body(j, carry):
        return _key_step(_stale_max_step, scores_fn(rows_of(j)), vt_fn(rows_of(j)), carry,
                         shifts_of(j))

    fast = lax.fori_loop(0, i, fast_body,
                         tuple((m, acc, jnp.zeros((1, QUERY_SUB), F32)) for m, acc in own))
    overflowed = functools.reduce(jnp.maximum, [jnp.max(bad) for _, _, bad in fast]) > 0.0

    def exact_body(j, carry):
        return _key_step(_softmax_step, scores_fn(rows_of(j)), vt_fn(rows_of(j)), carry,
                         shifts_of(j))

    def split(acc):
        return acc[LANES:LANES + 1], acc[:LANES]

    finish([split(acc) for _, acc, _ in fast])

    @pl.when(overflowed)
    def _():
        finish([split(acc) for _, acc in lax.fori_loop(0, i, exact_body, own)])


def _with_sum_rows(vt):
    return jnp.concatenate([vt, jnp.ones((SUM_ROWS, vt.shape[1]), vt.dtype)], axis=0)


def _moba_kernel(slope_ref, qa_ref, ka_ref, vt_ref, o_ref, *, step):
    i = pl.program_id(2)
    nsub = step // QUERY_SUB
    n_heads = qa_ref.shape[1] // LANES
    chains = [(h, qs) for h in range(n_heads) for qs in range(nsub)]
    qa = [qa_ref[qs * QUERY_SUB:(qs + 1) * QUERY_SUB, h * LANES:(h + 1) * LANES]
          for h, qs in chains]
    step_slopes = [slope_ref[n_heads * pl.program_id(1) + h] * float(step) for h, _ in chains]

    def scores_fn(rows):
        return lambda c: _dot_nt(ka_ref[rows, chains[c][0] * LANES:(chains[c][0] + 1) * LANES], qa[c])

    def vt_fn(rows):
        per_pair = [_with_sum_rows(vt_ref[pp * LANES:(pp + 1) * LANES, rows])
                    for pp in range(n_heads // 2)]
        return tuple(per_pair[h // 2] for h, _ in chains)

    chan = lax.broadcasted_iota(jnp.int32, (LANES, QUERY_SUB), 0)

    def finish(res):
        for pp in range(n_heads // 2):
            for qs in range(nsub):
                (l0, acc0), (l1, acc1) = res[2 * pp * nsub + qs], res[(2 * pp + 1) * nsub + qs]
                o_ref[qs * QUERY_SUB:(qs + 1) * QUERY_SUB, pp * LANES:(pp + 1) * LANES] = (
                    jnp.where(chan < HEAD_DIM, acc0 / l0, acc1 / l1).T.astype(o_ref.dtype))

    _causal_key_steps(scores_fn, vt_fn, finish, step_slopes, i, step)


def _moba_mixer(qkvm, vt, kmean, *, batch, seq, step=512):
    nblk = seq // MOBA_BLOCK
    assert nblk <= _SEL_LANES and step % MOBA_BLOCK == 0 and seq % step == 0
    blocks_per_step = step // MOBA_BLOCK
    npair = MOBA_HEADS // 2
    x = qkvm.reshape(batch, seq, QKVM_WIDTH)
    km = kmean.reshape(batch, nblk, MIX_WIDTH)
    slopes = _log2_slopes(MOBA_HEADS)
    q_const = np.zeros((npair, 8, LANES), np.float32)
    for p in range(npair):
        for hh in range(2):
            a0 = (1 - hh) * HEAD_DIM + _SEL_LANES
            pieces = _bf16_pieces(slopes[2 * p + hh], _ALIBI_PIECES)
            q_const[p, hh, a0:a0 + _ALIBI_PIECES] = pieces
            q_const[p, hh, a0 + _ALIBI_PIECES:a0 + 2 * _ALIBI_PIECES] = [
                v * MOBA_BLOCK for v in pieces]
    tq = min(seq, MOBA_PREP_ROWS)
    kcol = MIX_WIDTH // LANES
    qa, ka = pl.pallas_call(
        functools.partial(_moba_prep_kernel, nblk=nblk, blocks_per_step=blocks_per_step),
        grid=(batch, npair, seq // tq),
        in_specs=[pl.BlockSpec((None, 8, LANES), lambda b, p, i: (p, 0, 0)),
                  pl.BlockSpec((None, tq, LANES), lambda b, p, i: (b, i, p)),
                  pl.BlockSpec((None, tq, LANES), lambda b, p, i: (b, i, kcol + p)),
                  pl.BlockSpec((None, nblk, LANES), lambda b, p, i: (b, 0, p))],
        out_specs=[pl.BlockSpec((None, tq, 2 * LANES), lambda b, p, i: (b, i, p))] * 2,
        out_shape=[jax.ShapeDtypeStruct((batch, seq, MOBA_HEADS * LANES), BF16)] * 2,
        compiler_params=_compiler_params(3),
        name="moba_prep",
    )(jnp.asarray(q_const), x, x, km)
    ppc = MOBA_PAIRS_PER_CALL
    out = pl.pallas_call(
        functools.partial(_moba_kernel, step=step),
        grid=(batch, npair // ppc, seq // step),
        in_specs=[pl.BlockSpec(memory_space=pltpu.SMEM),
                  pl.BlockSpec((None, step, 2 * ppc * LANES), lambda b, p, i: (b, i, p)),
                  pl.BlockSpec((None, seq, 2 * ppc * LANES), lambda b, p, i: (b, 0, p)),
                  pl.BlockSpec((ppc * LANES, seq), lambda b, p, i: (p, b))],
        out_specs=pl.BlockSpec((None, step, ppc * LANES), lambda b, p, i: (b, i, p)),
        out_shape=jax.ShapeDtypeStruct((batch, seq, MIX_WIDTH), BF16),
        compiler_params=_compiler_params(3),
        name="moba_attn",
    )(jnp.asarray(slopes), qa, ka, vt)
    return out.reshape(batch * seq, MIX_WIDTH)


def _extras_lane(mi):
    return (1 - mi) * DIFF_D


def _position_lanes(step):
    tab = np.zeros((step, 2 * LANES), np.float32)
    r = np.arange(step)
    for mi in range(2):
        a0 = mi * LANES + _extras_lane(mi)
        tab[:, a0:a0 + _ALIBI_PIECES] = (r % MOBA_BLOCK)[:, None]
        tab[:, a0 + _ALIBI_PIECES:a0 + 2 * _ALIBI_PIECES] = (r // MOBA_BLOCK)[:, None]
    return jnp.asarray(tab, dtype=BF16)


def _diff_kernel(slope_ref, qx_ref, kx_ref, lam_ref, g_ref, q_ref, k_ref, vt_ref, o_ref, *,
                 step, lambda_init):
    i = pl.program_id(2)
    nsub = step // QUERY_SUB
    n_heads = q_ref.shape[1] // LANES
    lane_q = lax.broadcasted_iota(jnp.int32, (QUERY_SUB, LANES), 1)
    chains = [(h, mi, qs) for h in range(n_heads) for mi in range(2) for qs in range(nsub)]
    qa = []
    for h, mi, qs in chains:
        q2 = q_ref[qs * QUERY_SUB:(qs + 1) * QUERY_SUB, h * LANES:(h + 1) * LANES]
        qx = jnp.broadcast_to(qx_ref[h, mi:mi + 1, :], (QUERY_SUB, LANES)).astype(BF16)
        own_map = (lane_q >= mi * DIFF_D) & (lane_q < (mi + 1) * DIFF_D)
        qa.append(jnp.where(own_map, q2, qx))
    step_slopes = [slope_ref[n_heads * pl.program_id(1) + h] * float(step) for h, _, _ in chains]

    def scores_fn(rows):
        ka, n = {}, rows.size
        lane_k = lax.broadcasted_iota(jnp.int32, (n, LANES), 1)
        for h in range(n_heads):
            kb = k_ref[rows, h * LANES:(h + 1) * LANES]
            for mi in range(2):
                own_k = (lane_k >= mi * DIFF_D) & (lane_k < (mi + 1) * DIFF_D)
                ka[h, mi] = jnp.where(own_k, kb, kx_ref[0:n, mi * LANES:(mi + 1) * LANES])
        return lambda c: _dot_nt(ka[chains[c][0], chains[c][1]], qa[c])

    def vt_fn(rows):
        per_head = [_with_sum_rows(vt_ref[h * LANES:(h + 1) * LANES, rows]) for h in range(n_heads)]
        return tuple(per_head[h] for h, _, _ in chains)

    def finish(res):
        lq1, lk1, lq2, lk2 = (lam_ref[r:r + 1, :] for r in range(4))
        lam = (jnp.exp(jnp.sum(lq1 * lk1, axis=1, keepdims=True))
               - jnp.exp(jnp.sum(lq2 * lk2, axis=1, keepdims=True)) + lambda_init)
        for h in range(n_heads):
            for qs in range(nsub):
                (l1, acc1), (l2, acc2) = res[(2 * h) * nsub + qs], res[(2 * h + 1) * nsub + qs]
                o = acc1 / l1 - lam * (acc2 / l2)
                y = o * lax.rsqrt(jnp.mean(o * o, axis=0, keepdims=True) + SUBLN_EPS)
                o_ref[qs * QUERY_SUB:(qs + 1) * QUERY_SUB, h * LANES:(h + 1) * LANES] = (
                    (y * g_ref[...]) * (1.0 - lambda_init)).T.astype(o_ref.dtype)

    _causal_key_steps(scores_fn, vt_fn, finish, step_slopes, i, step)


def _diff_mixer(qkvm, vt, lam_vecs, subln_g, lambda_init, *, batch, seq, step=512):
    assert step % MOBA_BLOCK == 0 and seq % step == 0
    x = qkvm.reshape(batch, seq, QKVM_WIDTH)
    slopes = _log2_slopes(DIFF_HEADS)
    q_extra = np.zeros((DIFF_HEADS, 8, LANES), np.float32)
    for h in range(DIFF_HEADS):
        pieces = _bf16_pieces(slopes[h], _ALIBI_PIECES)
        for mi in range(2):
            a0 = _extras_lane(mi)
            q_extra[h, mi, a0:a0 + _ALIBI_PIECES] = pieces
            q_extra[h, mi, a0 + _ALIBI_PIECES:a0 + 2 * _ALIBI_PIECES] = [
                v * MOBA_BLOCK for v in pieces]
    kcol = MIX_WIDTH // LANES
    hpc = DIFF_HEADS_PER_CALL
    assert DIFF_HEADS % hpc == 0 and kcol % hpc == 0
    out = pl.pallas_call(
        functools.partial(_diff_kernel, step=step, lambda_init=float(lambda_init)),
        grid=(batch, DIFF_HEADS // hpc, seq // step),
        in_specs=[pl.BlockSpec(memory_space=pltpu.SMEM),
                  pl.BlockSpec((hpc, 8, LANES), lambda b, h, i: (h, 0, 0)),
                  pl.BlockSpec((step, 2 * LANES), lambda b, h, i: (0, 0)),
                  pl.BlockSpec((4, DIFF_D), lambda b, h, i: (0, 0)),
                  pl.BlockSpec((2 * DIFF_D, 1), lambda b, h, i: (0, 0)),
                  pl.BlockSpec((None, step, hpc * LANES), lambda b, h, i: (b, i, h)),
                  pl.BlockSpec((None, seq, hpc * LANES), lambda b, h, i: (b, 0, kcol // hpc + h)),
                  pl.BlockSpec((hpc * LANES, seq), lambda b, h, i: (h, b))],
        out_specs=pl.BlockSpec((None, step, hpc * LANES), lambda b, h, i: (b, i, h)),
        out_shape=jax.ShapeDtypeStruct((batch, seq, MIX_WIDTH), BF16),
        compiler_params=_compiler_params(3),
        name="diff_attn",
    )(jnp.asarray(slopes), jnp.asarray(q_extra), _position_lanes(step), lam_vecs,
      subln_g.reshape(2 * DIFF_D, 1), x, x, vt)
    return out.reshape(batch * seq, MIX_WIDTH)


def _post_kernel(*refs, n_mix, dilated, final):
    mix_refs = refs[:n_mix]
    x_ref, qm_ref, gate_ref, kv_ref, vmt_ref, w_ref = refs[n_mix:n_mix + 6]
    rest = refs[n_mix + 6:]
    fg_ref = rest[0] if final else None
    o_ref = rest[-1]
    tm = x_ref.shape[0]

    if dilated:
        outs = [r[...] for r in mix_refs[:3]]
        lses = [r[...] for r in mix_refs[3:]]
        mx = jnp.maximum(jnp.maximum(lses[0], lses[1]), lses[2])
        es = [jnp.exp(v - mx) for v in lses]
        den = es[0] + es[1] + es[2]
        mix_parts = [o * (e / den) for o, e in zip(outs, es)]
    else:
        mix = mix_refs[0][...].astype(F32)
        mix_parts = [mix[:, g * DSW_GROUP_WIDTH:(g + 1) * DSW_GROUP_WIDTH] for g in range(3)]

    lane = lax.broadcasted_iota(jnp.int32, (tm, LANES), 1)
    chan = lax.broadcasted_iota(jnp.int32, (LANES, tm), 0)
    n_pair = MEM_WIDTH // LANES
    scores = []
    for pair in range(n_pair):
        cs = slice(pair * LANES, (pair + 1) * LANES)
        q2 = qm_ref[:, cs].astype(BF16)
        for hh in range(2):
            own = (lane >= hh * HEAD_DIM) & (lane < (hh + 1) * HEAD_DIM)
            scores.append(_dot_nt(kv_ref[:, cs], jnp.where(own, q2, jnp.zeros_like(q2))))
    mem_parts = []
    for pair in range(n_pair):
        vt1 = _with_sum_rows(vmt_ref[pair * LANES:(pair + 1) * LANES, :])
        o_h = []
        for hh in range(2):
            st = scores[2 * pair + hh]
            p = jnp.exp2((st - jnp.max(st, axis=0, keepdims=True)).astype(BF16))
            acc = _dot(vt1, p)
            o_h.append(acc[:LANES] / acc[LANES:LANES + 1])
        mem_parts.append(jnp.where(chan < HEAD_DIM, o_h[0], o_h[1]).T)

    parts = [(g * DSW_GROUP_WIDTH, mix_parts[g]) for g in range(3)]
    parts += [(MIX_WIDTH + p * LANES, mem_parts[p]) for p in range(len(mem_parts))]
    ys = []
    for c0, val in parts:
        gt = gate_ref[:, c0:c0 + val.shape[1]].astype(F32)
        ys.append((val * (gt * (1.0 / (1.0 + jnp.exp(-gt))))).astype(BF16))
    acc = x_ref[...] + _dot(jnp.concatenate(ys, axis=1), w_ref[...].astype(BF16))
    if final:
        yn = acc * lax.rsqrt(jnp.mean(acc * acc, axis=-1, keepdims=True) + RMS_EPS)
        acc = yn * fg_ref[...]
    o_ref[...] = acc


def _post(mix_list, x, qkvm, gate, mem_kv, mem_vt, w_out, layer, final_g, *, dilated, batch, seq,
          tm=1024):
    T = x.shape[0]
    tiles_per_batch = seq // tm
    n_mix = len(mix_list)
    final = final_g is not None
    qm_block = (3 * MIX_WIDTH) // MEM_WIDTH
    in_specs = [pl.BlockSpec((tm, m.shape[1]), lambda i: (i, 0)) for m in mix_list]
    in_specs += [pl.BlockSpec((tm, D_MODEL), lambda i: (i, 0)),
                 pl.BlockSpec((tm, MEM_WIDTH), lambda i: (i, qm_block)),
                 pl.BlockSpec((tm, BRANCH), lambda i: (i, 0)),
                 pl.BlockSpec((None, N_MEM, 2 * MEM_WIDTH), lambda i: (i // tiles_per_batch, 0, 0)),
                 pl.BlockSpec((MEM_WIDTH, N_MEM), lambda i: (0, i // tiles_per_batch)),
                 pl.BlockSpec((None, BRANCH, D_MODEL), lambda i: (layer, 0, 0))]
    args = list(mix_list) + [x, qkvm, gate, mem_kv, mem_vt, w_out]
    if final:
        in_specs.append(pl.BlockSpec((1, D_MODEL), lambda i: (0, 0)))
        args.append(final_g)
    return pl.pallas_call(
        functools.partial(_post_kernel, n_mix=n_mix, dilated=dilated, final=final),
        grid=(T // tm,),
        in_specs=in_specs,
        out_specs=pl.BlockSpec((tm, D_MODEL), lambda i: (i, 0)),
        out_shape=jax.ShapeDtypeStruct((T, D_MODEL), F32),
        compiler_params=_compiler_params(1),
        name="post",
    )(*args)


def kernel(x, mem, norm_g, w_in, w_out, mem_norm_g, w_mem_kv, diff_lambda_q1, diff_lambda_k1,
           diff_lambda_q2, diff_lambda_k2, diff_subln_g, final_norm_g):
    batch, seq, d = x.shape
    depth = w_in.shape[0]
    T = batch * seq
    xf = x.reshape(T, d)
    memf = mem.reshape(batch * mem.shape[1], d)

    norm_g3 = norm_g.reshape(depth, 1, d)
    mem_norm_g3 = mem_norm_g.reshape(depth, 1, d)
    scale = HEAD_DIM ** -0.5
    v_cols = (2 * MIX_WIDTH, 3 * MIX_WIDTH)

    for i in range(depth):
        kind = i % N_MIXERS
        q_scale = scale if kind == 0 else scale * LOG2E
        scales = [(0, MIX_WIDTH, q_scale), (3 * MIX_WIDTH, QKVM_WIDTH, scale * LOG2E)]
        kmean_cols = (MIX_WIDTH, 2 * MIX_WIDTH) if kind == 1 else None
        proj = _rms_proj(xf, norm_g3, w_in, i, [(0, QKVM_WIDTH), (QKVM_WIDTH, IN_WIDTH)],
                         [F32 if kind == 0 else BF16, BF16], tm=512, scales=scales,
                         vt_cols=None if kind == 0 else v_cols, kmean_cols=kmean_cols)
        qkvm, gate = proj[0], proj[1]
        mem_kv, mem_vt = _rms_proj(memf, mem_norm_g3, w_mem_kv, i, [(0, 2 * MEM_WIDTH)], [BF16],
                                   tm=N_MEM,
                                   vt_cols=(MEM_WIDTH, 2 * MEM_WIDTH))
        mem_kv = mem_kv.reshape(batch, N_MEM, 2 * MEM_WIDTH)
        if kind == 0:
            res = [_dilated_group(qkvm, g, batch=batch, seq=seq) for g in range(len(DSW_GROUPS))]
            mix_list = [r[0] for r in res] + [r[1] for r in res]
        elif kind == 1:
            mix_list = [_moba_mixer(qkvm, proj[2], proj[3], batch=batch, seq=seq)]
        else:
            c = i // N_MIXERS
            lambda_init = 0.8 - 0.6 * math.exp(-0.3 * i)
            lam_vecs = jnp.stack([diff_lambda_q1[c], diff_lambda_k1[c],
                                  diff_lambda_q2[c], diff_lambda_k2[c]]).astype(F32)
            mix_list = [_diff_mixer(qkvm, proj[2], lam_vecs, diff_subln_g[c], lambda_init,
                                    batch=batch, seq=seq)]
        fg = final_norm_g.reshape(1, d) if i == depth - 1 else None
        xf = _post(mix_list, xf, qkvm, gate, mem_kv, mem_vt, w_out, i, fg,
                   dilated=(kind == 0), batch=batch, seq=seq)
    return xf.reshape(batch, seq, d)
```

```python
import functools
import math

import numpy as np
import jax
import jax.numpy as jnp
from jax import lax
from jax.experimental import pallas as pl
from jax.experimental.pallas import tpu as pltpu

D_MODEL = 1024
DEPTH = 4
N_MIXERS = 3
HEAD_DIM = 64
MIX_WIDTH = 768
N_MEM_HEADS = 4
MEM_WIDTH = N_MEM_HEADS * HEAD_DIM
N_MEM = 256
BRANCH = MIX_WIDTH + MEM_WIDTH
IN_WIDTH = 3 * MIX_WIDTH + MEM_WIDTH + BRANCH
QKVM_WIDTH = 3 * MIX_WIDTH + MEM_WIDTH
RMS_EPS = 1e-6
SUBLN_EPS = 1e-5
NEG_INF = -1e30
LOG2E = math.log2(math.e)

DSW_GROUPS = ((128, 1), (512, 4), (2048, 16))
DSW_HEADS = 12
DSW_GROUP_WIDTH = 256
BAND_BLOCK = 128

MOBA_HEADS = 12
MOBA_BLOCK = 256
MOBA_TOPK = 3

DIFF_HEADS = 6
DIFF_D = 64

LANES = 128
QUERY_SUB = 256
MOBA_PREP_ROWS = 4096
DILATED_BLOCK_ROWS = 4096
MOBA_PAIRS_PER_CALL = 3
DIFF_HEADS_PER_CALL = 3
VMEM_LIMIT_BYTES = 56 * 1024 * 1024

BF16 = jnp.bfloat16
F32 = jnp.float32


def _alibi_slopes(n):
    return np.asarray(2.0 ** (-8.0 * np.arange(1, n + 1) / n), dtype=np.float32)


def _log2_slopes(n):
    return np.asarray(_alibi_slopes(n).astype(np.float64) * LOG2E, dtype=np.float32)


def _dot(a, b):
    return jnp.dot(a, b, preferred_element_type=F32)


def _dot_nt(a, b):
    return lax.dot_general(a, b, (((1,), (1,)), ((), ())), preferred_element_type=F32)


def _dot_tn(a, b):
    return lax.dot_general(a, b, (((0,), (0,)), ((), ())), preferred_element_type=F32)


def _compiler_params(n_axes):
    return pltpu.CompilerParams(dimension_semantics=("parallel",) * n_axes,
                                vmem_limit_bytes=VMEM_LIMIT_BYTES)


def _rms_proj_kernel(x_ref, g_ref, w_ref, *out_refs, splits, n_chunk, scales, vt_cols, kmean_cols):
    x = x_ref[...].astype(F32)
    y = x * lax.rsqrt(jnp.mean(x * x, axis=-1, keepdims=True) + RMS_EPS)
    h = (y * g_ref[...]).astype(BF16)
    tm = x.shape[0]
    for (lo, hi), o_ref in zip(splits, out_refs):
        for c in range(lo, hi, n_chunk):
            acc = _dot(h, w_ref[:, c:c + n_chunk].astype(BF16))
            for s_lo, s_hi, factor in scales:
                if s_lo <= c < s_hi:
                    acc = acc * factor
            o_ref[:, c - lo:c - lo + n_chunk] = acc.astype(o_ref.dtype)
            if vt_cols is not None and vt_cols[0] <= c < vt_cols[1]:
                vt_ref = out_refs[len(splits)]
                vt_ref[c - vt_cols[0]:c - vt_cols[0] + n_chunk, :] = acc.T.astype(BF16)
            if kmean_cols is not None and kmean_cols[0] <= c < kmean_cols[1]:
                km_ref = out_refs[-1]
                for r in range(tm // MOBA_BLOCK):
                    blk = acc[r * MOBA_BLOCK:(r + 1) * MOBA_BLOCK]
                    km_ref[r, :, c - kmean_cols[0]:c - kmean_cols[0] + n_chunk] = (
                        jnp.mean(blk, axis=0, keepdims=True))


def _rms_proj(x, g, w, layer, splits, dtypes, *, tm, n_chunk=256, scales=(), vt_cols=None,
              kmean_cols=None):
    T, D = x.shape
    N = w.shape[2]
    out_shape = [jax.ShapeDtypeStruct((T, hi - lo), dt) for (lo, hi), dt in zip(splits, dtypes)]
    out_specs = [pl.BlockSpec((tm, hi - lo), lambda i: (i, 0)) for (lo, hi) in splits]
    if vt_cols is not None:
        vw = vt_cols[1] - vt_cols[0]
        out_shape.append(jax.ShapeDtypeStruct((vw, T), BF16))
        out_specs.append(pl.BlockSpec((vw, tm), lambda i: (0, i)))
    if kmean_cols is not None:
        kw = kmean_cols[1] - kmean_cols[0]
        out_shape.append(jax.ShapeDtypeStruct((T // MOBA_BLOCK, 1, kw), F32))
        out_specs.append(pl.BlockSpec((tm // MOBA_BLOCK, 1, kw), lambda i: (i, 0, 0)))
    kern = functools.partial(_rms_proj_kernel, splits=tuple(splits), n_chunk=n_chunk,
                             scales=tuple(scales), vt_cols=vt_cols, kmean_cols=kmean_cols)
    return pl.pallas_call(
        kern,
        grid=(T // tm,),
        in_specs=[pl.BlockSpec((tm, D), lambda i: (i, 0)),
                  pl.BlockSpec((None, 1, D), lambda i: (layer, 0, 0)),
                  pl.BlockSpec((None, D, N), lambda i: (layer, 0, 0),
                               pipeline_mode=pl.Buffered(1))],
        out_specs=out_specs,
        out_shape=out_shape,
        compiler_params=_compiler_params(1),
        name="rms_proj",
    )(x, g, w)


def _dilated_kernel(q_ref, kc_ref, kp_ref, vc_ref, vp_ref, o_ref, lse_ref, *, tu, dil, slopes):
    ui = pl.program_id(1)
    pair = pl.program_id(2)
    bb = BAND_BLOCK
    key = lax.broadcasted_iota(jnp.int32, (bb, bb), 0)
    qry = lax.broadcasted_iota(jnp.int32, (bb, bb), 1)
    d_cur = (qry - key).astype(F32)
    d_prev = d_cur + float(bb)
    valid_cur = qry >= key
    valid_prev_static = key >= qry
    lane = lax.broadcasted_iota(jnp.int32, (bb, LANES), 1)
    chan = lax.broadcasted_iota(jnp.int32, (LANES, bb), 0)
    head_slopes = [jnp.where(pair == 0, slopes[hh], slopes[2 + hh]) for hh in range(2)]

    def rows_of(blk, r):
        return pl.ds(blk * bb * dil + r, bb, stride=dil)

    def steps(ref, blk, r):
        return ref[rows_of(blk, r), :].astype(BF16)

    work = []
    for r in range(dil):
        for s in range(tu // bb):
            qs = steps(q_ref, s, r)
            v_cur = steps(vc_ref, s, r)
            if s == 0:
                k_prev, v_prev = steps(kp_ref, 0, r), steps(vp_ref, 0, r)
                valid_prev = jnp.logical_and(valid_prev_static, ui > 0)
            else:
                k_prev, v_prev = steps(kc_ref, s - 1, r), steps(vc_ref, s - 1, r)
                valid_prev = valid_prev_static
            k_cur = steps(kc_ref, s, r)
            scores = []
            for hh in range(2):
                own = (lane >= hh * HEAD_DIM) & (lane < (hh + 1) * HEAD_DIM)
                qh = jnp.where(own, qs, jnp.zeros_like(qs))
                scores.append((_dot_nt(k_cur, qh), _dot_nt(k_prev, qh)))
            work.append((r, s, v_cur, v_prev, valid_prev, scores))

    for r, s, v_cur, v_prev, valid_prev, scores in work:
        o_h, lse_h = [], []
        for hh in range(2):
            s_cur = jnp.where(valid_cur, scores[hh][0] - head_slopes[hh] * d_cur, NEG_INF)
            s_prev = jnp.where(valid_prev, scores[hh][1] - head_slopes[hh] * d_prev, NEG_INF)
            m = jnp.maximum(jnp.max(s_cur, axis=0, keepdims=True),
                            jnp.max(s_prev, axis=0, keepdims=True))
            p_cur = jnp.exp(s_cur - m)
            p_prev = jnp.exp(s_prev - m)
            l = jnp.sum(p_cur, axis=0, keepdims=True) + jnp.sum(p_prev, axis=0, keepdims=True)
            acc = _dot_tn(v_cur, p_cur.astype(BF16)) + _dot_tn(v_prev, p_prev.astype(BF16))
            o_h.append(acc / l)
            lse_h.append(jnp.broadcast_to(m + jnp.log(l), (LANES, bb)))
        first = chan < HEAD_DIM
        o_ref[rows_of(s, r), :] = jnp.where(first, o_h[0], o_h[1]).T
        lse_ref[rows_of(s, r), :] = jnp.where(first, lse_h[0], lse_h[1]).T


def _dilated_group(qkvm, g, *, batch, seq):
    window, dil = DSW_GROUPS[g]
    assert window // dil == BAND_BLOCK
    U = seq // dil
    tu = min(U, DILATED_BLOCK_ROWS // dil)
    gw = DSW_GROUP_WIDTH
    tiles = gw // LANES
    kcol = MIX_WIDTH // LANES
    x = qkvm.reshape(batch, seq, QKVM_WIDTH)
    sub = tu // BAND_BLOCK
    slopes = tuple(float(v) for v in _alibi_slopes(DSW_HEADS)[g * 4:(g + 1) * 4] * np.float32(dil))

    def cur(off):
        return pl.BlockSpec((None, tu * dil, LANES), lambda b, u, p: (b, u, off + g * tiles + p))

    def prev(off):
        return pl.BlockSpec((None, BAND_BLOCK * dil, LANES),
                            lambda b, u, p: (b, jnp.maximum(u * sub - 1, 0), off + g * tiles + p))

    out_spec = pl.BlockSpec((None, tu * dil, LANES), lambda b, u, p: (b, u, p))
    o, lse = pl.pallas_call(
        functools.partial(_dilated_kernel, tu=tu, dil=dil, slopes=slopes),
        grid=(batch, U // tu, tiles),
        in_specs=[cur(0), cur(kcol), prev(kcol), cur(2 * kcol), prev(2 * kcol)],
        out_specs=[out_spec, out_spec],
        out_shape=[jax.ShapeDtypeStruct((batch, seq, gw), F32)] * 2,
        compiler_params=_compiler_params(3),
        name=f"dilated_g{g}",
    )(x, x, x, x, x)
    return o.reshape(batch * seq, gw), lse.reshape(batch * seq, gw)


_SEL_LANES = 32
_ALIBI_PIECES = 3


def _bf16_pieces(v, n):
    out, rest = [], np.float32(v)
    for _ in range(n):
        piece = np.asarray(rest, np.float32).astype(jnp.bfloat16).astype(np.float32)
        out.append(float(piece))
        rest = np.float32(rest - piece)
    assert rest == 0.0, "slope does not split exactly into bf16 pieces"
    return out


def _moba_prep_kernel(const_ref, q_ref, k_ref, km_ref, qa_ref, ka_ref, *, nblk, blocks_per_step):
    tq = MOBA_BLOCK
    nsb = q_ref.shape[0] // tq
    km = km_ref[...]
    lane = lax.broadcasted_iota(jnp.int32, (tq, LANES), 1)
    rowi = lax.broadcasted_iota(jnp.int32, (tq, LANES), 0)
    km_lane = lax.broadcasted_iota(jnp.int32, (nblk, LANES), 1)
    km_row = lax.broadcasted_iota(jnp.int32, (nblk, LANES), 0)
    cand_blk = lax.broadcasted_iota(jnp.int32, (nblk, tq), 0)
    cand_blk_f = cand_blk.astype(F32)
    chains = [(sb, hh) for sb in range(nsb) for hh in range(2)]
    rows = [slice(sb * tq, (sb + 1) * tq) for sb in range(nsb)]
    own_blk = [pl.program_id(2) * nsb + sb for sb in range(nsb)]
    own = [(lane >= hh * HEAD_DIM) & (lane < (hh + 1) * HEAD_DIM) for hh in range(2)]
    off = [(1 - hh) * HEAD_DIM for hh in range(2)]
    km_h = [jnp.where((km_lane >= hh * HEAD_DIM) & (km_lane < (hh + 1) * HEAD_DIM), km, 0.0)
            .astype(BF16) for hh in range(2)]

    gates = []
    for sb, hh in chains:
        q2 = q_ref[rows[sb], :]
        gates.append(_dot_nt(km_h[hh], jnp.where(own[hh], q2, jnp.zeros_like(q2))))

    picked = []
    for c, (sb, hh) in enumerate(chains):
        g = jnp.where(cand_blk < own_blk[sb], gates[c], -jnp.inf)
        sel = jnp.zeros((nblk, tq), jnp.bool_)
        for _ in range(MOBA_TOPK):
            mx = jnp.max(g, axis=0, keepdims=True)
            hit = (g == mx) & (mx > -jnp.inf)
            idx = jnp.min(jnp.where(hit, cand_blk_f, float(nblk)), axis=0, keepdims=True)
            pick = cand_blk_f == idx
            sel = sel | pick
            g = jnp.where(pick, -jnp.inf, g)
        picked.append(jnp.where(sel, 1.0, 0.0).astype(BF16))

    place = [jnp.where(km_lane == km_row + off[hh], 1.0, 0.0).astype(BF16) for hh in range(2)]
    placed = [_dot_tn(picked[c], place[hh]) for c, (sb, hh) in enumerate(chains)]

    for c, (sb, hh) in enumerate(chains):
        n = own_blk[sb]
        blk = lane - off[hh]
        is_sel_lane = (blk >= 0) & (blk < _SEL_LANES)
        sel_val = jnp.where((placed[c] > 0.5) | (blk == n), 0.0, NEG_INF)
        q_extra = jnp.where(is_sel_lane, sel_val, const_ref[hh:hh + 1, :])
        qa = jnp.where(own[hh], q_ref[rows[sb], :].astype(F32), q_extra)
        a0 = off[hh] + _SEL_LANES
        is_fine = (lane >= a0) & (lane < a0 + _ALIBI_PIECES)
        is_coarse = (lane >= a0 + _ALIBI_PIECES) & (lane < a0 + 2 * _ALIBI_PIECES)
        coarse = (n % blocks_per_step).astype(F32)
        k_extra = jnp.where(blk == n, 1.0,
                            jnp.where(is_fine, rowi.astype(F32), jnp.where(is_coarse, coarse, 0.0)))
        ka = jnp.where(own[hh], k_ref[rows[sb], :].astype(F32), k_extra)
        qa_ref[rows[sb], hh * LANES:(hh + 1) * LANES] = qa.astype(BF16)
        ka_ref[rows[sb], hh * LANES:(hh + 1) * LANES] = ka.astype(BF16)


SUM_ROWS = 16


def _softmax_step(st, vt1, carry, shift):
    m, acc = carry
    m_new = jnp.maximum(m, jnp.max(st, axis=0, keepdims=True) + shift)
    p = jnp.exp2((st - (m_new - shift)).astype(BF16))
    acc = jnp.exp2(m - m_new) * acc + _dot(vt1, p)
    return m_new, acc


STALE_MAX_CAP = 64.0


def _stale_max_step(st, vt1, carry, shift):
    m, acc, bad = carry
    p = jnp.exp2((st - (m - shift)).astype(BF16))
    cmax = jnp.max(st, axis=0, keepdims=True) + shift
    m_new = jnp.maximum(m, cmax)
    bad = jnp.maximum(bad, jnp.where(cmax - m > STALE_MAX_CAP, 1.0, 0.0))
    acc = (acc + _dot(vt1, p)) * jnp.exp2(m - m_new)
    return m_new, acc, bad


SCORE_LOOKAHEAD = {_softmax_step: 6, _stale_max_step: 3}


def _key_step(step_fn, score_fn, vts, carry, shifts, masks=None):
    n_chain = len(carry)
    lookahead = SCORE_LOOKAHEAD[step_fn]
    sts, new = {}, [None] * n_chain
    for t in range(n_chain + lookahead):
        if t < n_chain:
            sts[t] = score_fn(t)
        c = t - lookahead
        if c >= 0:
            st = sts.pop(c)
            if masks is not None:
                st = jnp.where(masks[c], st, NEG_INF)
            new[c] = step_fn(st, vts[c], carry[c], shifts[c])
    return tuple(new)


def _causal_key_steps(scores_fn, vt_fn, finish, step_slopes, i, step):
    nsub = step // QUERY_SUB
    n_chain = len(step_slopes)

    def rows_of(j):
        return pl.ds(pl.multiple_of(j * step, step), step)

    def shifts_of(j):
        dj = (j - i).astype(F32)
        return [s * dj for s in step_slopes]

    own_rows = [pl.ds(pl.multiple_of(i * step, step), (qs + 1) * QUERY_SUB) for qs in range(nsub)]
    own_scores = [scores_fn(r) for r in own_rows]
    own_vts = [vt_fn(r) for r in own_rows]
    masks = []
    for qs in range(nsub):
        key = lax.broadcasted_iota(jnp.int32, ((qs + 1) * QUERY_SUB, QUERY_SUB), 0)
        qry = lax.broadcasted_iota(jnp.int32, ((qs + 1) * QUERY_SUB, QUERY_SUB), 1)
        masks.append(qry + qs * QUERY_SUB >= key)
    init = tuple((jnp.full((1, QUERY_SUB), NEG_INF, F32),
                  jnp.zeros((LANES + SUM_ROWS, QUERY_SUB), F32)) for _ in range(n_chain))
    own = _key_step(_softmax_step, lambda c: own_scores[c % nsub](c),
                    [own_vts[c % nsub][c] for c in range(n_chain)], init, [0.0] * n_chain,
                    [masks[c % nsub] for c in range(n_chain)])

    def fast_body(j, carry):
        return _key_step(_stale_max_step, scores_fn(rows_of(j)), vt_fn(rows_of(j)), carry,
                         shifts_of(j))

    fast = lax.fori_loop(0, i, fast_body,
                         tuple((m, acc, jnp.zeros((1, QUERY_SUB), F32)) for m, acc in own))
    overflowed = functools.reduce(jnp.maximum, [jnp.max(bad) for _, _, bad in fast]) > 0.0

    def exact_body(j, carry):
        return _key_step(_softmax_step, scores_fn(rows_of(j)), vt_fn(rows_of(j)), carry,
                         shifts_of(j))

    def split(acc):
        return acc[LANES:LANES + 1], acc[:LANES]

    finish([split(acc) for _, acc, _ in fast])

    @pl.when(overflowed)
    def _():
        finish([split(acc) for _, acc in lax.fori_loop(0, i, exact_body, own)])


def _with_sum_rows(vt):
    return jnp.concatenate([vt, jnp.ones((SUM_ROWS, vt.shape[1]), vt.dtype)], axis=0)


def _moba_kernel(slope_ref, qa_ref, ka_ref, vt_ref, o_ref, *, step):
    i = pl.program_id(2)
    nsub = step // QUERY_SUB
    n_heads = qa_ref.shape[1] // LANES
    chains = [(h, qs) for h in range(n_heads) for qs in range(nsub)]
    qa = [qa_ref[qs * QUERY_SUB:(qs + 1) * QUERY_SUB, h * LANES:(h + 1) * LANES]
          for h, qs in chains]
    step_slopes = [slope_ref[n_heads * pl.program_id(1) + h] * float(step) for h, _ in chains]

    def scores_fn(rows):
        return lambda c: _dot_nt(ka_ref[rows, chains[c][0] * LANES:(chains[c][0] + 1) * LANES], qa[c])

    def vt_fn(rows):
        per_pair = [_with_sum_rows(vt_ref[pp * LANES:(pp + 1) * LANES, rows])
                    for pp in range(n_heads // 2)]
        return tuple(per_pair[h // 2] for h, _ in chains)

    chan = lax.broadcasted_iota(jnp.int32, (LANES, QUERY_SUB), 0)

    def finish(res):
        for pp in range(n_heads // 2):
            for qs in range(nsub):
                (l0, acc0), (l1, acc1) = res[2 * pp * nsub + qs], res[(2 * pp + 1) * nsub + qs]
                o_ref[qs * QUERY_SUB:(qs + 1) * QUERY_SUB, pp * LANES:(pp + 1) * LANES] = (
                    jnp.where(chan < HEAD_DIM, acc0 / l0, acc1 / l1).T.astype(o_ref.dtype))

    _causal_key_steps(scores_fn, vt_fn, finish, step_slopes, i, step)


def _moba_mixer(qkvm, vt, kmean, *, batch, seq, step=512):
    nblk = seq // MOBA_BLOCK
    assert nblk <= _SEL_LANES and step % MOBA_BLOCK == 0 and seq % step == 0
    blocks_per_step = step // MOBA_BLOCK
    npair = MOBA_HEADS // 2
    x = qkvm.reshape(batch, seq, QKVM_WIDTH)
    km = kmean.reshape(batch, nblk, MIX_WIDTH)
    slopes = _log2_slopes(MOBA_HEADS)
    q_const = np.zeros((npair, 8, LANES), np.float32)
    for p in range(npair):
        for hh in range(2):
            a0 = (1 - hh) * HEAD_DIM + _SEL_LANES
            pieces = _bf16_pieces(slopes[2 * p + hh], _ALIBI_PIECES)
            q_const[p, hh, a0:a0 + _ALIBI_PIECES] = pieces
            q_const[p, hh, a0 + _ALIBI_PIECES:a0 + 2 * _ALIBI_PIECES] = [
                v * MOBA_BLOCK for v in pieces]
    tq = min(seq, MOBA_PREP_ROWS)
    kcol = MIX_WIDTH // LANES
    qa, ka = pl.pallas_call(
        functools.partial(_moba_prep_kernel, nblk=nblk, blocks_per_step=blocks_per_step),
        grid=(batch, npair, seq // tq),
        in_specs=[pl.BlockSpec((None, 8, LANES), lambda b, p, i: (p, 0, 0)),
                  pl.BlockSpec((None, tq, LANES), lambda b, p, i: (b, i, p)),
                  pl.BlockSpec((None, tq, LANES), lambda b, p, i: (b, i, kcol + p)),
                  pl.BlockSpec((None, nblk, LANES), lambda b, p, i: (b, 0, p))],
        out_specs=[pl.BlockSpec((None, tq, 2 * LANES), lambda b, p, i: (b, i, p))] * 2,
        out_shape=[jax.ShapeDtypeStruct((batch, seq, MOBA_HEADS * LANES), BF16)] * 2,
        compiler_params=_compiler_params(3),
        name="moba_prep",
    )(jnp.asarray(q_const), x, x, km)
    ppc = MOBA_PAIRS_PER_CALL
    out = pl.pallas_call(
        functools.partial(_moba_kernel, step=step),
        grid=(batch, npair // ppc, seq // step),
        in_specs=[pl.BlockSpec(memory_space=pltpu.SMEM),
                  pl.BlockSpec((None, step, 2 * ppc * LANES), lambda b, p, i: (b, i, p)),
                  pl.BlockSpec((None, seq, 2 * ppc * LANES), lambda b, p, i: (b, 0, p)),
                  pl.BlockSpec((ppc * LANES, seq), lambda b, p, i: (p, b))],
        out_specs=pl.BlockSpec((None, step, ppc * LANES), lambda b, p, i: (b, i, p)),
        out_shape=jax.ShapeDtypeStruct((batch, seq, MIX_WIDTH), BF16),
        compiler_params=_compiler_params(3),
        name="moba_attn",
    )(jnp.asarray(slopes), qa, ka, vt)
    return out.reshape(batch * seq, MIX_WIDTH)


def _extras_lane(mi):
    return (1 - mi) * DIFF_D


def _position_lanes(step):
    tab = np.zeros((step, 2 * LANES), np.float32)
    r = np.arange(step)
    for mi in range(2):
        a0 = mi * LANES + _extras_lane(mi)
        tab[:, a0:a0 + _ALIBI_PIECES] = (r % MOBA_BLOCK)[:, None]
        tab[:, a0 + _ALIBI_PIECES:a0 + 2 * _ALIBI_PIECES] = (r // MOBA_BLOCK)[:, None]
    return jnp.asarray(tab, dtype=BF16)


def _diff_kernel(slope_ref, qx_ref, kx_ref, lam_ref, g_ref, q_ref, k_ref, vt_ref, o_ref, *,
                 step, lambda_init):
    i = pl.program_id(2)
    nsub = step // QUERY_SUB
    n_heads = q_ref.shape[1] // LANES
    lane_q = lax.broadcasted_iota(jnp.int32, (QUERY_SUB, LANES), 1)
    chains = [(h, mi, qs) for h in range(n_heads) for mi in range(2) for qs in range(nsub)]
    qa = []
    for h, mi, qs in chains:
        q2 = q_ref[qs * QUERY_SUB:(qs + 1) * QUERY_SUB, h * LANES:(h + 1) * LANES]
        qx = jnp.broadcast_to(qx_ref[h, mi:mi + 1, :], (QUERY_SUB, LANES)).astype(BF16)
        own_map = (lane_q >= mi * DIFF_D) & (lane_q < (mi + 1) * DIFF_D)
        qa.append(jnp.where(own_map, q2, qx))
    step_slopes = [slope_ref[n_heads * pl.program_id(1) + h] * float(step) for h, _, _ in chains]

    def scores_fn(rows):
        ka, n = {}, rows.size
        lane_k = lax.broadcasted_iota(jnp.int32, (n, LANES), 1)
        for h in range(n_heads):
            kb = k_ref[rows, h * LANES:(h + 1) * LANES]
            for mi in range(2):
                own_k = (lane_k >= mi * DIFF_D) & (lane_k < (mi + 1) * DIFF_D)
                ka[h, mi] = jnp.where(own_k, kb, kx_ref[0:n, mi * LANES:(mi + 1) * LANES])
        return lambda c: _dot_nt(ka[chains[c][0], chains[c][1]], qa[c])

    def vt_fn(rows):
        per_head = [_with_sum_rows(vt_ref[h * LANES:(h + 1) * LANES, rows]) for h in range(n_heads)]
        return tuple(per_head[h] for h, _, _ in chains)

    def finish(res):
        lq1, lk1, lq2, lk2 = (lam_ref[r:r + 1, :] for r in range(4))
        lam = (jnp.exp(jnp.sum(lq1 * lk1, axis=1, keepdims=True))
               - jnp.exp(jnp.sum(lq2 * lk2, axis=1, keepdims=True)) + lambda_init)
        for h in range(n_heads):
            for qs in range(nsub):
                (l1, acc1), (l2, acc2) = res[(2 * h) * nsub + qs], res[(2 * h + 1) * nsub + qs]
                o = acc1 / l1 - lam * (acc2 / l2)
                y = o * lax.rsqrt(jnp.mean(o * o, axis=0, keepdims=True) + SUBLN_EPS)
                o_ref[qs * QUERY_SUB:(qs + 1) * QUERY_SUB, h * LANES:(h + 1) * LANES] = (
                    (y * g_ref[...]) * (1.0 - lambda_init)).T.astype(o_ref.dtype)

    _causal_key_steps(scores_fn, vt_fn, finish, step_slopes, i, step)


def _diff_mixer(qkvm, vt, lam_vecs, subln_g, lambda_init, *, batch, seq, step=512):
    assert step % MOBA_BLOCK == 0 and seq % step == 0
    x = qkvm.reshape(batch, seq, QKVM_WIDTH)
    slopes = _log2_slopes(DIFF_HEADS)
    q_extra = np.zeros((DIFF_HEADS, 8, LANES), np.float32)
    for h in range(DIFF_HEADS):
        pieces = _bf16_pieces(slopes[h], _ALIBI_PIECES)
        for mi in range(2):
            a0 = _extras_lane(mi)
            q_extra[h, mi, a0:a0 + _ALIBI_PIECES] = pieces
            q_extra[h, mi, a0 + _ALIBI_PIECES:a0 + 2 * _ALIBI_PIECES] = [
                v * MOBA_BLOCK for v in pieces]
    kcol = MIX_WIDTH // LANES
    hpc = DIFF_HEADS_PER_CALL
    assert DIFF_HEADS % hpc == 0 and kcol % hpc == 0
    out = pl.pallas_call(
        functools.partial(_diff_kernel, step=step, lambda_init=float(lambda_init)),
        grid=(batch, DIFF_HEADS // hpc, seq // step),
        in_specs=[pl.BlockSpec(memory_space=pltpu.SMEM),
                  pl.BlockSpec((hpc, 8, LANES), lambda b, h, i: (h, 0, 0)),
                  pl.BlockSpec((step, 2 * LANES), lambda b, h, i: (0, 0)),
                  pl.BlockSpec((4, DIFF_D), lambda b, h, i: (0, 0)),
                  pl.BlockSpec((2 * DIFF_D, 1), lambda b, h, i: (0, 0)),
                  pl.BlockSpec((None, step, hpc * LANES), lambda b, h, i: (b, i, h)),
                  pl.BlockSpec((None, seq, hpc * LANES), lambda b, h, i: (b, 0, kcol // hpc + h)),
                  pl.BlockSpec((hpc * LANES, seq), lambda b, h, i: (h, b))],
        out_specs=pl.BlockSpec((None, step, hpc * LANES), lambda b, h, i: (b, i, h)),
        out_shape=jax.ShapeDtypeStruct((batch, seq, MIX_WIDTH), BF16),
        compiler_params=_compiler_params(3),
        name="diff_attn",
    )(jnp.asarray(slopes), jnp.asarray(q_extra), _position_lanes(step), lam_vecs,
      subln_g.reshape(2 * DIFF_D, 1), x, x, vt)
    return out.reshape(batch * seq, MIX_WIDTH)


def _post_kernel(*refs, n_mix, dilated, final):
    mix_refs = refs[:n_mix]
    x_ref, qm_ref, gate_ref, kv_ref, vmt_ref, w_ref = refs[n_mix:n_mix + 6]
    rest = refs[n_mix + 6:]
    fg_ref = rest[0] if final else None
    o_ref = rest[-1]
    tm = x_ref.shape[0]

    if dilated:
        outs = [r[...] for r in mix_refs[:3]]
        lses = [r[...] for r in mix_refs[3:]]
        mx = jnp.maximum(jnp.maximum(lses[0], lses[1]), lses[2])
        es = [jnp.exp(v - mx) for v in lses]
        den = es[0] + es[1] + es[2]
        mix_parts = [o * (e / den) for o, e in zip(outs, es)]
    else:
        mix = mix_refs[0][...].astype(F32)
        mix_parts = [mix[:, g * DSW_GROUP_WIDTH:(g + 1) * DSW_GROUP_WIDTH] for g in range(3)]

    lane = lax.broadcasted_iota(jnp.int32, (tm, LANES), 1)
    chan = lax.broadcasted_iota(jnp.int32, (LANES, tm), 0)
    n_pair = MEM_WIDTH // LANES
    scores = []
    for pair in range(n_pair):
        cs = slice(pair * LANES, (pair + 1) * LANES)
        q2 = qm_ref[:, cs].astype(BF16)
        for hh in range(2):
            own = (lane >= hh * HEAD_DIM) & (lane < (hh + 1) * HEAD_DIM)
            scores.append(_dot_nt(kv_ref[:, cs], jnp.where(own, q2, jnp.zeros_like(q2))))
    mem_parts = []
    for pair in range(n_pair):
        vt1 = _with_sum_rows(vmt_ref[pair * LANES:(pair + 1) * LANES, :])
        o_h = []
        for hh in range(2):
            st = scores[2 * pair + hh]
            p = jnp.exp2((st - jnp.max(st, axis=0, keepdims=True)).astype(BF16))
            acc = _dot(vt1, p)
            o_h.append(acc[:LANES] / acc[LANES:LANES + 1])
        mem_parts.append(jnp.where(chan < HEAD_DIM, o_h[0], o_h[1]).T)

    parts = [(g * DSW_GROUP_WIDTH, mix_parts[g]) for g in range(3)]
    parts += [(MIX_WIDTH + p * LANES, mem_parts[p]) for p in range(len(mem_parts))]
    ys = []
    for c0, val in parts:
        gt = gate_ref[:, c0:c0 + val.shape[1]].astype(F32)
        ys.append((val * (gt * (1.0 / (1.0 + jnp.exp(-gt))))).astype(BF16))
    acc = x_ref[...] + _dot(jnp.concatenate(ys, axis=1), w_ref[...].astype(BF16))
    if final:
        yn = acc * lax.rsqrt(jnp.mean(acc * acc, axis=-1, keepdims=True) + RMS_EPS)
        acc = yn * fg_ref[...]
    o_ref[...] = acc


def _post(mix_list, x, qkvm, gate, mem_kv, mem_vt, w_out, layer, final_g, *, dilated, batch, seq,
          tm=1024):
    T = x.shape[0]
    tiles_per_batch = seq // tm
    n_mix = len(mix_list)
    final = final_g is not None
    qm_block = (3 * MIX_WIDTH) // MEM_WIDTH
    in_specs = [pl.BlockSpec((tm, m.shape[1]), lambda i: (i, 0)) for m in mix_list]
    in_specs += [pl.BlockSpec((tm, D_MODEL), lambda i: (i, 0)),
                 pl.BlockSpec((tm, MEM_WIDTH), lambda i: (i, qm_block)),
                 pl.BlockSpec((tm, BRANCH), lambda i: (i, 0)),
                 pl.BlockSpec((None, N_MEM, 2 * MEM_WIDTH), lambda i: (i // tiles_per_batch, 0, 0)),
                 pl.BlockSpec((MEM_WIDTH, N_MEM), lambda i: (0, i // tiles_per_batch)),
                 pl.BlockSpec((None, BRANCH, D_MODEL), lambda i: (layer, 0, 0))]
    args = list(mix_list) + [x, qkvm, gate, mem_kv, mem_vt, w_out]
    if final:
        in_specs.append(pl.BlockSpec((1, D_MODEL), lambda i: (0, 0)))
        args.append(final_g)
    return pl.pallas_call(
        functools.partial(_post_kernel, n_mix=n_mix, dilated=dilated, final=final),
        grid=(T // tm,),
        in_specs=in_specs,
        out_specs=pl.BlockSpec((tm, D_MODEL), lambda i: (i, 0)),
        out_shape=jax.ShapeDtypeStruct((T, D_MODEL), F32),
        compiler_params=_compiler_params(1),
        name="post",
    )(*args)


def kernel(x, mem, norm_g, w_in, w_out, mem_norm_g, w_mem_kv, diff_lambda_q1, diff_lambda_k1,
           diff_lambda_q2, diff_lambda_k2, diff_subln_g, final_norm_g):
    batch, seq, d = x.shape
    depth = w_in.shape[0]
    T = batch * seq
    xf = x.reshape(T, d)
    memf = mem.reshape(batch * mem.shape[1], d)

    norm_g3 = norm_g.reshape(depth, 1, d)
    mem_norm_g3 = mem_norm_g.reshape(depth, 1, d)
    scale = HEAD_DIM ** -0.5
    v_cols = (2 * MIX_WIDTH, 3 * MIX_WIDTH)

    for i in range(depth):
        kind = i % N_MIXERS
        q_scale = scale if kind == 0 else scale * LOG2E
        scales = [(0, MIX_WIDTH, q_scale), (3 * MIX_WIDTH, QKVM_WIDTH, scale * LOG2E)]
        kmean_cols = (MIX_WIDTH, 2 * MIX_WIDTH) if kind == 1 else None
        proj = _rms_proj(xf, norm_g3, w_in, i, [(0, QKVM_WIDTH), (QKVM_WIDTH, IN_WIDTH)],
                         [F32 if kind == 0 else BF16, BF16], tm=1024, scales=scales,
                         vt_cols=None if kind == 0 else v_cols, kmean_cols=kmean_cols)
        qkvm, gate = proj[0], proj[1]
        mem_kv, mem_vt = _rms_proj(memf, mem_norm_g3, w_mem_kv, i, [(0, 2 * MEM_WIDTH)], [BF16],
                                   tm=N_MEM,
                                   vt_cols=(MEM_WIDTH, 2 * MEM_WIDTH))
        mem_kv = mem_kv.reshape(batch, N_MEM, 2 * MEM_WIDTH)
        if kind == 0:
            res = [_dilated_group(qkvm, g, batch=batch, seq=seq) for g in range(len(DSW_GROUPS))]
            mix_list = [r[0] for r in res] + [r[1] for r in res]
        elif kind == 1:
            mix_list = [_moba_mixer(qkvm, proj[2], proj[3], batch=batch, seq=seq)]
        else:
            c = i // N_MIXERS
            lambda_init = 0.8 - 0.6 * math.exp(-0.3 * i)
            lam_vecs = jnp.stack([diff_lambda_q1[c], diff_lambda_k1[c],
                                  diff_lambda_q2[c], diff_lambda_k2[c]]).astype(F32)
            mix_list = [_diff_mixer(qkvm, proj[2], lam_vecs, diff_subln_g[c], lambda_init,
                                    batch=batch, seq=seq)]
        fg = final_norm_g.reshape(1, d) if i == depth - 1 else None
        xf = _post(mix_list, xf, qkvm, gate, mem_kv, mem_vt, w_out, i, fg,
                   dilated=(kind == 0), batch=batch, seq=seq)
    return xf.reshape(batch, seq, d)
```

```python
import functools
import math

import numpy as np
import jax
import jax.numpy as jnp
from jax import lax
from jax.experimental import pallas as pl
from jax.experimental.pallas import tpu as pltpu

D_MODEL = 1024
DEPTH = 4
N_MIXERS = 3
HEAD_DIM = 64
MIX_WIDTH = 768
N_MEM_HEADS = 4
MEM_WIDTH = N_MEM_HEADS * HEAD_DIM
N_MEM = 256
BRANCH = MIX_WIDTH + MEM_WIDTH
IN_WIDTH = 3 * MIX_WIDTH + MEM_WIDTH + BRANCH
QKVM_WIDTH = 3 * MIX_WIDTH + MEM_WIDTH
RMS_EPS = 1e-6
SUBLN_EPS = 1e-5
NEG_INF = -1e30
LOG2E = math.log2(math.e)

DSW_GROUPS = ((128, 1), (512, 4), (2048, 16))
DSW_HEADS = 12
DSW_GROUP_WIDTH = 256
BAND_BLOCK = 128

MOBA_HEADS = 12
MOBA_BLOCK = 256
MOBA_TOPK = 3

DIFF_HEADS = 6
DIFF_D = 64

LANES = 128
QUERY_SUB = 256
MOBA_PREP_ROWS = 4096
DILATED_BLOCK_ROWS = 4096
MOBA_PAIRS_PER_CALL = 3
DIFF_HEADS_PER_CALL = 3
VMEM_LIMIT_BYTES = 56 * 1024 * 1024

BF16 = jnp.bfloat16
F32 = jnp.float32


def _alibi_slopes(n):
    return np.asarray(2.0 ** (-8.0 * np.arange(1, n + 1) / n), dtype=np.float32)


def _log2_slopes(n):
    return np.asarray(_alibi_slopes(n).astype(np.float64) * LOG2E, dtype=np.float32)


def _dot(a, b):
    return jnp.dot(a, b, preferred_element_type=F32)


def _dot_nt(a, b):
    return lax.dot_general(a, b, (((1,), (1,)), ((), ())), preferred_element_type=F32)


def _dot_tn(a, b):
    return lax.dot_general(a, b, (((0,), (0,)), ((), ())), preferred_element_type=F32)


def _compiler_params(n_axes):
    return pltpu.CompilerParams(dimension_semantics=("parallel",) * n_axes,
                                vmem_limit_bytes=VMEM_LIMIT_BYTES)


def _rms_proj_kernel(x_ref, g_ref, w_ref, *out_refs, splits, n_chunk, scales, vt_cols, kmean_cols):
    x = x_ref[...].astype(F32)
    y = x * lax.rsqrt(jnp.mean(x * x, axis=-1, keepdims=True) + RMS_EPS)
    h = (y * g_ref[...]).astype(BF16)
    tm = x.shape[0]
    for (lo, hi), o_ref in zip(splits, out_refs):
        for c in range(lo, hi, n_chunk):
            acc = _dot(h, w_ref[:, c:c + n_chunk].astype(BF16))
            for s_lo, s_hi, factor in scales:
                if s_lo <= c < s_hi:
                    acc = acc * factor
            o_ref[:, c - lo:c - lo + n_chunk] = acc.astype(o_ref.dtype)
            if vt_cols is not None and vt_cols[0] <= c < vt_cols[1]:
                vt_ref = out_refs[len(splits)]
                vt_ref[c - vt_cols[0]:c - vt_cols[0] + n_chunk, :] = acc.T.astype(BF16)
            if kmean_cols is not None and kmean_cols[0] <= c < kmean_cols[1]:
                km_ref = out_refs[-1]
                for r in range(tm // MOBA_BLOCK):
                    blk = acc[r * MOBA_BLOCK:(r + 1) * MOBA_BLOCK]
                    km_ref[r, :, c - kmean_cols[0]:c - kmean_cols[0] + n_chunk] = (
                        jnp.mean(blk, axis=0, keepdims=True))


def _rms_proj(x, g, w, layer, splits, dtypes, *, tm, n_chunk=256, scales=(), vt_cols=None,
              kmean_cols=None):
    T, D = x.shape
    N = w.shape[2]
    out_shape = [jax.ShapeDtypeStruct((T, hi - lo), dt) for (lo, hi), dt in zip(splits, dtypes)]
    out_specs = [pl.BlockSpec((tm, hi - lo), lambda i: (i, 0)) for (lo, hi) in splits]
    if vt_cols is not None:
        vw = vt_cols[1] - vt_cols[0]
        out_shape.append(jax.ShapeDtypeStruct((vw, T), BF16))
        out_specs.append(pl.BlockSpec((vw, tm), lambda i: (0, i)))
    if kmean_cols is not None:
        kw = kmean_cols[1] - kmean_cols[0]
        out_shape.append(jax.ShapeDtypeStruct((T // MOBA_BLOCK, 1, kw), F32))
        out_specs.append(pl.BlockSpec((tm // MOBA_BLOCK, 1, kw), lambda i: (i, 0, 0)))
    kern = functools.partial(_rms_proj_kernel, splits=tuple(splits), n_chunk=n_chunk,
                             scales=tuple(scales), vt_cols=vt_cols, kmean_cols=kmean_cols)
    return pl.pallas_call(
        kern,
        grid=(T // tm,),
        in_specs=[pl.BlockSpec((tm, D), lambda i: (i, 0)),
                  pl.BlockSpec((None, 1, D), lambda i: (layer, 0, 0)),
                  pl.BlockSpec((None, D, N), lambda i: (layer, 0, 0),
                               pipeline_mode=pl.Buffered(1))],
        out_specs=out_specs,
        out_shape=out_shape,
        compiler_params=_compiler_params(1),
        name="rms_proj",
    )(x, g, w)


def _mem_proj_all(mem, g, w):
    Tm, D = mem.shape
    L, _, N = w.shape
    kern = functools.partial(_rms_proj_kernel, splits=((0, N),), n_chunk=256, scales=(),
                             vt_cols=(MEM_WIDTH, N), kmean_cols=None)
    return pl.pallas_call(
        kern,
        grid=(L,),
        in_specs=[pl.BlockSpec((Tm, D), lambda l: (0, 0)),
                  pl.BlockSpec((None, 1, D), lambda l: (l, 0, 0)),
                  pl.BlockSpec((None, D, N), lambda l: (l, 0, 0))],
        out_specs=[pl.BlockSpec((None, Tm, N), lambda l: (l, 0, 0)),
                   pl.BlockSpec((None, N - MEM_WIDTH, Tm), lambda l: (l, 0, 0))],
        out_shape=[jax.ShapeDtypeStruct((L, Tm, N), BF16),
                   jax.ShapeDtypeStruct((L, N - MEM_WIDTH, Tm), BF16)],
        compiler_params=_compiler_params(1),
        name="mem_proj",
    )(mem, g, w)


def _dilated_kernel(q_ref, kc_ref, kp_ref, vc_ref, vp_ref, o_ref, lse_ref, *, tu, dil, slopes):
    ui = pl.program_id(1)
    pair = pl.program_id(2)
    bb = BAND_BLOCK
    key = lax.broadcasted_iota(jnp.int32, (bb, bb), 0)
    qry = lax.broadcasted_iota(jnp.int32, (bb, bb), 1)
    d_cur = (qry - key).astype(F32)
    d_prev = d_cur + float(bb)
    valid_cur = qry >= key
    valid_prev_static = key >= qry
    lane = lax.broadcasted_iota(jnp.int32, (bb, LANES), 1)
    chan = lax.broadcasted_iota(jnp.int32, (LANES, bb), 0)
    head_slopes = [jnp.where(pair == 0, slopes[hh], slopes[2 + hh]) for hh in range(2)]

    def rows_of(blk, r):
        return pl.ds(blk * bb * dil + r, bb, stride=dil)

    def steps(ref, blk, r):
        return ref[rows_of(blk, r), :].astype(BF16)

    work = []
    for r in range(dil):
        for s in range(tu // bb):
            qs = steps(q_ref, s, r)
            v_cur = steps(vc_ref, s, r)
            if s == 0:
                k_prev, v_prev = steps(kp_ref, 0, r), steps(vp_ref, 0, r)
                valid_prev = jnp.logical_and(valid_prev_static, ui > 0)
            else:
                k_prev, v_prev = steps(kc_ref, s - 1, r), steps(vc_ref, s - 1, r)
                valid_prev = valid_prev_static
            k_cur = steps(kc_ref, s, r)
            scores = []
            for hh in range(2):
                own = (lane >= hh * HEAD_DIM) & (lane < (hh + 1) * HEAD_DIM)
                qh = jnp.where(own, qs, jnp.zeros_like(qs))
                scores.append((_dot_nt(k_cur, qh), _dot_nt(k_prev, qh)))
            work.append((r, s, v_cur, v_prev, valid_prev, scores))

    for r, s, v_cur, v_prev, valid_prev, scores in work:
        o_h, lse_h = [], []
        for hh in range(2):
            s_cur = jnp.where(valid_cur, scores[hh][0] - head_slopes[hh] * d_cur, NEG_INF)
            s_prev = jnp.where(valid_prev, scores[hh][1] - head_slopes[hh] * d_prev, NEG_INF)
            m = jnp.maximum(jnp.max(s_cur, axis=0, keepdims=True),
                            jnp.max(s_prev, axis=0, keepdims=True))
            p_cur = jnp.exp(s_cur - m)
            p_prev = jnp.exp(s_prev - m)
            l = jnp.sum(p_cur, axis=0, keepdims=True) + jnp.sum(p_prev, axis=0, keepdims=True)
            acc = _dot_tn(v_cur, p_cur.astype(BF16)) + _dot_tn(v_prev, p_prev.astype(BF16))
            o_h.append(acc / l)
            lse_h.append(jnp.broadcast_to(m + jnp.log(l), (LANES, bb)))
        first = chan < HEAD_DIM
        o_ref[rows_of(s, r), :] = jnp.where(first, o_h[0], o_h[1]).T
        lse_ref[rows_of(s, r), :] = jnp.where(first, lse_h[0], lse_h[1]).T


def _dilated_group(qkvm, g, *, batch, seq):
    window, dil = DSW_GROUPS[g]
    assert window // dil == BAND_BLOCK
    U = seq // dil
    tu = min(U, DILATED_BLOCK_ROWS // dil)
    gw = DSW_GROUP_WIDTH
    tiles = gw // LANES
    kcol = MIX_WIDTH // LANES
    x = qkvm.reshape(batch, seq, QKVM_WIDTH)
    sub = tu // BAND_BLOCK
    slopes = tuple(float(v) for v in _alibi_slopes(DSW_HEADS)[g * 4:(g + 1) * 4] * np.float32(dil))

    def cur(off):
        return pl.BlockSpec((None, tu * dil, LANES), lambda b, u, p: (b, u, off + g * tiles + p))

    def prev(off):
        return pl.BlockSpec((None, BAND_BLOCK * dil, LANES),
                            lambda b, u, p: (b, jnp.maximum(u * sub - 1, 0), off + g * tiles + p))

    out_spec = pl.BlockSpec((None, tu * dil, LANES), lambda b, u, p: (b, u, p))
    o, lse = pl.pallas_call(
        functools.partial(_dilated_kernel, tu=tu, dil=dil, slopes=slopes),
        grid=(batch, U // tu, tiles),
        in_specs=[cur(0), cur(kcol), prev(kcol), cur(2 * kcol), prev(2 * kcol)],
        out_specs=[out_spec, out_spec],
        out_shape=[jax.ShapeDtypeStruct((batch, seq, gw), F32)] * 2,
        compiler_params=_compiler_params(3),
        name=f"dilated_g{g}",
    )(x, x, x, x, x)
    return o.reshape(batch * seq, gw), lse.reshape(batch * seq, gw)


_SEL_LANES = 32
_ALIBI_PIECES = 3


def _bf16_pieces(v, n):
    out, rest = [], np.float32(v)
    for _ in range(n):
        piece = np.asarray(rest, np.float32).astype(jnp.bfloat16).astype(np.float32)
        out.append(float(piece))
        rest = np.float32(rest - piece)
    assert rest == 0.0, "slope does not split exactly into bf16 pieces"
    return out


def _moba_prep_kernel(const_ref, q_ref, k_ref, km_ref, qa_ref, ka_ref, *, nblk, blocks_per_step):
    tq = MOBA_BLOCK
    nsb = q_ref.shape[0] // tq
    km = km_ref[...]
    lane = lax.broadcasted_iota(jnp.int32, (tq, LANES), 1)
    rowi = lax.broadcasted_iota(jnp.int32, (tq, LANES), 0)
    km_lane = lax.broadcasted_iota(jnp.int32, (nblk, LANES), 1)
    km_row = lax.broadcasted_iota(jnp.int32, (nblk, LANES), 0)
    cand_blk = lax.broadcasted_iota(jnp.int32, (nblk, tq), 0)
    cand_blk_f = cand_blk.astype(F32)
    chains = [(sb, hh) for sb in range(nsb) for hh in range(2)]
    rows = [slice(sb * tq, (sb + 1) * tq) for sb in range(nsb)]
    own_blk = [pl.program_id(2) * nsb + sb for sb in range(nsb)]
    own = [(lane >= hh * HEAD_DIM) & (lane < (hh + 1) * HEAD_DIM) for hh in range(2)]
    off = [(1 - hh) * HEAD_DIM for hh in range(2)]
    km_h = [jnp.where((km_lane >= hh * HEAD_DIM) & (km_lane < (hh + 1) * HEAD_DIM), km, 0.0)
            .astype(BF16) for hh in range(2)]

    gates = []
    for sb, hh in chains:
        q2 = q_ref[rows[sb], :]
        gates.append(_dot_nt(km_h[hh], jnp.where(own[hh], q2, jnp.zeros_like(q2))))

    picked = []
    for c, (sb, hh) in enumerate(chains):
        g = jnp.where(cand_blk < own_blk[sb], gates[c], -jnp.inf)
        sel = jnp.zeros((nblk, tq), jnp.bool_)
        for _ in range(MOBA_TOPK):
            mx = jnp.max(g, axis=0, keepdims=True)
            hit = (g == mx) & (mx > -jnp.inf)
            idx = jnp.min(jnp.where(hit, cand_blk_f, float(nblk)), axis=0, keepdims=True)
            pick = cand_blk_f == idx
            sel = sel | pick
            g = jnp.where(pick, -jnp.inf, g)
        picked.append(jnp.where(sel, 1.0, 0.0).astype(BF16))

    place = [jnp.where(km_lane == km_row + off[hh], 1.0, 0.0).astype(BF16) for hh in range(2)]
    placed = [_dot_tn(picked[c], place[hh]) for c, (sb, hh) in enumerate(chains)]

    for c, (sb, hh) in enumerate(chains):
        n = own_blk[sb]
        blk = lane - off[hh]
        is_sel_lane = (blk >= 0) & (blk < _SEL_LANES)
        sel_val = jnp.where((placed[c] > 0.5) | (blk == n), 0.0, NEG_INF)
        q_extra = jnp.where(is_sel_lane, sel_val, const_ref[hh:hh + 1, :])
        qa = jnp.where(own[hh], q_ref[rows[sb], :].astype(F32), q_extra)
        a0 = off[hh] + _SEL_LANES
        is_fine = (lane >= a0) & (lane < a0 + _ALIBI_PIECES)
        is_coarse = (lane >= a0 + _ALIBI_PIECES) & (lane < a0 + 2 * _ALIBI_PIECES)
        coarse = (n % blocks_per_step).astype(F32)
        k_extra = jnp.where(blk == n, 1.0,
                            jnp.where(is_fine, rowi.astype(F32), jnp.where(is_coarse, coarse, 0.0)))
        ka = jnp.where(own[hh], k_ref[rows[sb], :].astype(F32), k_extra)
        qa_ref[rows[sb], hh * LANES:(hh + 1) * LANES] = qa.astype(BF16)
        ka_ref[rows[sb], hh * LANES:(hh + 1) * LANES] = ka.astype(BF16)


SUM_ROWS = 16


def _softmax_step(st, vt1, carry, shift):
    m, acc = carry
    m_new = jnp.maximum(m, jnp.max(st, axis=0, keepdims=True) + shift)
    p = jnp.exp2((st - (m_new - shift)).astype(BF16))
    acc = jnp.exp2(m - m_new) * acc + _dot(vt1, p)
    return m_new, acc


STALE_MAX_CAP = 64.0


def _stale_max_step(st, vt1, carry, shift):
    m, acc, bad = carry
    p = jnp.exp2((st - (m - shift)).astype(BF16))
    cmax = jnp.max(st, axis=0, keepdims=True) + shift
    m_new = jnp.maximum(m, cmax)
    bad = jnp.maximum(bad, jnp.where(cmax - m > STALE_MAX_CAP, 1.0, 0.0))
    acc = (acc + _dot(vt1, p)) * jnp.exp2(m - m_new)
    return m_new, acc, bad


SCORE_LOOKAHEAD = {_softmax_step: 6, _stale_max_step: 3}


def _key_step(step_fn, score_fn, vts, carry, shifts, masks=None):
    n_chain = len(carry)
    lookahead = SCORE_LOOKAHEAD[step_fn]
    sts, new = {}, [None] * n_chain
    for t in range(n_chain + lookahead):
        if t < n_chain:
            sts[t] = score_fn(t)
        c = t - lookahead
        if c >= 0:
            st = sts.pop(c)
            if masks is not None:
                st = jnp.where(masks[c], st, NEG_INF)
            new[c] = step_fn(st, vts[c], carry[c], shifts[c])
    return tuple(new)


def _causal_key_steps(scores_fn, vt_fn, finish, step_slopes, i, step):
    nsub = step // QUERY_SUB
    n_chain = len(step_slopes)

    def rows_of(j):
        return pl.ds(pl.multiple_of(j * step, step), step)

    def shifts_of(j):
        dj = (j - i).astype(F32)
        return [s * dj for s in step_slopes]

    own_rows = [pl.ds(pl.multiple_of(i * step, step), (qs + 1) * QUERY_SUB) for qs in range(nsub)]
    own_scores = [scores_fn(r) for r in own_rows]
    own_vts = [vt_fn(r) for r in own_rows]
    masks = []
    for qs in range(nsub):
        key = lax.broadcasted_iota(jnp.int32, ((qs + 1) * QUERY_SUB, QUERY_SUB), 0)
        qry = lax.broadcasted_iota(jnp.int32, ((qs + 1) * QUERY_SUB, QUERY_SUB), 1)
        masks.append(qry + qs * QUERY_SUB >= key)
    init = tuple((jnp.full((1, QUERY_SUB), NEG_INF, F32),
                  jnp.zeros((LANES + SUM_ROWS, QUERY_SUB), F32)) for _ in range(n_chain))
    own = _key_step(_softmax_step, lambda c: own_scores[c % nsub](c),
                    [own_vts[c % nsub][c] for c in range(n_chain)], init, [0.0] * n_chain,
                    [masks[c % nsub] for c in range(n_chain)])

    def fast_body(j, carry):
        return _key_step(_stale_max_step, scores_fn(rows_of(j)), vt_fn(rows_of(j)), carry,
                         shifts_of(j))

    fast = lax.fori_loop(0, i, fast_body,
                         tuple((m, acc, jnp.zeros((1, QUERY_SUB), F32)) for m, acc in own))
    overflowed = functools.reduce(jnp.maximum, [jnp.max(bad) for _, _, bad in fast]) > 0.0

    def exact_body(j, carry):
        return _key_step(_softmax_step, scores_fn(rows_of(j)), vt_fn(rows_of(j)), carry,
                         shifts_of(j))

    def split(acc):
        return acc[LANES:LANES + 1], acc[:LANES]

    finish([split(acc) for _, acc, _ in fast])

    @pl.when(overflowed)
    def _():
        finish([split(acc) for _, acc in lax.fori_loop(0, i, exact_body, own)])


def _with_sum_rows(vt):
    return jnp.concatenate([vt, jnp.ones((SUM_ROWS, vt.shape[1]), vt.dtype)], axis=0)


def _moba_kernel(slope_ref, qa_ref, ka_ref, vt_ref, o_ref, *, step):
    i = pl.program_id(2)
    nsub = step // QUERY_SUB
    n_heads = qa_ref.shape[1] // LANES
    chains = [(h, qs) for h in range(n_heads) for qs in range(nsub)]
    qa = [qa_ref[qs * QUERY_SUB:(qs + 1) * QUERY_SUB, h * LANES:(h + 1) * LANES]
          for h, qs in chains]
    step_slopes = [slope_ref[n_heads * pl.program_id(1) + h] * float(step) for h, _ in chains]

    def scores_fn(rows):
        return lambda c: _dot_nt(ka_ref[rows, chains[c][0] * LANES:(chains[c][0] + 1) * LANES], qa[c])

    def vt_fn(rows):
        per_pair = [_with_sum_rows(vt_ref[pp * LANES:(pp + 1) * LANES, rows])
                    for pp in range(n_heads // 2)]
        return tuple(per_pair[h // 2] for h, _ in chains)

    chan = lax.broadcasted_iota(jnp.int32, (LANES, QUERY_SUB), 0)

    def finish(res):
        for pp in range(n_heads // 2):
            for qs in range(nsub):
                (l0, acc0), (l1, acc1) = res[2 * pp * nsub + qs], res[(2 * pp + 1) * nsub + qs]
                o_ref[qs * QUERY_SUB:(qs + 1) * QUERY_SUB, pp * LANES:(pp + 1) * LANES] = (
                    jnp.where(chan < HEAD_DIM, acc0 / l0, acc1 / l1).T.astype(o_ref.dtype))

    _causal_key_steps(scores_fn, vt_fn, finish, step_slopes, i, step)


def _moba_mixer(qkvm, vt, kmean, *, batch, seq, step=512):
    nblk = seq // MOBA_BLOCK
    assert nblk <= _SEL_LANES and step % MOBA_BLOCK == 0 and seq % step == 0
    blocks_per_step = step // MOBA_BLOCK
    npair = MOBA_HEADS // 2
    x = qkvm.reshape(batch, seq, QKVM_WIDTH)
    km = kmean.reshape(batch, nblk, MIX_WIDTH)
    slopes = _log2_slopes(MOBA_HEADS)
    q_const = np.zeros((npair, 8, LANES), np.float32)
    for p in range(npair):
        for hh in range(2):
            a0 = (1 - hh) * HEAD_DIM + _SEL_LANES
            pieces = _bf16_pieces(slopes[2 * p + hh], _ALIBI_PIECES)
            q_const[p, hh, a0:a0 + _ALIBI_PIECES] = pieces
            q_const[p, hh, a0 + _ALIBI_PIECES:a0 + 2 * _ALIBI_PIECES] = [
                v * MOBA_BLOCK for v in pieces]
    tq = min(seq, MOBA_PREP_ROWS)
    kcol = MIX_WIDTH // LANES
    qa, ka = pl.pallas_call(
        functools.partial(_moba_prep_kernel, nblk=nblk, blocks_per_step=blocks_per_step),
        grid=(batch, npair, seq // tq),
        in_specs=[pl.BlockSpec((None, 8, LANES), lambda b, p, i: (p, 0, 0)),
                  pl.BlockSpec((None, tq, LANES), lambda b, p, i: (b, i, p)),
                  pl.BlockSpec((None, tq, LANES), lambda b, p, i: (b, i, kcol + p)),
                  pl.BlockSpec((None, nblk, LANES), lambda b, p, i: (b, 0, p))],
        out_specs=[pl.BlockSpec((None, tq, 2 * LANES), lambda b, p, i: (b, i, p))] * 2,
        out_shape=[jax.ShapeDtypeStruct((batch, seq, MOBA_HEADS * LANES), BF16)] * 2,
        compiler_params=_compiler_params(3),
        name="moba_prep",
    )(jnp.asarray(q_const), x, x, km)
    ppc = MOBA_PAIRS_PER_CALL
    out = pl.pallas_call(
        functools.partial(_moba_kernel, step=step),
        grid=(batch, npair // ppc, seq // step),
        in_specs=[pl.BlockSpec(memory_space=pltpu.SMEM),
                  pl.BlockSpec((None, step, 2 * ppc * LANES), lambda b, p, i: (b, i, p)),
                  pl.BlockSpec((None, seq, 2 * ppc * LANES), lambda b, p, i: (b, 0, p)),
                  pl.BlockSpec((ppc * LANES, seq), lambda b, p, i: (p, b))],
        out_specs=pl.BlockSpec((None, step, ppc * LANES), lambda b, p, i: (b, i, p)),
        out_shape=jax.ShapeDtypeStruct((batch, seq, MIX_WIDTH), BF16),
        compiler_params=_compiler_params(3),
        name="moba_attn",
    )(jnp.asarray(slopes), qa, ka, vt)
    return out.reshape(batch * seq, MIX_WIDTH)


def _extras_lane(mi):
    return (1 - mi) * DIFF_D


def _position_lanes(step):
    tab = np.zeros((step, 2 * LANES), np.float32)
    r = np.arange(step)
    for mi in range(2):
        a0 = mi * LANES + _extras_lane(mi)
        tab[:, a0:a0 + _ALIBI_PIECES] = (r % MOBA_BLOCK)[:, None]
        tab[:, a0 + _ALIBI_PIECES:a0 + 2 * _ALIBI_PIECES] = (r // MOBA_BLOCK)[:, None]
    return jnp.asarray(tab, dtype=BF16)


def _diff_kernel(slope_ref, qx_ref, kx_ref, lam_ref, g_ref, q_ref, k_ref, vt_ref, o_ref, *,
                 step, lambda_init):
    i = pl.program_id(2)
    nsub = step // QUERY_SUB
    n_heads = q_ref.shape[1] // LANES
    lane_q = lax.broadcasted_iota(jnp.int32, (QUERY_SUB, LANES), 1)
    chains = [(h, mi, qs) for h in range(n_heads) for mi in range(2) for qs in range(nsub)]
    qa = []
    for h, mi, qs in chains:
        q2 = q_ref[qs * QUERY_SUB:(qs + 1) * QUERY_SUB, h * LANES:(h + 1) * LANES]
        qx = jnp.broadcast_to(qx_ref[h, mi:mi + 1, :], (QUERY_SUB, LANES)).astype(BF16)
        own_map = (lane_q >= mi * DIFF_D) & (lane_q < (mi + 1) * DIFF_D)
        qa.append(jnp.where(own_map, q2, qx))
    step_slopes = [slope_ref[n_heads * pl.program_id(1) + h] * float(step) for h, _, _ in chains]

    def scores_fn(rows):
        ka, n = {}, rows.size
        lane_k = lax.broadcasted_iota(jnp.int32, (n, LANES), 1)
        for h in range(n_heads):
            kb = k_ref[rows, h * LANES:(h + 1) * LANES]
            for mi in range(2):
                own_k = (lane_k >= mi * DIFF_D) & (lane_k < (mi + 1) * DIFF_D)
                ka[h, mi] = jnp.where(own_k, kb, kx_ref[0:n, mi * LANES:(mi + 1) * LANES])
        return lambda c: _dot_nt(ka[chains[c][0], chains[c][1]], qa[c])

    def vt_fn(rows):
        per_head = [_with_sum_rows(vt_ref[h * LANES:(h + 1) * LANES, rows]) for h in range(n_heads)]
        return tuple(per_head[h] for h, _, _ in chains)

    def finish(res):
        lq1, lk1, lq2, lk2 = (lam_ref[r:r + 1, :] for r in range(4))
        lam = (jnp.exp(jnp.sum(lq1 * lk1, axis=1, keepdims=True))
               - jnp.exp(jnp.sum(lq2 * lk2, axis=1, keepdims=True)) + lambda_init)
        for h in range(n_heads):
            for qs in range(nsub):
                (l1, acc1), (l2, acc2) = res[(2 * h) * nsub + qs], res[(2 * h + 1) * nsub + qs]
                o = acc1 / l1 - lam * (acc2 / l2)
                y = o * lax.rsqrt(jnp.mean(o * o, axis=0, keepdims=True) + SUBLN_EPS)
                o_ref[qs * QUERY_SUB:(qs + 1) * QUERY_SUB, h * LANES:(h + 1) * LANES] = (
                    (y * g_ref[...]) * (1.0 - lambda_init)).T.astype(o_ref.dtype)

    _causal_key_steps(scores_fn, vt_fn, finish, step_slopes, i, step)


def _diff_mixer(qkvm, vt, lam_vecs, subln_g, lambda_init, *, batch, seq, step=512):
    assert step % MOBA_BLOCK == 0 and seq % step == 0
    x = qkvm.reshape(batch, seq, QKVM_WIDTH)
    slopes = _log2_slopes(DIFF_HEADS)
    q_extra = np.zeros((DIFF_HEADS, 8, LANES), np.float32)
    for h in range(DIFF_HEADS):
        pieces = _bf16_pieces(slopes[h], _ALIBI_PIECES)
        for mi in range(2):
            a0 = _extras_lane(mi)
            q_extra[h, mi, a0:a0 + _ALIBI_PIECES] = pieces
            q_extra[h, mi, a0 + _ALIBI_PIECES:a0 + 2 * _ALIBI_PIECES] = [
                v * MOBA_BLOCK for v in pieces]
    kcol = MIX_WIDTH // LANES
    hpc = DIFF_HEADS_PER_CALL
    assert DIFF_HEADS % hpc == 0 and kcol % hpc == 0
    out = pl.pallas_call(
        functools.partial(_diff_kernel, step=step, lambda_init=float(lambda_init)),
        grid=(batch, DIFF_HEADS // hpc, seq // step),
        in_specs=[pl.BlockSpec(memory_space=pltpu.SMEM),
                  pl.BlockSpec((hpc, 8, LANES), lambda b, h, i: (h, 0, 0)),
                  pl.BlockSpec((step, 2 * LANES), lambda b, h, i: (0, 0)),
                  pl.BlockSpec((4, DIFF_D), lambda b, h, i: (0, 0)),
                  pl.BlockSpec((2 * DIFF_D, 1), lambda b, h, i: (0, 0)),
                  pl.BlockSpec((None, step, hpc * LANES), lambda b, h, i: (b, i, h)),
                  pl.BlockSpec((None, seq, hpc * LANES), lambda b, h, i: (b, 0, kcol // hpc + h)),
                  pl.BlockSpec((hpc * LANES, seq), lambda b, h, i: (h, b))],
        out_specs=pl.BlockSpec((None, step, hpc * LANES), lambda b, h, i: (b, i, h)),
        out_shape=jax.ShapeDtypeStruct((batch, seq, MIX_WIDTH), BF16),
        compiler_params=_compiler_params(3),
        name="diff_attn",
    )(jnp.asarray(slopes), jnp.asarray(q_extra), _position_lanes(step), lam_vecs,
      subln_g.reshape(2 * DIFF_D, 1), x, x, vt)
    return out.reshape(batch * seq, MIX_WIDTH)


def _post_kernel(*refs, n_mix, dilated, final):
    mix_refs = refs[:n_mix]
    x_ref, qm_ref, gate_ref, kv_ref, vmt_ref, w_ref = refs[n_mix:n_mix + 6]
    rest = refs[n_mix + 6:]
    fg_ref = rest[0] if final else None
    o_ref = rest[-1]
    tm = x_ref.shape[0]

    if dilated:
        outs = [r[...] for r in mix_refs[:3]]
        lses = [r[...] for r in mix_refs[3:]]
        mx = jnp.maximum(jnp.maximum(lses[0], lses[1]), lses[2])
        es = [jnp.exp(v - mx) for v in lses]
        den = es[0] + es[1] + es[2]
        mix_parts = [o * (e / den) for o, e in zip(outs, es)]
    else:
        mix = mix_refs[0][...].astype(F32)
        mix_parts = [mix[:, g * DSW_GROUP_WIDTH:(g + 1) * DSW_GROUP_WIDTH] for g in range(3)]

    lane = lax.broadcasted_iota(jnp.int32, (tm, LANES), 1)
    chan = lax.broadcasted_iota(jnp.int32, (LANES, tm), 0)
    n_pair = MEM_WIDTH // LANES
    scores = []
    for pair in range(n_pair):
        cs = slice(pair * LANES, (pair + 1) * LANES)
        q2 = qm_ref[:, cs].astype(BF16)
        for hh in range(2):
            own = (lane >= hh * HEAD_DIM) & (lane < (hh + 1) * HEAD_DIM)
            scores.append(_dot_nt(kv_ref[:, cs], jnp.where(own, q2, jnp.zeros_like(q2))))
    mem_parts = []
    for pair in range(n_pair):
        vt1 = _with_sum_rows(vmt_ref[pair * LANES:(pair + 1) * LANES, :])
        o_h = []
        for hh in range(2):
            st = scores[2 * pair + hh]
            p = jnp.exp2((st - jnp.max(st, axis=0, keepdims=True)).astype(BF16))
            acc = _dot(vt1, p)
            o_h.append(acc[:LANES] / acc[LANES:LANES + 1])
        mem_parts.append(jnp.where(chan < HEAD_DIM, o_h[0], o_h[1]).T)

    parts = [(g * DSW_GROUP_WIDTH, mix_parts[g]) for g in range(3)]
    parts += [(MIX_WIDTH + p * LANES, mem_parts[p]) for p in range(len(mem_parts))]
    ys = []
    for c0, val in parts:
        gt = gate_ref[:, c0:c0 + val.shape[1]].astype(F32)
        ys.append((val * (gt * (1.0 / (1.0 + jnp.exp(-gt))))).astype(BF16))
    acc = x_ref[...] + _dot(jnp.concatenate(ys, axis=1), w_ref[...].astype(BF16))
    if final:
        yn = acc * lax.rsqrt(jnp.mean(acc * acc, axis=-1, keepdims=True) + RMS_EPS)
        acc = yn * fg_ref[...]
    o_ref[...] = acc


def _post(mix_list, x, qkvm, gate, mem_kv, mem_vt, w_out, layer, final_g, *, dilated, batch, seq,
          tm=1024):
    T = x.shape[0]
    tiles_per_batch = seq // tm
    n_mix = len(mix_list)
    final = final_g is not None
    qm_block = (3 * MIX_WIDTH) // MEM_WIDTH
    in_specs = [pl.BlockSpec((tm, m.shape[1]), lambda i: (i, 0)) for m in mix_list]
    in_specs += [pl.BlockSpec((tm, D_MODEL), lambda i: (i, 0)),
                 pl.BlockSpec((tm, MEM_WIDTH), lambda i: (i, qm_block)),
                 pl.BlockSpec((tm, BRANCH), lambda i: (i, 0)),
                 pl.BlockSpec((None, None, N_MEM, 2 * MEM_WIDTH),
                              lambda i: (layer, i // tiles_per_batch, 0, 0)),
                 pl.BlockSpec((None, MEM_WIDTH, N_MEM), lambda i: (layer, 0, i // tiles_per_batch)),
                 pl.BlockSpec((None, BRANCH, D_MODEL), lambda i: (layer, 0, 0))]
    args = list(mix_list) + [x, qkvm, gate, mem_kv, mem_vt, w_out]
    if final:
        in_specs.append(pl.BlockSpec((1, D_MODEL), lambda i: (0, 0)))
        args.append(final_g)
    return pl.pallas_call(
        functools.partial(_post_kernel, n_mix=n_mix, dilated=dilated, final=final),
        grid=(T // tm,),
        in_specs=in_specs,
        out_specs=pl.BlockSpec((tm, D_MODEL), lambda i: (i, 0)),
        out_shape=jax.ShapeDtypeStruct((T, D_MODEL), F32),
        compiler_params=_compiler_params(1),
        name="post",
    )(*args)


def kernel(x, mem, norm_g, w_in, w_out, mem_norm_g, w_mem_kv, diff_lambda_q1, diff_lambda_k1,
           diff_lambda_q2, diff_lambda_k2, diff_subln_g, final_norm_g):
    batch, seq, d = x.shape
    depth = w_in.shape[0]
    T = batch * seq
    xf = x.reshape(T, d)
    memf = mem.reshape(batch * mem.shape[1], d)

    norm_g3 = norm_g.reshape(depth, 1, d)
    mem_norm_g3 = mem_norm_g.reshape(depth, 1, d)
    scale = HEAD_DIM ** -0.5
    v_cols = (2 * MIX_WIDTH, 3 * MIX_WIDTH)
    mem_kv, mem_vt = _mem_proj_all(memf, mem_norm_g3, w_mem_kv)
    mem_kv = mem_kv.reshape(depth, batch, N_MEM, 2 * MEM_WIDTH)

    for i in range(depth):
        kind = i % N_MIXERS
        q_scale = scale if kind == 0 else scale * LOG2E
        scales = [(0, MIX_WIDTH, q_scale), (3 * MIX_WIDTH, QKVM_WIDTH, scale * LOG2E)]
        kmean_cols = (MIX_WIDTH, 2 * MIX_WIDTH) if kind == 1 else None
        proj = _rms_proj(xf, norm_g3, w_in, i, [(0, QKVM_WIDTH), (QKVM_WIDTH, IN_WIDTH)],
                         [F32 if kind == 0 else BF16, BF16], tm=1024, scales=scales,
                         vt_cols=None if kind == 0 else v_cols, kmean_cols=kmean_cols)
        qkvm, gate = proj[0], proj[1]
        if kind == 0:
            res = [_dilated_group(qkvm, g, batch=batch, seq=seq) for g in range(len(DSW_GROUPS))]
            mix_list = [r[0] for r in res] + [r[1] for r in res]
        elif kind == 1:
            mix_list = [_moba_mixer(qkvm, proj[2], proj[3], batch=batch, seq=seq)]
        else:
            c = i // N_MIXERS
            lambda_init = 0.8 - 0.6 * math.exp(-0.3 * i)
            lam_vecs = jnp.stack([diff_lambda_q1[c], diff_lambda_k1[c],
                                  diff_lambda_q2[c], diff_lambda_k2[c]]).astype(F32)
            mix_list = [_diff_mixer(qkvm, proj[2], lam_vecs, diff_subln_g[c], lambda_init,
                                    batch=batch, seq=seq)]
        fg = final_norm_g.reshape(1, d) if i == depth - 1 else None
        xf = _post(mix_list, xf, qkvm, gate, mem_kv, mem_vt, w_out, i, fg,
                   dilated=(kind == 0), batch=batch, seq=seq)
    return xf.reshape(batch, seq, d)
```
